```python
import math
import jax, jax.numpy as jnp
from jax import lax
import numpy as np

D_MODEL = 1024
BATCH = 8
SEQ = 4096
DEPTH = 1

CHUNK = 64
Q_BLOCK = 128
HEAD_DIM = 64
FOX_HEADS = 8
FOX_WIDTH = FOX_HEADS * HEAD_DIM
DIFF_HEADS = 4
DIFF_V_DIM = 2 * HEAD_DIM
DIFF_QK_WIDTH = DIFF_HEADS * 2 * HEAD_DIM
DIFF_WIDTH = DIFF_HEADS * DIFF_V_DIM
MIX_WIDTH = FOX_WIDTH + DIFF_WIDTH
IN_COLS = 3 * FOX_WIDTH + FOX_HEADS + 2 * DIFF_QK_WIDTH + DIFF_WIDTH
N_GROUPS = 4
EXPERTS_PER_GROUP = 8
N_EXPERTS = N_GROUPS * EXPERTS_PER_GROUP
TOP_K_IN_GROUP = 2
D_EXPERT = 512
DISPATCH_BLOCK = 256
NORM_EPS = 1e-6
SUBLN_EPS = 1e-5
FORGET_BIAS_MEAN = 3.0

kernel_name = 'hybrid_fox_diffattn_hmoe_adaln'


def rms_norm(x, g, eps=NORM_EPS):
    xf = x.astype(jnp.float32)
    y = xf * lax.rsqrt(jnp.mean(xf * xf, axis=-1, keepdims=True) + eps)
    return (y * g.astype(jnp.float32)).astype(x.dtype)


def alibi_slopes(n):
    return jnp.asarray([2.0 ** (-8.0 * (i + 1) / n) for i in range(n)], dtype=jnp.float32)


def forgetting_attention(q, k, v, log_f_cum):
    seq = q.shape[1]
    scale = HEAD_DIM ** -0.5
    cum = jnp.swapaxes(log_f_cum, 1, 2)
    outs = []
    for q0 in range(0, seq, Q_BLOCK):
        q1 = q0 + Q_BLOCK
        s = jnp.einsum('bqhd,bkhd->bhqk', q[:, q0:q1], k[:, :q1],
                       preferred_element_type=jnp.float32) * scale
        s = s + (cum[:, :, q0:q1, None] - cum[:, :, None, :q1])
        causal = jnp.arange(q0, q1)[:, None] >= jnp.arange(q1)[None, :]
        p = jax.nn.softmax(jnp.where(causal, s, -jnp.inf), axis=-1)
        outs.append(jnp.einsum('bhqk,bkhd->bqhd', p.astype(v.dtype), v[:, :q1]))
    return jnp.concatenate(outs, axis=1)


def differential_attention(q, k, v, lam):
    seq = q.shape[1]
    scale = HEAD_DIM ** -0.5
    slopes = alibi_slopes(DIFF_HEADS)
    outs = []
    for q0 in range(0, seq, Q_BLOCK):
        q1 = q0 + Q_BLOCK
        tq = jnp.arange(q0, q1)
        tk = jnp.arange(q1)
        s = jnp.einsum('bqhmd,bkhmd->bhmqk', q[:, q0:q1], k[:, :q1],
                       preferred_element_type=jnp.float32) * scale
        dist = jnp.abs(tq[:, None] - tk[None, :]).astype(jnp.float32)
        s = s - slopes[:, None, None, None] * dist
        chunk_ok = (tq // CHUNK)[:, None] >= (tk // CHUNK)[None, :]
        p = jax.nn.softmax(jnp.where(chunk_ok, s, -jnp.inf), axis=-1)
        a = p[:, :, 0] - lam * p[:, :, 1]
        outs.append(jnp.einsum('bhqk,bkhe->bqhe', a.astype(v.dtype), v[:, :q1]))
    return jnp.concatenate(outs, axis=1)


def hybrid_mixer(h, w_in, b_f, lam_q1, lam_k1, lam_q2, lam_k2, subln_g, w_o, lam_init):
    b, s, _ = h.shape
    proj = h @ w_in
    cuts = [FOX_WIDTH, 2 * FOX_WIDTH, 3 * FOX_WIDTH, 3 * FOX_WIDTH + FOX_HEADS,
            3 * FOX_WIDTH + FOX_HEADS + DIFF_QK_WIDTH,
            3 * FOX_WIDTH + FOX_HEADS + 2 * DIFF_QK_WIDTH]
    fq, fk, fv, fz, dq, dk, dv = jnp.split(proj, cuts, axis=-1)
    log_f_cum = jnp.cumsum(jax.nn.log_sigmoid((fz + b_f).astype(jnp.float32)), axis=1)
    fox = forgetting_attention(fq.reshape(b, s, FOX_HEADS, HEAD_DIM),
                               fk.reshape(b, s, FOX_HEADS, HEAD_DIM),
                               fv.reshape(b, s, FOX_HEADS, HEAD_DIM), log_f_cum)
    fox = fox.reshape(b, s, FOX_WIDTH)
    f32 = jnp.float32
    lam = (jnp.exp(jnp.sum(lam_q1.astype(f32) * lam_k1.astype(f32)))
           - jnp.exp(jnp.sum(lam_q2.astype(f32) * lam_k2.astype(f32))) + lam_init)
    diff = differential_attention(dq.reshape(b, s, DIFF_HEADS, 2, HEAD_DIM),
                                  dk.reshape(b, s, DIFF_HEADS, 2, HEAD_DIM),
                                  dv.reshape(b, s, DIFF_HEADS, DIFF_V_DIM), lam)
    diff = (rms_norm(diff, subln_g, SUBLN_EPS) * (1.0 - lam_init)).reshape(b, s, DIFF_WIDTH)
    return jnp.concatenate([fox, diff], axis=-1) @ w_o


def hierarchical_route(h, w_rg, b_rg, w_re, b_re):
    t = h.shape[0]
    hf = h.astype(jnp.float32)
    g_logits = hf @ w_rg.astype(jnp.float32) + b_rg.astype(jnp.float32)
    g_prob = jax.nn.softmax(g_logits, axis=-1)
    g = jnp.argmax(g_logits, axis=-1).astype(jnp.int32)
    p_g = jnp.take_along_axis(g_prob, g[:, None], axis=-1)
    e_logits = (hf @ w_re.astype(jnp.float32) + b_re.astype(jnp.float32)).reshape(
        t, N_GROUPS, EXPERTS_PER_GROUP)
    e_logits = jnp.take_along_axis(e_logits, g[:, None, None], axis=1)[:, 0]
    top_v, top_i = lax.top_k(e_logits, TOP_K_IN_GROUP)
    w = jax.nn.softmax(top_v, axis=-1) * p_g
    eid = g[:, None] * EXPERTS_PER_GROUP + top_i.astype(jnp.int32)
    tok = jnp.repeat(jnp.arange(t, dtype=jnp.int32), TOP_K_IN_GROUP)
    return eid.reshape(-1), tok, w.reshape(-1)


def sparse_experts(h, eid, tok, wt, w_gate, w_up, w_down):
    n_assign = eid.shape[0]
    order = jnp.argsort(eid)
    se = eid[order]
    counts = jnp.zeros((N_EXPERTS,), jnp.int32).at[eid].add(1)
    starts = jnp.cumsum(counts) - counts
    padded = (counts + DISPATCH_BLOCK - 1) // DISPATCH_BLOCK * DISPATCH_BLOCK
    pends = jnp.cumsum(padded)
    pstarts = pends - padded
    dest = pstarts[se] + jnp.arange(n_assign, dtype=jnp.int32) - starts[se]
    n_blocks = -(-n_assign // DISPATCH_BLOCK) + N_EXPERTS
    n_slots = n_blocks * DISPATCH_BLOCK
    slot_tok = jnp.zeros((n_slots,), jnp.int32).at[dest].set(tok[order])
    slot_w = jnp.zeros((n_slots,), h.dtype).at[dest].set(wt[order].astype(h.dtype))
    block_e = jnp.clip(jnp.searchsorted(pends, jnp.arange(n_blocks) * DISPATCH_BLOCK, side='right'),
                       0, N_EXPERTS - 1).astype(jnp.int32)

    def run_block(args):
        btok, bw, e = args
        xb = h[btok]
        hid = jax.nn.silu(xb @ w_gate[e]) * (xb @ w_up[e])
        return (hid @ w_down[e]) * bw[:, None]

    y = lax.map(run_block, (slot_tok.reshape(n_blocks, DISPATCH_BLOCK),
                            slot_w.reshape(n_blocks, DISPATCH_BLOCK), block_e))
    return jnp.zeros_like(h).at[slot_tok].add(y.reshape(n_slots, h.shape[-1]))


def setup_inputs(seed: int = 0) -> dict:
    key = jax.random.key(seed)
    ks = jax.random.split(key, 24)
    f32 = jnp.float32

    def nrm(k, shape, std):
        return jax.random.normal(k, shape, f32) * std

    d, L = D_MODEL, DEPTH
    return {
        'x': nrm(ks[0], (BATCH, SEQ, d), 1.0),
        'c': nrm(ks[1], (BATCH, d), 1.0),
        'ada_w': nrm(ks[2], (L, d, 6 * d), 0.5 * d ** -0.5),
        'ada_b': nrm(ks[3], (L, 6 * d), 0.02),
        'norm1_g': 1.0 + nrm(ks[4], (L, d), 0.1),
        'w_in': nrm(ks[5], (L, d, IN_COLS), d ** -0.5),
        'b_f': FORGET_BIAS_MEAN + nrm(ks[6], (L, FOX_HEADS), 0.5),
        'lam_q1': nrm(ks[7], (L, HEAD_DIM), 0.1),
        'lam_k1': nrm(ks[8], (L, HEAD_DIM), 0.1),
        'lam_q2': nrm(ks[9], (L, HEAD_DIM), 0.1),
        'lam_k2': nrm(ks[10], (L, HEAD_DIM), 0.1),
        'subln_g': 1.0 + nrm(ks[11], (L, DIFF_V_DIM), 0.1),
        'w_o': nrm(ks[12], (L, MIX_WIDTH, d), MIX_WIDTH ** -0.5),
        'norm2_g': 1.0 + nrm(ks[13], (L, d), 0.1),
        'w_rg': nrm(ks[14], (L, d, N_GROUPS), d ** -0.5),
        'b_rg': nrm(ks[15], (L, N_GROUPS), 0.01),
        'w_re': nrm(ks[16], (L, d, N_EXPERTS), d ** -0.5),
        'b_re': nrm(ks[17], (L, N_EXPERTS), 0.01),
        'w_gate': nrm(ks[18], (L, N_EXPERTS, d, D_EXPERT), d ** -0.5),
        'w_up': nrm(ks[19], (L, N_EXPERTS, d, D_EXPERT), d ** -0.5),
        'w_down': nrm(ks[20], (L, N_EXPERTS, D_EXPERT, d), D_EXPERT ** -0.5),
        'norm_f_g': 1.0 + nrm(ks[21], (d,), 0.1),
    }


def reference(x, c, ada_w, ada_b, norm1_g, w_in, b_f, lam_q1, lam_k1, lam_q2, lam_k2,
              subln_g, w_o, norm2_g, w_rg, b_rg, w_re, b_re, w_gate, w_up, w_down, norm_f_g):
    b, s, d = x.shape
    for l in range(DEPTH):
        lam_init = 0.8 - 0.6 * math.exp(-0.3 * l)
        mod = (c @ ada_w[l] + ada_b[l]).reshape(b, 6, d)
        shift1, scale1, gate1, shift2, scale2, gate2 = [mod[:, i, None, :] for i in range(6)]
        h = rms_norm(x, norm1_g[l]) * (1.0 + scale1) + shift1
        x = x + gate1 * hybrid_mixer(h, w_in[l], b_f[l], lam_q1[l], lam_k1[l], lam_q2[l],
                                     lam_k2[l], subln_g[l], w_o[l], lam_init)
        h = (rms_norm(x, norm2_g[l]) * (1.0 + scale2) + shift2).reshape(b * s, d)
        eid, tok, wt = hierarchical_route(h, w_rg[l], b_rg[l], w_re[l], b_re[l])
        moe = sparse_experts(h, eid, tok, wt, w_gate[l], w_up[l], w_down[l])
        x = x + gate2 * moe.reshape(b, s, d)
    return rms_norm(x, norm_f_g)
```

```python
import functools
import math

import jax
import jax.numpy as jnp
from jax import lax
from jax.experimental import pallas as pl
from jax.experimental.pallas import tpu as pltpu

f32 = jnp.float32
bf16 = jnp.bfloat16
i32 = jnp.int32

D_MODEL = 1024
HEAD_DIM = 64
FOX_HEADS = 8
FOX_WIDTH = FOX_HEADS * HEAD_DIM
DIFF_HEADS = 4
DIFF_QK_WIDTH = DIFF_HEADS * 2 * HEAD_DIM
DIFF_WIDTH = DIFF_HEADS * 2 * HEAD_DIM
CHUNK = 64
N_GROUPS = 4
EXPERTS_PER_GROUP = 8
N_EXPERTS = N_GROUPS * EXPERTS_PER_GROUP
D_EXPERT = 512
NORM_EPS = 1e-6
SUBLN_EPS = 1e-5

LANES = 128
LOG2E = 1.4426950408889634
Q_SCALE = HEAD_DIM ** -0.5 * LOG2E
NEG = -1e30
HALF = D_MODEL // 2

ROW_TILE = 512
ATT_TQ = 256
ATT_TK = 256
MOE_BLOCK = 256
GATHER_TILE = 256
VMEM_LIMIT = 48 * 1024 * 1024


def _cparams(*sem):
    return pltpu.CompilerParams(dimension_semantics=sem, vmem_limit_bytes=VMEM_LIMIT)


def _split3(c):
    hi = c.astype(bf16).astype(f32)
    r = c - hi
    mid = r.astype(bf16).astype(f32)
    lo = r - mid
    return hi, mid, lo


def _pack_rows(y):
    a = pltpu.bitcast(y[:, :HALF].astype(bf16).astype(f32), i32)
    b = pltpu.bitcast(y[:, HALF:].astype(bf16).astype(f32), i32)
    return lax.shift_right_logical(a, 16) | (b & jnp.int32(-65536))


def _unpack_rows(w):
    lo = pltpu.bitcast(lax.shift_left(w, 16), f32)
    hi = pltpu.bitcast(w & jnp.int32(-65536), f32)
    return lo, hi


def _ada_kernel(c_ref, w_ref, b_ref, o_ref):
    c = c_ref[...]
    w = w_ref[...]
    c_hi = c.astype(bf16)
    c_lo = (c - c_hi.astype(f32)).astype(bf16)
    w_hi = w.astype(bf16)
    w_lo = (w - w_hi.astype(f32)).astype(bf16)
    acc = jnp.dot(c_hi, w_hi, preferred_element_type=f32)
    acc += jnp.dot(c_hi, w_lo, preferred_element_type=f32)
    acc += jnp.dot(c_lo, w_hi, preferred_element_type=f32)
    o_ref[...] = acc + b_ref[...]


def _ada(c, w, b):
    bsz, d = c.shape
    n = w.shape[1]
    tn = 1024
    return pl.pallas_call(
        _ada_kernel,
        grid=(n // tn,),
        in_specs=[pl.BlockSpec((bsz, d), lambda j: (0, 0)),
                  pl.BlockSpec((d, tn), lambda j: (0, j)),
                  pl.BlockSpec((1, tn), lambda j: (0, j))],
        out_specs=pl.BlockSpec((bsz, tn), lambda j: (0, j)),
        out_shape=jax.ShapeDtypeStruct((bsz, n), f32),
        compiler_params=_cparams("parallel"),
        name="ada_mod",
    )(c, w, b.reshape(1, n))


def _inproj_kernel(x_ref, mod_ref, g_ref, wm_ref, wz_ref, bf_ref, tril_ref,
                   proj_ref, cum_ref, carry_ref, *, tiles_per_batch):
    i = pl.program_id(0)
    x = x_ref[...]
    ms = jnp.mean(x * x, axis=-1, keepdims=True)
    y = x * lax.rsqrt(ms + NORM_EPS) * g_ref[...]
    h = (y * (1.0 + mod_ref[0, 1:2, :]) + mod_ref[0, 0:1, :]).astype(bf16)
    n_chunks = proj_ref.shape[1] // 512
    for j in range(n_chunks):
        acc = jnp.dot(h, wm_ref[:, j * 512:(j + 1) * 512], preferred_element_type=f32)
        if j in (0, 3):
            acc = acc * Q_SCALE
        proj_ref[:, j * 512:(j + 1) * 512] = acc.astype(bf16)
    fz = jnp.dot(h, wz_ref[...], preferred_element_type=f32) + bf_ref[...]
    ls = (jnp.minimum(fz, 0.0) - jnp.log(1.0 + jnp.exp(-jnp.abs(fz)))) * LOG2E
    hi, mid, lo = _split3(ls)
    tril = tril_ref[...]
    local = jnp.dot(tril, hi.astype(bf16), preferred_element_type=f32)
    local += jnp.dot(tril, mid.astype(bf16), preferred_element_type=f32)
    local += jnp.dot(tril, lo.astype(bf16), preferred_element_type=f32)

    @pl.when(i % tiles_per_batch == 0)
    def _():
        carry_ref[...] = jnp.zeros_like(carry_ref)

    cum = local + carry_ref[0:1, :]
    cum_ref[...] = cum
    tm = x.shape[0]
    carry_ref[0:1, :] = cum[tm - 1:tm, :]


def _inproj(x2, mod, g1, w_main, w_fz, b_fz, seq):
    t, d = x2.shape
    tm = ROW_TILE
    tiles_per_batch = seq // tm
    n_main = w_main.shape[1]
    tril = (jnp.arange(tm)[:, None] >= jnp.arange(tm)[None, :]).astype(bf16)
    return pl.pallas_call(
        functools.partial(_inproj_kernel, tiles_per_batch=tiles_per_batch),
        grid=(t // tm,),
        in_specs=[pl.BlockSpec((tm, d), lambda i: (i, 0)),
                  pl.BlockSpec((1, 6, d), lambda i: (i // tiles_per_batch, 0, 0)),
                  pl.BlockSpec((1, d), lambda i: (0, 0)),
                  pl.BlockSpec((d, n_main), lambda i: (0, 0)),
                  pl.BlockSpec((d, LANES), lambda i: (0, 0)),
                  pl.BlockSpec((1, LANES), lambda i: (0, 0)),
                  pl.BlockSpec((tm, tm), lambda i: (0, 0))],
        out_specs=[pl.BlockSpec((tm, n_main), lambda i: (i, 0)),
                   pl.BlockSpec((tm, LANES), lambda i: (i, 0))],
        out_shape=[jax.ShapeDtypeStruct((t, n_main), bf16),
                   jax.ShapeDtypeStruct((t, LANES), f32)],
        scratch_shapes=[pltpu.VMEM((8, LANES), f32)],
        compiler_params=_cparams("arbitrary"),
        name="inproj",
    )(x2, mod, g1, w_main, w_fz, b_fz, tril)


def _aug(data, lane, low_map, first, last):
    base = 64 if low_map else 0
    out = jnp.zeros_like(data)
    for n, val in enumerate(tuple(first) + tuple(last)):
        out = jnp.where(lane == base + n, val, out)
    keep = (lane < 64) if low_map else (lane >= 64)
    return jnp.where(keep, data, out)


_ONES3 = (1.0, 1.0, 1.0)


def _q_aug(q, lane, low_map, c):
    return _aug(q, lane, low_map, _split3(c), _ONES3).astype(bf16)


def _k_aug(k, lane, low_map, c):
    hi, mid, lo = _split3(c)
    return _aug(k, lane, low_map, _ONES3, (-hi, -mid, -lo)).astype(bf16)


def _flash_pair(q_a, q_b, ka_s, kb_s, v_ref, mask, qi, tq, tk):
    nt = (((1,), (1,)), ((), ()))

    def one(qh, k_s, rows, v, m, l, acc, masked):
        s = lax.dot_general(qh, k_s[rows, :], nt, preferred_element_type=f32)
        if masked:
            s = s + mask
        m_new = jnp.maximum(m, jnp.max(s, axis=-1, keepdims=True))
        p = jnp.exp2(s - m_new)
        alpha = jnp.exp2(m - m_new)
        l = alpha * l + jnp.sum(p, axis=-1, keepdims=True)
        acc = alpha * acc + jnp.dot(p.astype(bf16), v, preferred_element_type=f32)
        return m_new, l, acc

    def step(j, carry, masked):
        m_a, l_a, acc_a, m_b, l_b, acc_b = carry
        rows = pl.ds(pl.multiple_of(j * tk, tk), tk)
        v = v_ref[0, rows, :]
        m_a, l_a, acc_a = one(q_a, ka_s, rows, v, m_a, l_a, acc_a, masked)
        m_b, l_b, acc_b = one(q_b, kb_s, rows, v, m_b, l_b, acc_b, masked)
        return m_a, l_a, acc_a, m_b, l_b, acc_b

    m0 = jnp.full((tq, 1), NEG, f32)
    l0 = jnp.zeros((tq, 1), f32)
    a0 = jnp.zeros((tq, LANES), f32)
    carry = lax.fori_loop(0, qi, lambda j, c: step(j, c, False), (m0, l0, a0, m0, l0, a0))
    _, l_a, acc_a, _, l_b, acc_b = step(qi, carry, True)
    return l_a, acc_a, l_b, acc_b


def _fox_kernel(q_ref, k_ref, v_ref, cum_ref, mask_ref, o_ref, ka_s, kb_s, *, seq, tq, tk):
    hp = pl.program_id(1)
    qi = pl.program_id(2)
    lane = lax.broadcasted_iota(i32, (1, LANES), 1)

    def head_cums(cm):
        c_a = jnp.sum(jnp.where(lane == 2 * hp, cm, 0.0), axis=-1, keepdims=True)
        c_b = jnp.sum(jnp.where(lane == 2 * hp + 1, cm, 0.0), axis=-1, keepdims=True)
        return c_a, c_b

    @pl.when(qi == 0)
    def _():
        for c in range(seq // 512):
            rows = pl.ds(c * 512, 512)
            kk = k_ref[0, rows, :].astype(f32)
            c_a, c_b = head_cums(cum_ref[0, rows, :])
            ka_s[rows, :] = _k_aug(kk, lane, True, c_a)
            kb_s[rows, :] = _k_aug(kk, lane, False, c_b)

    q = q_ref[0].astype(f32)
    c_a, c_b = head_cums(cum_ref[0, pl.ds(pl.multiple_of(qi * tq, tq), tq), :])
    q_a = _q_aug(q, lane, True, c_a)
    q_b = _q_aug(q, lane, False, c_b)
    l_a, acc_a, l_b, acc_b = _flash_pair(q_a, q_b, ka_s, kb_s, v_ref, mask_ref[...], qi, tq, tk)
    o_ref[0] = jnp.where(lane < 64, acc_a / l_a, acc_b / l_b).astype(bf16)


def _fox_attention(proj3, cum3):
    bsz, seq, _ = proj3.shape
    tq, tk = ATT_TQ, ATT_TK
    n_pairs = FOX_HEADS // 2
    r = jnp.arange(tq)
    mask = jnp.where(r[:, None] >= r[None, :], 0.0, NEG).astype(f32)
    return pl.pallas_call(
        functools.partial(_fox_kernel, seq=seq, tq=tq, tk=tk),
        grid=(bsz, n_pairs, seq // tq),
        in_specs=[pl.BlockSpec((1, tq, LANES), lambda b, h, q: (b, q, h)),
                  pl.BlockSpec((1, seq, LANES), lambda b, h, q: (b, 0, 4 + h)),
                  pl.BlockSpec((1, seq, LANES), lambda b, h, q: (b, 0, 8 + h)),
                  pl.BlockSpec((1, seq, LANES), lambda b, h, q: (b, 0, 0)),
                  pl.BlockSpec((tq, tk), lambda b, h, q: (0, 0))],
        out_specs=pl.BlockSpec((1, tq, LANES), lambda b, h, q: (b, q, h)),
        out_shape=jax.ShapeDtypeStruct((bsz, seq, FOX_WIDTH), bf16),
        scratch_shapes=[pltpu.VMEM((seq, LANES), bf16), pltpu.VMEM((seq, LANES), bf16)],
        compiler_params=_cparams("parallel", "parallel", "arbitrary"),
        name="fox_attn",
    )(proj3, proj3, proj3, cum3, mask)


def _diff_kernel(slope_ref, q_ref, k_ref, v_ref, mask_ref, lamv_ref, g_ref, o_ref,
                 ka_s, kb_s, *, seq, tq, tk, lam_init):
    h = pl.program_id(1)
    qi = pl.program_id(2)
    lane = lax.broadcasted_iota(i32, (1, LANES), 1)
    slope = slope_ref[h]

    def pos_bias(start, n):
        pos = (start + lax.broadcasted_iota(i32, (n, 1), 0)).astype(f32)
        return -(slope * pos)

    @pl.when(qi == 0)
    def _():
        for c in range(seq // 512):
            rows = pl.ds(c * 512, 512)
            kk = k_ref[0, rows, :].astype(f32)
            cb = pos_bias(c * 512, 512)
            ka_s[rows, :] = _k_aug(kk, lane, True, cb)
            kb_s[rows, :] = _k_aug(kk, lane, False, cb)

    q = q_ref[0].astype(f32)
    cq = pos_bias(qi * tq, tq)
    q_a = _q_aug(q, lane, True, cq)
    q_b = _q_aug(q, lane, False, cq)
    l_a, acc_a, l_b, acc_b = _flash_pair(q_a, q_b, ka_s, kb_s, v_ref, mask_ref[0], qi, tq, tk)
    lv = lamv_ref[...]
    s1 = jnp.sum(lv[0:1, :] * lv[1:2, :], axis=-1, keepdims=True)
    s2 = jnp.sum(lv[2:3, :] * lv[3:4, :], axis=-1, keepdims=True)
    lam = jnp.exp(s1) - jnp.exp(s2) + lam_init
    d = acc_a / l_a - lam * (acc_b / l_b)
    y = d * lax.rsqrt(jnp.mean(d * d, axis=-1, keepdims=True) + SUBLN_EPS) * g_ref[...]
    o_ref[0] = (y * (1.0 - lam_init)).astype(bf16)


def _diff_attention(proj3, lam_vecs, subln_g, lam_init):
    bsz, seq, _ = proj3.shape
    tq, tk = ATT_TQ, ATT_TK
    slopes = jnp.asarray([2.0 ** (-8.0 * (i + 1) / DIFF_HEADS) for i in range(DIFF_HEADS)], f32) * LOG2E
    r = jnp.arange(tq)
    tq_i, tk_i = r[:, None], r[None, :]
    chunk_ok = (tq_i // CHUNK) >= (tk_i // CHUNK)
    ahead = jnp.maximum(tk_i - tq_i, 0).astype(f32)
    mask = jnp.where(chunk_ok[None], -2.0 * slopes[:, None, None] * ahead[None], NEG).astype(f32)
    grid_spec = pltpu.PrefetchScalarGridSpec(
        num_scalar_prefetch=1,
        grid=(bsz, DIFF_HEADS, seq // tq),
        in_specs=[pl.BlockSpec((1, tq, LANES), lambda b, h, q, s: (b, q, 12 + h)),
                  pl.BlockSpec((1, seq, LANES), lambda b, h, q, s: (b, 0, 16 + h)),
                  pl.BlockSpec((1, seq, LANES), lambda b, h, q, s: (b, 0, 20 + h)),
                  pl.BlockSpec((1, tq, tk), lambda b, h, q, s: (h, 0, 0)),
                  pl.BlockSpec((8, LANES), lambda b, h, q, s: (0, 0)),
                  pl.BlockSpec((1, LANES), lambda b, h, q, s: (0, 0))],
        out_specs=pl.BlockSpec((1, tq, LANES), lambda b, h, q, s: (b, q, h)),
        scratch_shapes=[pltpu.VMEM((seq, LANES), bf16), pltpu.VMEM((seq, LANES), bf16)],
    )
    return pl.pallas_call(
        functools.partial(_diff_kernel, seq=seq, tq=tq, tk=tk, lam_init=lam_init),
        grid_spec=grid_spec,
        out_shape=jax.ShapeDtypeStruct((bsz, seq, DIFF_WIDTH), bf16),
        compiler_params=_cparams("parallel", "parallel", "arbitrary"),
        name="diff_attn",
    )(slopes, proj3, proj3, proj3, mask, lam_vecs, subln_g)


ROUTER_ROWS = 8 + N_EXPERTS


def _outproj_kernel(fox_ref, diff_ref, x_ref, mod_ref, g_ref, wo_ref, wr_hi_ref, wr_lo_ref,
                    br_ref, triu_ref, x1_ref, hp_ref, route_ref, wcol_ref, cnt_ref):
    tm = x_ref.shape[0]
    y = jnp.dot(fox_ref[...], wo_ref[0:FOX_WIDTH, :], preferred_element_type=f32)
    y += jnp.dot(diff_ref[...], wo_ref[FOX_WIDTH:, :], preferred_element_type=f32)
    x1 = x_ref[...] + mod_ref[0, 2:3, :] * y
    x1_ref[...] = x1
    ms = jnp.mean(x1 * x1, axis=-1, keepdims=True)
    h2 = (x1 * lax.rsqrt(ms + NORM_EPS) * g_ref[...]) * (1.0 + mod_ref[0, 4:5, :]) + mod_ref[0, 3:4, :]
    hp_ref[...] = _pack_rows(h2)

    nt = (((1,), (1,)), ((), ()))
    h_hi = h2.astype(bf16)
    h_lo = (h2 - h_hi.astype(f32)).astype(bf16)
    logits = lax.dot_general(wr_hi_ref[...], h_hi, nt, preferred_element_type=f32)
    logits += lax.dot_general(wr_lo_ref[...], h_hi, nt, preferred_element_type=f32)
    logits += lax.dot_general(wr_hi_ref[...], h_lo, nt, preferred_element_type=f32)
    logits = logits + br_ref[...]

    row8 = lax.broadcasted_iota(i32, (8, tm), 0)
    gl = jnp.where(row8 < N_GROUPS, logits[0:8, :], NEG)
    gmax = jnp.max(gl, axis=0, keepdims=True)
    grp = jnp.min(jnp.where(gl == gmax, row8, 8), axis=0, keepdims=True)
    p_g = 1.0 / jnp.sum(jnp.exp(gl - gmax), axis=0, keepdims=True)
    sel = logits[8:16, :]
    for g in range(1, N_GROUPS):
        sel = jnp.where(grp == g, logits[8 + 8 * g:16 + 8 * g, :], sel)
    v1 = jnp.max(sel, axis=0, keepdims=True)
    i1 = jnp.min(jnp.where(sel == v1, row8, 8), axis=0, keepdims=True)
    sel2 = jnp.where(row8 == i1, -jnp.inf, sel)
    v2 = jnp.max(sel2, axis=0, keepdims=True)
    i2 = jnp.min(jnp.where(sel2 == v2, row8, 8), axis=0, keepdims=True)
    e21 = jnp.exp(v2 - v1)
    w1 = p_g / (1.0 + e21)
    w2 = w1 * e21
    e1 = grp * EXPERTS_PER_GROUP + i1
    e2 = grp * EXPERTS_PER_GROUP + i2

    row32 = lax.broadcasted_iota(i32, (N_EXPERTS, tm), 0)
    oh1 = row32 == e1
    oh2 = row32 == e2
    both = jnp.where(oh1 | oh2, 1.0, 0.0)
    prefix = jnp.dot(both.astype(bf16), triu_ref[...], preferred_element_type=f32)
    r1 = jnp.sum(jnp.where(oh1, prefix, 0.0), axis=0, keepdims=True).astype(i32)
    r2 = jnp.sum(jnp.where(oh2, prefix, 0.0), axis=0, keepdims=True).astype(i32)
    route = jnp.where(row8 == 0, e1, jnp.where(row8 == 1, e2,
                      jnp.where(row8 == 2, r1, jnp.where(row8 == 3, r2, 0))))
    route_ref[...] = route
    cnt = jnp.sum(both, axis=1, keepdims=True)
    cnt_ref[0] = jnp.broadcast_to(cnt, (N_EXPERTS, LANES)).astype(i32)
    row128 = lax.broadcasted_iota(i32, (LANES, tm), 0)
    w_rows = jnp.where(row128 == 0, w1, jnp.where(row128 == 1, w2, 0.0))
    wcol_ref[...] = w_rows.T


def _outproj(fox2, diff2, x2, mod, g2, w_o, wr_hi, wr_lo, b_r, seq):
    t, d = x2.shape
    tm = ROW_TILE
    tiles_per_batch = seq // tm
    n_tiles = t // tm
    triu = (jnp.arange(tm)[:, None] < jnp.arange(tm)[None, :]).astype(bf16)
    row = lambda i: (i, 0)
    const = lambda i: (0, 0)
    return pl.pallas_call(
        _outproj_kernel,
        grid=(n_tiles,),
        in_specs=[pl.BlockSpec((tm, FOX_WIDTH), row),
                  pl.BlockSpec((tm, DIFF_WIDTH), row),
                  pl.BlockSpec((tm, d), row),
                  pl.BlockSpec((1, 6, d), lambda i: (i // tiles_per_batch, 0, 0)),
                  pl.BlockSpec((1, d), const),
                  pl.BlockSpec((d, d), const),
                  pl.BlockSpec((ROUTER_ROWS, d), const),
                  pl.BlockSpec((ROUTER_ROWS, d), const),
                  pl.BlockSpec((ROUTER_ROWS, 1), const),
                  pl.BlockSpec((tm, tm), const)],
        out_specs=[pl.BlockSpec((tm, d), row),
                   pl.BlockSpec((tm, HALF), row),
                   pl.BlockSpec((8, tm), lambda i: (0, i)),
                   pl.BlockSpec((tm, LANES), row),
                   pl.BlockSpec((1, N_EXPERTS, LANES), lambda i: (i, 0, 0))],
        out_shape=[jax.ShapeDtypeStruct((t, d), f32),
                   jax.ShapeDtypeStruct((t, HALF), i32),
                   jax.ShapeDtypeStruct((8, t), i32),
                   jax.ShapeDtypeStruct((t, LANES), f32),
                   jax.ShapeDtypeStruct((n_tiles, N_EXPERTS, LANES), i32)],
        compiler_params=_cparams("parallel"),
        name="outproj_router",
    )(fox2, diff2, x2, mod, g2, w_o, wr_hi, wr_lo, b_r, triu)


def _dispatch_kernel(dest_ref, h_ref, xs_in_ref, xs_ref, sem, *, tm):
    del xs_in_ref
    i = pl.program_id(0)
    base = i * (2 * tm)

    def row_copy(r, k):
        d = dest_ref[base + 2 * r + k]
        return pltpu.make_async_copy(h_ref.at[pl.ds(r, 1), :], xs_ref.at[pl.ds(d, 1), :], sem)

    def issue(r, _):
        row_copy(r, 0).start()
        row_copy(r, 1).start()
        return 0

    lax.fori_loop(0, tm, issue, 0, unroll=8)

    def drain(r, _):
        row_copy(r, 0).wait()
        row_copy(r, 1).wait()
        return 0

    lax.fori_loop(0, tm, drain, 0, unroll=8)


def _dispatch(dest, h_packed, n_slots):
    t = h_packed.shape[0]
    tm = GATHER_TILE
    xs0 = jnp.zeros((n_slots, HALF), i32)
    grid_spec = pltpu.PrefetchScalarGridSpec(
        num_scalar_prefetch=1,
        grid=(t // tm,),
        in_specs=[pl.BlockSpec((tm, HALF), lambda i, d: (i, 0)),
                  pl.BlockSpec(memory_space=pl.ANY)],
        out_specs=pl.BlockSpec(memory_space=pl.ANY),
        scratch_shapes=[pltpu.SemaphoreType.DMA(())],
    )
    return pl.pallas_call(
        functools.partial(_dispatch_kernel, tm=tm),
        grid_spec=grid_spec,
        out_shape=jax.ShapeDtypeStruct((n_slots, HALF), i32),
        input_output_aliases={2: 0},
        compiler_params=_cparams("arbitrary"),
        name="moe_dispatch",
    )(dest, h_packed, xs0)


def _experts_kernel(be_ref, nu_ref, xs_ref, wg_ref, wu_ref, wd_ref, y_ref, wg_s, wu_s, wd_s):
    i = pl.program_id(0)
    prev = be_ref[jnp.maximum(i - 1, 0)]
    fresh = jnp.logical_or(i == 0, be_ref[i] != prev)

    @pl.when(jnp.logical_and(fresh, i < nu_ref[0]))
    def _():
        wg_s[...] = wg_ref[0].astype(bf16)
        wu_s[...] = wu_ref[0].astype(bf16)
        wd_s[...] = wd_ref[0].astype(bf16)

    @pl.when(i < nu_ref[0])
    def _():
        lo, hi = _unpack_rows(xs_ref[...])
        xb = jnp.concatenate([lo.astype(bf16), hi.astype(bf16)], axis=-1)
        g = jnp.dot(xb, wg_s[...], preferred_element_type=f32)
        u = jnp.dot(xb, wu_s[...], preferred_element_type=f32)
        hid = (g * (1.0 / (1.0 + jnp.exp(-g)))) * u
        y = jnp.dot(hid.astype(bf16), wd_s[...], preferred_element_type=f32)
        y_ref[...] = _pack_rows(y)

    @pl.when(i >= nu_ref[0])
    def _():
        y_ref[...] = jnp.zeros_like(y_ref)


def _experts(block_e, n_used, xs, w_gate, w_up, w_down):
    n_slots = xs.shape[0]
    bk = MOE_BLOCK
    grid_spec = pltpu.PrefetchScalarGridSpec(
        num_scalar_prefetch=2,
        grid=(n_slots // bk,),
        in_specs=[pl.BlockSpec((bk, HALF), lambda i, be, nu: (i, 0)),
                  pl.BlockSpec((1, D_MODEL, D_EXPERT), lambda i, be, nu: (be[i], 0, 0)),
                  pl.BlockSpec((1, D_MODEL, D_EXPERT), lambda i, be, nu: (be[i], 0, 0)),
                  pl.BlockSpec((1, D_EXPERT, D_MODEL), lambda i, be, nu: (be[i], 0, 0))],
        out_specs=pl.BlockSpec((bk, HALF), lambda i, be, nu: (i, 0)),
        scratch_shapes=[pltpu.VMEM((D_MODEL, D_EXPERT), bf16),
                        pltpu.VMEM((D_MODEL, D_EXPERT), bf16),
                        pltpu.VMEM((D_EXPERT, D_MODEL), bf16)],
    )
    return pl.pallas_call(
        _experts_kernel,
        grid_spec=grid_spec,
        out_shape=jax.ShapeDtypeStruct((n_slots, HALF), i32),
        compiler_params=_cparams("arbitrary"),
        name="moe_experts",
    )(block_e, n_used, xs, w_gate, w_up, w_down)


def _combine_kernel(dest_ref, y_ref, x1_ref, wcol_ref, mod_ref, g_ref, o_ref, buf, sems, *, tm):
    i = pl.program_id(0)
    n = pl.num_programs(0)

    def row_copy(tile, slot, r, k):
        d = dest_ref[tile * (2 * tm) + 2 * r + k]
        return pltpu.make_async_copy(y_ref.at[pl.ds(d, 1), :],
                                     buf.at[slot, k, pl.ds(r, 1), :], sems.at[slot])

    def issue_tile(tile, slot):
        def body(r, _):
            row_copy(tile, slot, r, 0).start()
            row_copy(tile, slot, r, 1).start()
            return 0
        lax.fori_loop(0, tm, body, 0, unroll=8)

    @pl.when(i == 0)
    def _():
        issue_tile(0, 0)

    @pl.when(i + 1 < n)
    def _():
        issue_tile(i + 1, (i + 1) % 2)

    slot = i % 2

    def drain(r, _):
        row_copy(i, slot, r, 0).wait()
        row_copy(i, slot, r, 1).wait()
        return 0

    lax.fori_loop(0, tm, drain, 0, unroll=8)

    wc = wcol_ref[...]
    w0 = wc[:, 0:1]
    w1 = wc[:, 1:2]
    lo0, hi0 = _unpack_rows(buf[slot, 0])
    lo1, hi1 = _unpack_rows(buf[slot, 1])
    moe = jnp.concatenate([w0 * lo0 + w1 * lo1, w0 * hi0 + w1 * hi1], axis=-1)
    x = x1_ref[...] + mod_ref[0, 5:6, :] * moe
    ms = jnp.mean(x * x, axis=-1, keepdims=True)
    o_ref[...] = x * lax.rsqrt(ms + NORM_EPS) * g_ref[...]


def _combine(dest, y_packed, x1, wcol, mod, g_f, seq):
    t, d = x1.shape
    tm = GATHER_TILE
    tiles_per_batch = seq // tm
    grid_spec = pltpu.PrefetchScalarGridSpec(
        num_scalar_prefetch=1,
        grid=(t // tm,),
        in_specs=[pl.BlockSpec(memory_space=pl.ANY),
                  pl.BlockSpec((tm, d), lambda i, ds: (i, 0)),
                  pl.BlockSpec((tm, LANES), lambda i, ds: (i, 0)),
                  pl.BlockSpec((1, 6, d), lambda i, ds: (i // tiles_per_batch, 0, 0)),
                  pl.BlockSpec((1, d), lambda i, ds: (0, 0))],
        out_specs=pl.BlockSpec((tm, d), lambda i, ds: (i, 0)),
        scratch_shapes=[pltpu.VMEM((2, 2, tm, HALF), i32),
                        pltpu.SemaphoreType.DMA((2,))],
    )
    return pl.pallas_call(
        functools.partial(_combine_kernel, tm=tm),
        grid_spec=grid_spec,
        out_shape=jax.ShapeDtypeStruct((t, d), f32),
        compiler_params=_cparams("arbitrary"),
        name="moe_combine",
    )(dest, y_packed, x1, wcol, mod, g_f)


def _route_tables(route, cnt, tm):
    n_tiles = cnt.shape[0]
    t = route.shape[1]
    bk = MOE_BLOCK
    n_blocks = (2 * t) // bk + N_EXPERTS
    tile_base = jnp.cumsum(cnt, axis=0) - cnt
    total = jnp.sum(cnt, axis=0)
    padded = (total + bk - 1) // bk * bk
    pends = jnp.cumsum(padded)
    base = (pends - padded)[None, :] + tile_base
    tile_of = jnp.arange(t, dtype=i32) // tm
    flat = base.reshape(-1)
    d0 = flat[tile_of * N_EXPERTS + route[0]] + route[2]
    d1 = flat[tile_of * N_EXPERTS + route[1]] + route[3]
    dest = jnp.stack([d0, d1], axis=1).reshape(-1).astype(i32)
    block_e = jnp.clip(jnp.searchsorted(pends, jnp.arange(n_blocks, dtype=i32) * bk, side='right'),
                       0, N_EXPERTS - 1).astype(i32)
    n_used = (pends[-1] // bk).astype(i32).reshape(1)
    return dest, block_e, n_used, n_blocks * bk


def _layer(x2, c, seq, l, ada_w, ada_b, norm1_g, w_in, b_f, lam_q1, lam_k1, lam_q2, lam_k2,
           subln_g, w_o, norm2_g, w_rg, b_rg, w_re, b_re, w_gate, w_up, w_down):
    t, d = x2.shape
    bsz = t // seq
    lam_init = 0.8 - 0.6 * math.exp(-0.3 * l)
    mod = _ada(c, ada_w, ada_b).reshape(bsz, 6, d)

    z0 = 3 * FOX_WIDTH
    w_main = jnp.concatenate([w_in[:, :z0], w_in[:, z0 + FOX_HEADS:]], axis=1).astype(bf16)
    w_fz = jnp.pad(w_in[:, z0:z0 + FOX_HEADS], ((0, 0), (0, LANES - FOX_HEADS))).astype(bf16)
    b_fz = jnp.pad(b_f, (0, LANES - FOX_HEADS)).reshape(1, LANES)
    proj, cum = _inproj(x2, mod, norm1_g.reshape(1, d), w_main, w_fz, b_fz, seq)
    proj3 = proj.reshape(bsz, seq, -1)

    fox = _fox_attention(proj3, cum.reshape(bsz, seq, LANES))
    lam_vecs = jnp.pad(jnp.stack([lam_q1, lam_k1, lam_q2, lam_k2]).astype(f32),
                       ((0, 4), (0, LANES - HEAD_DIM)))
    diff = _diff_attention(proj3, lam_vecs, subln_g.reshape(1, LANES), lam_init)

    w_r = jnp.concatenate([w_rg.T, jnp.zeros((8 - N_GROUPS, d), f32), w_re.T], axis=0)
    wr_hi = w_r.astype(bf16)
    wr_lo = (w_r - wr_hi.astype(f32)).astype(bf16)
    b_r = jnp.concatenate([b_rg, jnp.zeros((8 - N_GROUPS,), f32), b_re]).reshape(ROUTER_ROWS, 1)
    x1, h_packed, route, wcol, cnt = _outproj(
        fox.reshape(t, FOX_WIDTH), diff.reshape(t, DIFF_WIDTH), x2, mod, norm2_g.reshape(1, d),
        w_o.astype(bf16), wr_hi, wr_lo, b_r, seq)

    dest, block_e, n_used, n_slots = _route_tables(route, cnt[:, :, 0], ROW_TILE)
    xs = _dispatch(dest, h_packed, n_slots)
    y_packed = _experts(block_e, n_used, xs, w_gate, w_up, w_down)
    return dest, y_packed, x1, wcol, mod


def kernel(x, c, ada_w, ada_b, norm1_g, w_in, b_f, lam_q1, lam_k1, lam_q2, lam_k2, subln_g, w_o,
           norm2_g, w_rg, b_rg, w_re, b_re, w_gate, w_up, w_down, norm_f_g):
    bsz, seq, d = x.shape
    depth = ada_w.shape[0]
    assert depth == 1 and d == D_MODEL and seq % ROW_TILE == 0 and ATT_TQ == ATT_TK
    x2 = x.reshape(bsz * seq, d)
    dest, y_packed, x1, wcol, mod = _layer(
        x2, c, seq, 0, ada_w[0], ada_b[0], norm1_g[0], w_in[0], b_f[0], lam_q1[0], lam_k1[0],
        lam_q2[0], lam_k2[0], subln_g[0], w_o[0], norm2_g[0], w_rg[0], b_rg[0], w_re[0], b_re[0],
        w_gate[0], w_up[0], w_down[0])
    out = _combine(dest, y_packed, x1, wcol, mod, norm_f_g.reshape(1, d), seq)
    return out.reshape(bsz, seq, d)
```

```python
import functools
import math

import jax
import jax.numpy as jnp
from jax import lax
from jax.experimental import pallas as pl
from jax.experimental.pallas import tpu as pltpu

f32 = jnp.float32
bf16 = jnp.bfloat16
i32 = jnp.int32

D_MODEL = 1024
HEAD_DIM = 64
FOX_HEADS = 8
FOX_WIDTH = FOX_HEADS * HEAD_DIM
DIFF_HEADS = 4
DIFF_QK_WIDTH = DIFF_HEADS * 2 * HEAD_DIM
DIFF_WIDTH = DIFF_HEADS * 2 * HEAD_DIM
CHUNK = 64
N_GROUPS = 4
EXPERTS_PER_GROUP = 8
N_EXPERTS = N_GROUPS * EXPERTS_PER_GROUP
D_EXPERT = 512
NORM_EPS = 1e-6
SUBLN_EPS = 1e-5

LANES = 128
LOG2E = 1.4426950408889634
Q_SCALE = HEAD_DIM ** -0.5 * LOG2E
NEG = -1e30
HALF = D_MODEL // 2

ROW_TILE = 512
ATT_TQ = 512
ATT_WIDE = 1024
MOE_BLOCK = 256
GATHER_TILE = 256
VMEM_LIMIT = 48 * 1024 * 1024


def _cparams(*sem):
    return pltpu.CompilerParams(dimension_semantics=sem, vmem_limit_bytes=VMEM_LIMIT)


def _split3(c):
    hi = c.astype(bf16).astype(f32)
    r = c - hi
    mid = r.astype(bf16).astype(f32)
    lo = r - mid
    return hi, mid, lo


def _pack_rows(y):
    a = pltpu.bitcast(y[:, :HALF].astype(bf16).astype(f32), i32)
    b = pltpu.bitcast(y[:, HALF:].astype(bf16).astype(f32), i32)
    return lax.shift_right_logical(a, 16) | (b & jnp.int32(-65536))


def _unpack_rows(w):
    lo = pltpu.bitcast(lax.shift_left(w, 16), f32)
    hi = pltpu.bitcast(w & jnp.int32(-65536), f32)
    return lo, hi


def _ada_kernel(c_ref, w_ref, b_ref, o_ref):
    c = c_ref[...]
    w = w_ref[...]
    c_hi = c.astype(bf16)
    c_lo = (c - c_hi.astype(f32)).astype(bf16)
    w_hi = w.astype(bf16)
    w_lo = (w - w_hi.astype(f32)).astype(bf16)
    acc = jnp.dot(c_hi, w_hi, preferred_element_type=f32)
    acc += jnp.dot(c_hi, w_lo, preferred_element_type=f32)
    acc += jnp.dot(c_lo, w_hi, preferred_element_type=f32)
    o_ref[...] = acc + b_ref[...]


def _ada(c, w, b):
    bsz, d = c.shape
    n = w.shape[1]
    tn = 1024
    return pl.pallas_call(
        _ada_kernel,
        grid=(n // tn,),
        in_specs=[pl.BlockSpec((bsz, d), lambda j: (0, 0)),
                  pl.BlockSpec((d, tn), lambda j: (0, j)),
                  pl.BlockSpec((1, tn), lambda j: (0, j))],
        out_specs=pl.BlockSpec((bsz, tn), lambda j: (0, j)),
        out_shape=jax.ShapeDtypeStruct((bsz, n), f32),
        compiler_params=_cparams("parallel"),
        name="ada_mod",
    )(c, w, b.reshape(1, n))


def _inproj_kernel(x_ref, mod_ref, g_ref, wm_ref, wz_ref, bf_ref, tril_ref,
                   proj_ref, cum_ref, carry_ref, *, tiles_per_batch):
    i = pl.program_id(0)
    x = x_ref[...]
    ms = jnp.mean(x * x, axis=-1, keepdims=True)
    y = x * lax.rsqrt(ms + NORM_EPS) * g_ref[...]
    h = (y * (1.0 + mod_ref[0, 1:2, :]) + mod_ref[0, 0:1, :]).astype(bf16)
    n_chunks = proj_ref.shape[1] // 512
    for j in range(n_chunks):
        acc = jnp.dot(h, wm_ref[:, j * 512:(j + 1) * 512], preferred_element_type=f32)
        if j in (0, 3):
            acc = acc * Q_SCALE
        proj_ref[:, j * 512:(j + 1) * 512] = acc.astype(bf16)
    fz = jnp.dot(h, wz_ref[...], preferred_element_type=f32) + bf_ref[...]
    ls = (jnp.minimum(fz, 0.0) - jnp.log(1.0 + jnp.exp(-jnp.abs(fz)))) * LOG2E
    hi, mid, lo = _split3(ls)
    tril = tril_ref[...]
    local = jnp.dot(tril, hi.astype(bf16), preferred_element_type=f32)
    local += jnp.dot(tril, mid.astype(bf16), preferred_element_type=f32)
    local += jnp.dot(tril, lo.astype(bf16), preferred_element_type=f32)

    @pl.when(i % tiles_per_batch == 0)
    def _():
        carry_ref[...] = jnp.zeros_like(carry_ref)

    cum = local + carry_ref[0:1, :]
    cum_ref[...] = cum
    tm = x.shape[0]
    carry_ref[0:1, :] = cum[tm - 1:tm, :]


def _inproj(x2, mod, g1, w_main, w_fz, b_fz, seq):
    t, d = x2.shape
    tm = ROW_TILE
    tiles_per_batch = seq // tm
    n_main = w_main.shape[1]
    tril = (jnp.arange(tm)[:, None] >= jnp.arange(tm)[None, :]).astype(bf16)
    return pl.pallas_call(
        functools.partial(_inproj_kernel, tiles_per_batch=tiles_per_batch),
        grid=(t // tm,),
        in_specs=[pl.BlockSpec((tm, d), lambda i: (i, 0)),
                  pl.BlockSpec((1, 6, d), lambda i: (i // tiles_per_batch, 0, 0)),
                  pl.BlockSpec((1, d), lambda i: (0, 0)),
                  pl.BlockSpec((d, n_main), lambda i: (0, 0)),
                  pl.BlockSpec((d, LANES), lambda i: (0, 0)),
                  pl.BlockSpec((1, LANES), lambda i: (0, 0)),
                  pl.BlockSpec((tm, tm), lambda i: (0, 0))],
        out_specs=[pl.BlockSpec((tm, n_main), lambda i: (i, 0)),
                   pl.BlockSpec((tm, LANES), lambda i: (i, 0))],
        out_shape=[jax.ShapeDtypeStruct((t, n_main), bf16),
                   jax.ShapeDtypeStruct((t, LANES), f32)],
        scratch_shapes=[pltpu.VMEM((8, LANES), f32)],
        compiler_params=_cparams("arbitrary"),
        name="inproj",
    )(x2, mod, g1, w_main, w_fz, b_fz, tril)


def _aug(data, lane, low_map, first, last):
    base = 64 if low_map else 0
    out = jnp.zeros_like(data)
    for n, val in enumerate(tuple(first) + tuple(last)):
        out = jnp.where(lane == base + n, val, out)
    keep = (lane < 64) if low_map else (lane >= 64)
    return jnp.where(keep, data, out)


_ONES3 = (1.0, 1.0, 1.0)


def _q_aug(q, lane, low_map, c):
    return _aug(q, lane, low_map, _split3(c), _ONES3).astype(bf16)


def _k_aug(k, lane, low_map, c):
    hi, mid, lo = _split3(c)
    return _aug(k, lane, low_map, _ONES3, (-hi, -mid, -lo)).astype(bf16)


def _v_aug(v):
    lane = lax.broadcasted_iota(i32, (1, LANES), 1)
    ones_lane = jnp.where(lane == 0, 1.0, 0.0).astype(v.dtype)
    return jnp.concatenate([v, jnp.broadcast_to(ones_lane, v.shape)], axis=-1)


def _flash_scratch(tq):
    return ([pltpu.VMEM((tq, ATT_WIDE), f32)] * 4
            + [pltpu.VMEM((tq, 2 * LANES), f32)] * 2 + [pltpu.VMEM((tq, 1), f32)] * 2)


def _flash_pair(q_a, q_b, ka_s, kb_s, v_s, mask_ref, qi, tq, finish, scratch):
    s0a, s0b, s1a, s1b, acc_a, acc_b, m_a, m_b = scratch
    nt = (((1,), (1,)), ((), ()))
    wide = ATT_WIDE
    per_wide = wide // tq

    def scores_to(buf_a, buf_b, start):
        rows = pl.ds(pl.multiple_of(start, wide), wide)
        buf_a[...] = lax.dot_general(q_a, ka_s[rows, :], nt, preferred_element_type=f32)
        buf_b[...] = lax.dot_general(q_b, kb_s[rows, :], nt, preferred_element_type=f32)

    def update(buf, m_ref, acc_ref, start, width, mask):
        s = buf[:, :width]
        if mask is not None:
            s = s + mask
        v = v_s[pl.ds(pl.multiple_of(start, wide), width), :]
        m = m_ref[...]
        m_new = jnp.maximum(m, jnp.max(s, axis=-1, keepdims=True))
        p = jnp.exp2(s - m_new)
        acc_ref[...] = (jnp.exp2(m - m_new) * acc_ref[...]
                        + jnp.dot(p.astype(bf16), v, preferred_element_type=f32))
        m_ref[...] = m_new

    def update_both(buf_a, buf_b, start, width=wide, mask=None):
        update(buf_a, m_a, acc_a, start, width, mask)
        update(buf_b, m_b, acc_b, start, width, mask)

    m_a[...] = jnp.full(m_a.shape, NEG, f32)
    m_b[...] = jnp.full(m_b.shape, NEG, f32)
    acc_a[...] = jnp.zeros(acc_a.shape, f32)
    acc_b[...] = jnp.zeros(acc_b.shape, f32)
    n_wide = qi // per_wide
    scores_to(s0a, s0b, 0)

    def pair(jj, _):
        t0 = 2 * jj * wide
        scores_to(s1a, s1b, t0 + wide)
        update_both(s0a, s0b, t0)
        scores_to(s0a, s0b, t0 + 2 * wide)
        update_both(s1a, s1b, t0 + wide)
        return 0

    lax.fori_loop(0, n_wide // 2, pair, 0)

    @pl.when(n_wide % 2 == 1)
    def _():
        update_both(s0a, s0b, (n_wide - 1) * wide)
        scores_to(s0a, s0b, n_wide * wide)

    for r in range(per_wide):
        @pl.when(qi % per_wide == r)
        def _(r=r):
            width = (r + 1) * tq
            mask = mask_ref[:, wide - width:]
            update_both(s0a, s0b, n_wide * wide, width, mask)
            finish(acc_a[...], acc_b[...])


def _fox_kernel(q_ref, k_ref, v_ref, cum_ref, mask_ref, o_ref, ka_s, kb_s, v_s, *flash_scratch,
                seq, tq):
    hp = pl.program_id(1)
    qi = pl.program_id(2)
    lane = lax.broadcasted_iota(i32, (1, LANES), 1)

    def head_cums(cm):
        c_a = jnp.sum(jnp.where(lane == 2 * hp, cm, 0.0), axis=-1, keepdims=True)
        c_b = jnp.sum(jnp.where(lane == 2 * hp + 1, cm, 0.0), axis=-1, keepdims=True)
        return c_a, c_b

    @pl.when(qi == 0)
    def _():
        for c in range(seq // 512):
            rows = pl.ds(c * 512, 512)
            kk = k_ref[0, rows, :].astype(f32)
            c_a, c_b = head_cums(cum_ref[0, rows, :])
            ka_s[rows, :] = _k_aug(kk, lane, True, c_a)
            kb_s[rows, :] = _k_aug(kk, lane, False, c_b)
            v_s[rows, :] = _v_aug(v_ref[0, rows, :])

    q = q_ref[0].astype(f32)
    c_a, c_b = head_cums(cum_ref[0, pl.ds(pl.multiple_of(qi * tq, tq), tq), :])
    q_a = _q_aug(q, lane, True, c_a)
    q_b = _q_aug(q, lane, False, c_b)

    def finish(acc_a, acc_b):
        o_a = acc_a[:, :LANES] / acc_a[:, LANES:LANES + 1]
        o_b = acc_b[:, :LANES] / acc_b[:, LANES:LANES + 1]
        o_ref[0] = jnp.where(lane < 64, o_a, o_b).astype(bf16)

    _flash_pair(q_a, q_b, ka_s, kb_s, v_s, mask_ref, qi, tq, finish, flash_scratch)


def _tail_mask(diag):
    pad = [(0, 0)] * (diag.ndim - 1) + [(ATT_WIDE - diag.shape[-1], 0)]
    return jnp.pad(diag, pad)


def _fox_attention(proj3, cum3):
    bsz, seq, _ = proj3.shape
    tq, tk = ATT_TQ, ATT_WIDE
    n_pairs = FOX_HEADS // 2
    r = jnp.arange(tq)
    mask = _tail_mask(jnp.where(r[:, None] >= r[None, :], 0.0, NEG).astype(f32))
    return pl.pallas_call(
        functools.partial(_fox_kernel, seq=seq, tq=tq),
        grid=(bsz, n_pairs, seq // tq),
        in_specs=[pl.BlockSpec((1, tq, LANES), lambda b, h, q: (b, q, h)),
                  pl.BlockSpec((1, seq, LANES), lambda b, h, q: (b, 0, 4 + h)),
                  pl.BlockSpec((1, seq, LANES), lambda b, h, q: (b, 0, 8 + h)),
                  pl.BlockSpec((1, seq, LANES), lambda b, h, q: (b, 0, 0)),
                  pl.BlockSpec((tq, tk), lambda b, h, q: (0, 0))],
        out_specs=pl.BlockSpec((1, tq, LANES), lambda b, h, q: (b, q, h)),
        out_shape=jax.ShapeDtypeStruct((bsz, seq, FOX_WIDTH), bf16),
        scratch_shapes=[pltpu.VMEM((seq, LANES), bf16), pltpu.VMEM((seq, LANES), bf16),
                        pltpu.VMEM((seq, 2 * LANES), bf16)] + _flash_scratch(tq),
        compiler_params=_cparams("parallel", "parallel", "arbitrary"),
        name="fox_attn",
    )(proj3, proj3, proj3, cum3, mask)


def _diff_kernel(slope_ref, q_ref, k_ref, v_ref, mask_ref, lamv_ref, g_ref, o_ref,
                 ka_s, kb_s, v_s, *flash_scratch, seq, tq, lam_init):
    h = pl.program_id(1)
    qi = pl.program_id(2)
    lane = lax.broadcasted_iota(i32, (1, LANES), 1)
    slope = slope_ref[h]

    def pos_bias(start, n):
        pos = (start + lax.broadcasted_iota(i32, (n, 1), 0)).astype(f32)
        return -(slope * pos)

    @pl.when(qi == 0)
    def _():
        for c in range(seq // 512):
            rows = pl.ds(c * 512, 512)
            kk = k_ref[0, rows, :].astype(f32)
            cb = pos_bias(c * 512, 512)
            ka_s[rows, :] = _k_aug(kk, lane, True, cb)
            kb_s[rows, :] = _k_aug(kk, lane, False, cb)
            v_s[rows, :] = _v_aug(v_ref[0, rows, :])

    q = q_ref[0].astype(f32)
    cq = pos_bias(qi * tq, tq)
    q_a = _q_aug(q, lane, True, cq)
    q_b = _q_aug(q, lane, False, cq)

    def finish(acc_a, acc_b):
        lv = lamv_ref[...]
        s1 = jnp.sum(lv[0:1, :] * lv[1:2, :], axis=-1, keepdims=True)
        s2 = jnp.sum(lv[2:3, :] * lv[3:4, :], axis=-1, keepdims=True)
        lam = jnp.exp(s1) - jnp.exp(s2) + lam_init
        o_a = acc_a[:, :LANES] / acc_a[:, LANES:LANES + 1]
        o_b = acc_b[:, :LANES] / acc_b[:, LANES:LANES + 1]
        d = o_a - lam * o_b
        y = d * lax.rsqrt(jnp.mean(d * d, axis=-1, keepdims=True) + SUBLN_EPS) * g_ref[...]
        o_ref[0] = (y * (1.0 - lam_init)).astype(bf16)

    _flash_pair(q_a, q_b, ka_s, kb_s, v_s, mask_ref.at[0], qi, tq, finish, flash_scratch)


def _diff_attention(proj3, lam_vecs, subln_g, lam_init):
    bsz, seq, _ = proj3.shape
    tq, tk = ATT_TQ, ATT_WIDE
    slopes = jnp.asarray([2.0 ** (-8.0 * (i + 1) / DIFF_HEADS) for i in range(DIFF_HEADS)], f32) * LOG2E
    r = jnp.arange(tq)
    tq_i, tk_i = r[:, None], r[None, :]
    chunk_ok = (tq_i // CHUNK) >= (tk_i // CHUNK)
    ahead = jnp.maximum(tk_i - tq_i, 0).astype(f32)
    mask = _tail_mask(jnp.where(chunk_ok[None], -2.0 * slopes[:, None, None] * ahead[None], NEG).astype(f32))
    grid_spec = pltpu.PrefetchScalarGridSpec(
        num_scalar_prefetch=1,
        grid=(bsz, DIFF_HEADS, seq // tq),
        in_specs=[pl.BlockSpec((1, tq, LANES), lambda b, h, q, s: (b, q, 12 + h)),
                  pl.BlockSpec((1, seq, LANES), lambda b, h, q, s: (b, 0, 16 + h)),
                  pl.BlockSpec((1, seq, LANES), lambda b, h, q, s: (b, 0, 20 + h)),
                  pl.BlockSpec((1, tq, tk), lambda b, h, q, s: (h, 0, 0)),
                  pl.BlockSpec((8, LANES), lambda b, h, q, s: (0, 0)),
                  pl.BlockSpec((1, LANES), lambda b, h, q, s: (0, 0))],
        out_specs=pl.BlockSpec((1, tq, LANES), lambda b, h, q, s: (b, q, h)),
        scratch_shapes=[pltpu.VMEM((seq, LANES), bf16), pltpu.VMEM((seq, LANES), bf16),
                        pltpu.VMEM((seq, 2 * LANES), bf16)] + _flash_scratch(tq),
    )
    return pl.pallas_call(
        functools.partial(_diff_kernel, seq=seq, tq=tq, lam_init=lam_init),
        grid_spec=grid_spec,
        out_shape=jax.ShapeDtypeStruct((bsz, seq, DIFF_WIDTH), bf16),
        compiler_params=_cparams("parallel", "parallel", "arbitrary"),
        name="diff_attn",
    )(slopes, proj3, proj3, proj3, mask, lam_vecs, subln_g)


ROUTER_ROWS = 8 + N_EXPERTS


def _outproj_kernel(fox_ref, diff_ref, x_ref, mod_ref, g_ref, wo_ref, wr_hi_ref, wr_lo_ref,
                    br_ref, triu_ref, x1_ref, hp_ref, route_ref, wcol_ref, cnt_ref):
    tm = x_ref.shape[0]
    y = jnp.dot(fox_ref[...], wo_ref[0:FOX_WIDTH, :], preferred_element_type=f32)
    y += jnp.dot(diff_ref[...], wo_ref[FOX_WIDTH:, :], preferred_element_type=f32)
    x1 = x_ref[...] + mod_ref[0, 2:3, :] * y
    x1_ref[...] = x1
    ms = jnp.mean(x1 * x1, axis=-1, keepdims=True)
    h2 = (x1 * lax.rsqrt(ms + NORM_EPS) * g_ref[...]) * (1.0 + mod_ref[0, 4:5, :]) + mod_ref[0, 3:4, :]
    hp_ref[...] = _pack_rows(h2)

    nt = (((1,), (1,)), ((), ()))
    h_hi = h2.astype(bf16)
    h_lo = (h2 - h_hi.astype(f32)).astype(bf16)
    logits = lax.dot_general(wr_hi_ref[...], h_hi, nt, preferred_element_type=f32)
    logits += lax.dot_general(wr_lo_ref[...], h_hi, nt, preferred_element_type=f32)
    logits += lax.dot_general(wr_hi_ref[...], h_lo, nt, preferred_element_type=f32)
    logits = logits + br_ref[...]

    row8 = lax.broadcasted_iota(i32, (8, tm), 0)
    gl = jnp.where(row8 < N_GROUPS, logits[0:8, :], NEG)
    gmax = jnp.max(gl, axis=0, keepdims=True)
    grp = jnp.min(jnp.where(gl == gmax, row8, 8), axis=0, keepdims=True)
    p_g = 1.0 / jnp.sum(jnp.exp(gl - gmax), axis=0, keepdims=True)
    sel = logits[8:16, :]
    for g in range(1, N_GROUPS):
        sel = jnp.where(grp == g, logits[8 + 8 * g:16 + 8 * g, :], sel)
    v1 = jnp.max(sel, axis=0, keepdims=True)
    i1 = jnp.min(jnp.where(sel == v1, row8, 8), axis=0, keepdims=True)
    sel2 = jnp.where(row8 == i1, -jnp.inf, sel)
    v2 = jnp.max(sel2, axis=0, keepdims=True)
    i2 = jnp.min(jnp.where(sel2 == v2, row8, 8), axis=0, keepdims=True)
    e21 = jnp.exp(v2 - v1)
    w1 = p_g / (1.0 + e21)
    w2 = w1 * e21
    e1 = grp * EXPERTS_PER_GROUP + i1
    e2 = grp * EXPERTS_PER_GROUP + i2

    row32 = lax.broadcasted_iota(i32, (N_EXPERTS, tm), 0)
    oh1 = row32 == e1
    oh2 = row32 == e2
    both = jnp.where(oh1 | oh2, 1.0, 0.0)
    prefix = jnp.dot(both.astype(bf16), triu_ref[...], preferred_element_type=f32)
    r1 = jnp.sum(jnp.where(oh1, prefix, 0.0), axis=0, keepdims=True).astype(i32)
    r2 = jnp.sum(jnp.where(oh2, prefix, 0.0), axis=0, keepdims=True).astype(i32)
    route = jnp.where(row8 == 0, e1, jnp.where(row8 == 1, e2,
                      jnp.where(row8 == 2, r1, jnp.where(row8 == 3, r2, 0))))
    route_ref[...] = route
    cnt = jnp.sum(both, axis=1, keepdims=True)
    cnt_ref[0] = jnp.broadcast_to(cnt, (N_EXPERTS, LANES)).astype(i32)
    row128 = lax.broadcasted_iota(i32, (LANES, tm), 0)
    w_rows = jnp.where(row128 == 0, w1, jnp.where(row128 == 1, w2, 0.0))
    wcol_ref[...] = w_rows.T


def _outproj(fox2, diff2, x2, mod, g2, w_o, wr_hi, wr_lo, b_r, seq):
    t, d = x2.shape
    tm = ROW_TILE
    tiles_per_batch = seq // tm
    n_tiles = t // tm
    triu = (jnp.arange(tm)[:, None] < jnp.arange(tm)[None, :]).astype(bf16)
    row = lambda i: (i, 0)
    const = lambda i: (0, 0)
    return pl.pallas_call(
        _outproj_kernel,
        grid=(n_tiles,),
        in_specs=[pl.BlockSpec((tm, FOX_WIDTH), row),
                  pl.BlockSpec((tm, DIFF_WIDTH), row),
                  pl.BlockSpec((tm, d), row),
                  pl.BlockSpec((1, 6, d), lambda i: (i // tiles_per_batch, 0, 0)),
                  pl.BlockSpec((1, d), const),
                  pl.BlockSpec((d, d), const),
                  pl.BlockSpec((ROUTER_ROWS, d), const),
                  pl.BlockSpec((ROUTER_ROWS, d), const),
                  pl.BlockSpec((ROUTER_ROWS, 1), const),
                  pl.BlockSpec((tm, tm), const)],
        out_specs=[pl.BlockSpec((tm, d), row),
                   pl.BlockSpec((tm, HALF), row),
                   pl.BlockSpec((8, tm), lambda i: (0, i)),
                   pl.BlockSpec((tm, LANES), row),
                   pl.BlockSpec((1, N_EXPERTS, LANES), lambda i: (i, 0, 0))],
        out_shape=[jax.ShapeDtypeStruct((t, d), f32),
                   jax.ShapeDtypeStruct((t, HALF), i32),
                   jax.ShapeDtypeStruct((8, t), i32),
                   jax.ShapeDtypeStruct((t, LANES), f32),
                   jax.ShapeDtypeStruct((n_tiles, N_EXPERTS, LANES), i32)],
        compiler_params=_cparams("parallel"),
        name="outproj_router",
    )(fox2, diff2, x2, mod, g2, w_o, wr_hi, wr_lo, b_r, triu)


def _dispatch_kernel(dest_ref, h_ref, xs_in_ref, xs_ref, sem, *, tm):
    del xs_in_ref
    i = pl.program_id(0)
    base = i * (2 * tm)

    def row_copy(r, k):
        d = dest_ref[base + 2 * r + k]
        return pltpu.make_async_copy(h_ref.at[pl.ds(r, 1), :], xs_ref.at[pl.ds(d, 1), :], sem)

    def issue(r, _):
        row_copy(r, 0).start()
        row_copy(r, 1).start()
        return 0

    lax.fori_loop(0, tm, issue, 0, unroll=8)

    def drain(r, _):
        row_copy(r, 0).wait()
        row_copy(r, 1).wait()
        return 0

    lax.fori_loop(0, tm, drain, 0, unroll=8)


def _dispatch(dest, h_packed, n_slots):
    t = h_packed.shape[0]
    tm = GATHER_TILE
    xs0 = jnp.zeros((n_slots, HALF), i32)
    grid_spec = pltpu.PrefetchScalarGridSpec(
        num_scalar_prefetch=1,
        grid=(t // tm,),
        in_specs=[pl.BlockSpec((tm, HALF), lambda i, d: (i, 0)),
                  pl.BlockSpec(memory_space=pl.ANY)],
        out_specs=pl.BlockSpec(memory_space=pl.ANY),
        scratch_shapes=[pltpu.SemaphoreType.DMA(())],
    )
    return pl.pallas_call(
        functools.partial(_dispatch_kernel, tm=tm),
        grid_spec=grid_spec,
        out_shape=jax.ShapeDtypeStruct((n_slots, HALF), i32),
        input_output_aliases={2: 0},
        compiler_params=_cparams("arbitrary"),
        name="moe_dispatch",
    )(dest, h_packed, xs0)


def _experts_kernel(be_ref, nu_ref, xs_ref, wg_ref, wu_ref, wd_ref, y_ref, wg_s, wu_s, wd_s):
    i = pl.program_id(0)
    prev = be_ref[jnp.maximum(i - 1, 0)]
    fresh = jnp.logical_or(i == 0, be_ref[i] != prev)

    @pl.when(jnp.logical_and(fresh, i < nu_ref[0]))
    def _():
        wg_s[...] = wg_ref[0].astype(bf16)
        wu_s[...] = wu_ref[0].astype(bf16)
        wd_s[...] = wd_ref[0].astype(bf16)

    @pl.when(i < nu_ref[0])
    def _():
        lo, hi = _unpack_rows(xs_ref[...])
        xb = jnp.concatenate([lo.astype(bf16), hi.astype(bf16)], axis=-1)
        g = jnp.dot(xb, wg_s[...], preferred_element_type=f32)
        u = jnp.dot(xb, wu_s[...], preferred_element_type=f32)
        hid = (g * (1.0 / (1.0 + jnp.exp(-g)))) * u
        y = jnp.dot(hid.astype(bf16), wd_s[...], preferred_element_type=f32)
        y_ref[...] = _pack_rows(y)

    @pl.when(i >= nu_ref[0])
    def _():
        y_ref[...] = jnp.zeros_like(y_ref)


def _experts(block_e, n_used, xs, w_gate, w_up, w_down):
    n_slots = xs.shape[0]
    bk = MOE_BLOCK
    grid_spec = pltpu.PrefetchScalarGridSpec(
        num_scalar_prefetch=2,
        grid=(n_slots // bk,),
        in_specs=[pl.BlockSpec((bk, HALF), lambda i, be, nu: (i, 0)),
                  pl.BlockSpec((1, D_MODEL, D_EXPERT), lambda i, be, nu: (be[i], 0, 0)),
                  pl.BlockSpec((1, D_MODEL, D_EXPERT), lambda i, be, nu: (be[i], 0, 0)),
                  pl.BlockSpec((1, D_EXPERT, D_MODEL), lambda i, be, nu: (be[i], 0, 0))],
        out_specs=pl.BlockSpec((bk, HALF), lambda i, be, nu: (i, 0)),
        scratch_shapes=[pltpu.VMEM((D_MODEL, D_EXPERT), bf16),
                        pltpu.VMEM((D_MODEL, D_EXPERT), bf16),
                        pltpu.VMEM((D_EXPERT, D_MODEL), bf16)],
    )
    return pl.pallas_call(
        _experts_kernel,
        grid_spec=grid_spec,
        out_shape=jax.ShapeDtypeStruct((n_slots, HALF), i32),
        compiler_params=_cparams("arbitrary"),
        name="moe_experts",
    )(block_e, n_used, xs, w_gate, w_up, w_down)


def _combine_kernel(dest_ref, y_ref, x1_ref, wcol_ref, mod_ref, g_ref, o_ref, buf, sems, *, tm):
    i = pl.program_id(0)
    n = pl.num_programs(0)

    def row_copy(tile, slot, r, k):
        d = dest_ref[tile * (2 * tm) + 2 * r + k]
        return pltpu.make_async_copy(y_ref.at[pl.ds(d, 1), :],
                                     buf.at[slot, k, pl.ds(r, 1), :], sems.at[slot])

    def issue_tile(tile, slot):
        def body(r, _):
            row_copy(tile, slot, r, 0).start()
            row_copy(tile, slot, r, 1).start()
            return 0
        lax.fori_loop(0, tm, body, 0, unroll=8)

    @pl.when(i == 0)
    def _():
        issue_tile(0, 0)

    @pl.when(i + 1 < n)
    def _():
        issue_tile(i + 1, (i + 1) % 2)

    slot = i % 2

    def drain(r, _):
        row_copy(i, slot, r, 0).wait()
        row_copy(i, slot, r, 1).wait()
        return 0

    lax.fori_loop(0, tm, drain, 0, unroll=8)

    wc = wcol_ref[...]
    w0 = wc[:, 0:1]
    w1 = wc[:, 1:2]
    lo0, hi0 = _unpack_rows(buf[slot, 0])
    lo1, hi1 = _unpack_rows(buf[slot, 1])
    moe = jnp.concatenate([w0 * lo0 + w1 * lo1, w0 * hi0 + w1 * hi1], axis=-1)
    x = x1_ref[...] + mod_ref[0, 5:6, :] * moe
    ms = jnp.mean(x * x, axis=-1, keepdims=True)
    o_ref[...] = x * lax.rsqrt(ms + NORM_EPS) * g_ref[...]


def _combine(dest, y_packed, x1, wcol, mod, g_f, seq):
    t, d = x1.shape
    tm = GATHER_TILE
    tiles_per_batch = seq // tm
    grid_spec = pltpu.PrefetchScalarGridSpec(
        num_scalar_prefetch=1,
        grid=(t // tm,),
        in_specs=[pl.BlockSpec(memory_space=pl.ANY),
                  pl.BlockSpec((tm, d), lambda i, ds: (i, 0)),
                  pl.BlockSpec((tm, LANES), lambda i, ds: (i, 0)),
                  pl.BlockSpec((1, 6, d), lambda i, ds: (i // tiles_per_batch, 0, 0)),
                  pl.BlockSpec((1, d), lambda i, ds: (0, 0))],
        out_specs=pl.BlockSpec((tm, d), lambda i, ds: (i, 0)),
        scratch_shapes=[pltpu.VMEM((2, 2, tm, HALF), i32),
                        pltpu.SemaphoreType.DMA((2,))],
    )
    return pl.pallas_call(
        functools.partial(_combine_kernel, tm=tm),
        grid_spec=grid_spec,
        out_shape=jax.ShapeDtypeStruct((t, d), f32),
        compiler_params=_cparams("arbitrary"),
        name="moe_combine",
    )(dest, y_packed, x1, wcol, mod, g_f)


def _route_tables(route, cnt, tm):
    n_tiles = cnt.shape[0]
    t = route.shape[1]
    bk = MOE_BLOCK
    n_blocks = (2 * t) // bk + N_EXPERTS
    ti = jnp.arange(n_tiles)
    ei = jnp.arange(N_EXPERTS)
    tile_base = jnp.sum(jnp.where((ti[:, None] > ti[None, :])[:, :, None], cnt[None], 0), axis=1)
    total = jnp.sum(cnt, axis=0)
    padded = (total + bk - 1) // bk * bk
    pends = jnp.sum(jnp.where(ei[None, :] <= ei[:, None], padded[None, :], 0), axis=1)
    base = (pends - padded)[None, :] + tile_base
    base_tok = jnp.broadcast_to(base[:, None, :], (n_tiles, tm, N_EXPERTS)).reshape(t, N_EXPERTS)

    def slot_base(e):
        return jnp.sum(jnp.where(ei[None, :] == e[:, None], base_tok, 0), axis=1)

    d0 = slot_base(route[0]) + route[2]
    d1 = slot_base(route[1]) + route[3]
    dest = jnp.stack([d0, d1], axis=1).reshape(-1).astype(i32)
    block_start = jnp.arange(n_blocks, dtype=i32) * bk
    block_e = jnp.clip(jnp.sum(pends[None, :] <= block_start[:, None], axis=1),
                       0, N_EXPERTS - 1).astype(i32)
    n_used = (pends[-1] // bk).astype(i32).reshape(1)
    return dest, block_e, n_used, n_blocks * bk


def _layer(x2, c, seq, l, ada_w, ada_b, norm1_g, w_in, b_f, lam_q1, lam_k1, lam_q2, lam_k2,
           subln_g, w_o, norm2_g, w_rg, b_rg, w_re, b_re, w_gate, w_up, w_down):
    t, d = x2.shape
    bsz = t // seq
    lam_init = 0.8 - 0.6 * math.exp(-0.3 * l)
    mod = _ada(c, ada_w, ada_b).reshape(bsz, 6, d)

    z0 = 3 * FOX_WIDTH
    w_main = jnp.concatenate([w_in[:, :z0], w_in[:, z0 + FOX_HEADS:]], axis=1).astype(bf16)
    w_fz = jnp.pad(w_in[:, z0:z0 + FOX_HEADS], ((0, 0), (0, LANES - FOX_HEADS))).astype(bf16)
    b_fz = jnp.pad(b_f, (0, LANES - FOX_HEADS)).reshape(1, LANES)
    proj, cum = _inproj(x2, mod, norm1_g.reshape(1, d), w_main, w_fz, b_fz, seq)
    proj3 = proj.reshape(bsz, seq, -1)

    fox = _fox_attention(proj3, cum.reshape(bsz, seq, LANES))
    lam_vecs = jnp.pad(jnp.stack([lam_q1, lam_k1, lam_q2, lam_k2]).astype(f32),
                       ((0, 4), (0, LANES - HEAD_DIM)))
    diff = _diff_attention(proj3, lam_vecs, subln_g.reshape(1, LANES), lam_init)

    w_r = jnp.concatenate([w_rg.T, jnp.zeros((8 - N_GROUPS, d), f32), w_re.T], axis=0)
    wr_hi = w_r.astype(bf16)
    wr_lo = (w_r - wr_hi.astype(f32)).astype(bf16)
    b_r = jnp.concatenate([b_rg, jnp.zeros((8 - N_GROUPS,), f32), b_re]).reshape(ROUTER_ROWS, 1)
    x1, h_packed, route, wcol, cnt = _outproj(
        fox.reshape(t, FOX_WIDTH), diff.reshape(t, DIFF_WIDTH), x2, mod, norm2_g.reshape(1, d),
        w_o.astype(bf16), wr_hi, wr_lo, b_r, seq)

    dest, block_e, n_used, n_slots = _route_tables(route, cnt[:, :, 0], ROW_TILE)
    xs = _dispatch(dest, h_packed, n_slots)
    y_packed = _experts(block_e, n_used, xs, w_gate, w_up, w_down)
    return dest, y_packed, x1, wcol, mod


def kernel(x, c, ada_w, ada_b, norm1_g, w_in, b_f, lam_q1, lam_k1, lam_q2, lam_k2, subln_g, w_o,
           norm2_g, w_rg, b_rg, w_re, b_re, w_gate, w_up, w_down, norm_f_g):
    bsz, seq, d = x.shape
    depth = ada_w.shape[0]
    assert depth == 1 and d == D_MODEL and seq % ROW_TILE == 0 and seq % ATT_WIDE == 0
    x2 = x.reshape(bsz * seq, d)
    dest, y_packed, x1, wcol, mod = _layer(
        x2, c, seq, 0, ada_w[0], ada_b[0], norm1_g[0], w_in[0], b_f[0], lam_q1[0], lam_k1[0],
        lam_q2[0], lam_k2[0], subln_g[0], w_o[0], norm2_g[0], w_rg[0], b_rg[0], w_re[0], b_re[0],
        w_gate[0], w_up[0], w_down[0])
    out = _combine(dest, y_packed, x1, wcol, mod, norm_f_g.reshape(1, d), seq)
    return out.reshape(bsz, seq, d)
```

```python
import functools
import math

import jax
import jax.numpy as jnp
from jax import lax
from jax.experimental import pallas as pl
from jax.experimental.pallas import tpu as pltpu

f32 = jnp.float32
bf16 = jnp.bfloat16
i32 = jnp.int32

D_MODEL = 1024
HEAD_DIM = 64
FOX_HEADS = 8
FOX_WIDTH = FOX_HEADS * HEAD_DIM
DIFF_HEADS = 4
DIFF_QK_WIDTH = DIFF_HEADS * 2 * HEAD_DIM
DIFF_WIDTH = DIFF_HEADS * 2 * HEAD_DIM
CHUNK = 64
N_GROUPS = 4
EXPERTS_PER_GROUP = 8
N_EXPERTS = N_GROUPS * EXPERTS_PER_GROUP
D_EXPERT = 512
NORM_EPS = 1e-6
SUBLN_EPS = 1e-5

LANES = 128
LOG2E = 1.4426950408889634
Q_SCALE = HEAD_DIM ** -0.5 * LOG2E
NEG = -1e30
HALF = D_MODEL // 2
ROW_CHUNKS = HALF // LANES

ROW_TILE = 512
ATT_TQ = 512
ATT_WIDE = 1024
MOE_BLOCK = 256
GATHER_TILE = 256
VMEM_LIMIT = 48 * 1024 * 1024


def _cparams(*sem):
    return pltpu.CompilerParams(dimension_semantics=sem, vmem_limit_bytes=VMEM_LIMIT)


def _split3(c):
    hi = c.astype(bf16).astype(f32)
    r = c - hi
    mid = r.astype(bf16).astype(f32)
    lo = r - mid
    return hi, mid, lo


def _pack_rows(y):
    a = pltpu.bitcast(y[:, :HALF].astype(bf16).astype(f32), i32)
    b = pltpu.bitcast(y[:, HALF:].astype(bf16).astype(f32), i32)
    return lax.shift_right_logical(a, 16) | (b & jnp.int32(-65536))


def _store_slabs(ref, packed):
    n = packed.shape[0]
    for j in range(ROW_CHUNKS):
        ref[pl.ds(j, n, stride=ROW_CHUNKS), :] = packed[:, j * LANES:(j + 1) * LANES]


def _load_slabs(ref, n):
    return jnp.concatenate([ref[pl.ds(j, n, stride=ROW_CHUNKS), :] for j in range(ROW_CHUNKS)], axis=-1)


def _unpack_rows(w):
    lo = pltpu.bitcast(lax.shift_left(w, 16), f32)
    hi = pltpu.bitcast(w & jnp.int32(-65536), f32)
    return lo, hi


def _ada_kernel(c_ref, w_ref, b_ref, o_ref):
    c = c_ref[...]
    w = w_ref[...]
    c_hi = c.astype(bf16)
    c_lo = (c - c_hi.astype(f32)).astype(bf16)
    w_hi = w.astype(bf16)
    w_lo = (w - w_hi.astype(f32)).astype(bf16)
    acc = jnp.dot(c_hi, w_hi, preferred_element_type=f32)
    acc += jnp.dot(c_hi, w_lo, preferred_element_type=f32)
    acc += jnp.dot(c_lo, w_hi, preferred_element_type=f32)
    o_ref[...] = acc + b_ref[...]


def _ada(c, w, b):
    bsz, d = c.shape
    n = w.shape[1]
    tn = 1024
    return pl.pallas_call(
        _ada_kernel,
        grid=(n // tn,),
        in_specs=[pl.BlockSpec((bsz, d), lambda j: (0, 0)),
                  pl.BlockSpec((d, tn), lambda j: (0, j)),
                  pl.BlockSpec((1, tn), lambda j: (0, j))],
        out_specs=pl.BlockSpec((bsz, tn), lambda j: (0, j)),
        out_shape=jax.ShapeDtypeStruct((bsz, n), f32),
        compiler_params=_cparams("parallel"),
        name="ada_mod",
    )(c, w, b.reshape(1, n))


def _inproj_kernel(x_ref, mod_ref, g_ref, wm_ref, wz_ref, bf_ref, tril_ref,
                   proj_ref, cum_ref, carry_ref, *, tiles_per_batch):
    i = pl.program_id(0)
    x = x_ref[...]
    ms = jnp.mean(x * x, axis=-1, keepdims=True)
    y = x * lax.rsqrt(ms + NORM_EPS) * g_ref[...]
    h = (y * (1.0 + mod_ref[0, 1:2, :]) + mod_ref[0, 0:1, :]).astype(bf16)
    n_chunks = proj_ref.shape[1] // 512
    for j in range(n_chunks):
        acc = jnp.dot(h, wm_ref[:, j * 512:(j + 1) * 512], preferred_element_type=f32)
        if j in (0, 3):
            acc = acc * Q_SCALE
        proj_ref[:, j * 512:(j + 1) * 512] = acc.astype(bf16)
    fz = jnp.dot(h, wz_ref[...], preferred_element_type=f32) + bf_ref[...]
    ls = (jnp.minimum(fz, 0.0) - jnp.log(1.0 + jnp.exp(-jnp.abs(fz)))) * LOG2E
    hi, mid, lo = _split3(ls)
    tril = tril_ref[...]
    local = jnp.dot(tril, hi.astype(bf16), preferred_element_type=f32)
    local += jnp.dot(tril, mid.astype(bf16), preferred_element_type=f32)
    local += jnp.dot(tril, lo.astype(bf16), preferred_element_type=f32)

    @pl.when(i % tiles_per_batch == 0)
    def _():
        carry_ref[...] = jnp.zeros_like(carry_ref)

    cum = local + carry_ref[0:1, :]
    cum_ref[...] = cum
    tm = x.shape[0]
    carry_ref[0:1, :] = cum[tm - 1:tm, :]


def _inproj(x2, mod, g1, w_main, w_fz, b_fz, seq):
    t, d = x2.shape
    tm = ROW_TILE
    tiles_per_batch = seq // tm
    n_main = w_main.shape[1]
    tril = (jnp.arange(tm)[:, None] >= jnp.arange(tm)[None, :]).astype(bf16)
    return pl.pallas_call(
        functools.partial(_inproj_kernel, tiles_per_batch=tiles_per_batch),
        grid=(t // tm,),
        in_specs=[pl.BlockSpec((tm, d), lambda i: (i, 0)),
                  pl.BlockSpec((1, 6, d), lambda i: (i // tiles_per_batch, 0, 0)),
                  pl.BlockSpec((1, d), lambda i: (0, 0)),
                  pl.BlockSpec((d, n_main), lambda i: (0, 0)),
                  pl.BlockSpec((d, LANES), lambda i: (0, 0)),
                  pl.BlockSpec((1, LANES), lambda i: (0, 0)),
                  pl.BlockSpec((tm, tm), lambda i: (0, 0))],
        out_specs=[pl.BlockSpec((tm, n_main), lambda i: (i, 0)),
                   pl.BlockSpec((tm, LANES), lambda i: (i, 0))],
        out_shape=[jax.ShapeDtypeStruct((t, n_main), bf16),
                   jax.ShapeDtypeStruct((t, LANES), f32)],
        scratch_shapes=[pltpu.VMEM((8, LANES), f32)],
        compiler_params=_cparams("arbitrary"),
        name="inproj",
    )(x2, mod, g1, w_main, w_fz, b_fz, tril)


def _aug(data, lane, low_map, first, last):
    base = 64 if low_map else 0
    out = jnp.zeros_like(data)
    for n, val in enumerate(tuple(first) + tuple(last)):
        out = jnp.where(lane == base + n, val, out)
    keep = (lane < 64) if low_map else (lane >= 64)
    return jnp.where(keep, data, out)


_ONES3 = (1.0, 1.0, 1.0)


def _q_aug(q, lane, low_map, c):
    return _aug(q, lane, low_map, _split3(c), _ONES3).astype(bf16)


def _k_aug(k, lane, low_map, c):
    hi, mid, lo = _split3(c)
    return _aug(k, lane, low_map, _ONES3, (-hi, -mid, -lo)).astype(bf16)


def _v_aug(v):
    lane = lax.broadcasted_iota(i32, (1, LANES), 1)
    ones_lane = jnp.where(lane == 0, 1.0, 0.0).astype(v.dtype)
    return jnp.concatenate([v, jnp.broadcast_to(ones_lane, v.shape)], axis=-1)


def _flash_scratch(tq):
    return ([pltpu.VMEM((tq, ATT_WIDE), f32)] * 4
            + [pltpu.VMEM((tq, 2 * LANES), f32)] * 2 + [pltpu.VMEM((tq, 1), f32)] * 2)


def _flash_pair(q_a, q_b, ka_s, kb_s, v_s, mask_ref, qi, tq, finish, scratch):
    s0a, s0b, s1a, s1b, acc_a, acc_b, m_a, m_b = scratch
    nt = (((1,), (1,)), ((), ()))
    wide = ATT_WIDE
    per_wide = wide // tq

    def scores_to(buf_a, buf_b, start):
        rows = pl.ds(pl.multiple_of(start, wide), wide)
        buf_a[...] = lax.dot_general(q_a, ka_s[rows, :], nt, preferred_element_type=f32)
        buf_b[...] = lax.dot_general(q_b, kb_s[rows, :], nt, preferred_element_type=f32)

    def update(buf, m_ref, acc_ref, start, width, mask):
        s = buf[:, :width]
        if mask is not None:
            s = s + mask
        v = v_s[pl.ds(pl.multiple_of(start, wide), width), :]
        m = m_ref[...]
        m_new = jnp.maximum(m, jnp.max(s, axis=-1, keepdims=True))
        p = jnp.exp2(s - m_new)
        acc_ref[...] = (jnp.exp2(m - m_new) * acc_ref[...]
                        + jnp.dot(p.astype(bf16), v, preferred_element_type=f32))
        m_ref[...] = m_new

    def update_both(buf_a, buf_b, start, width=wide, mask=None):
        update(buf_a, m_a, acc_a, start, width, mask)
        update(buf_b, m_b, acc_b, start, width, mask)

    m_a[...] = jnp.full(m_a.shape, NEG, f32)
    m_b[...] = jnp.full(m_b.shape, NEG, f32)
    acc_a[...] = jnp.zeros(acc_a.shape, f32)
    acc_b[...] = jnp.zeros(acc_b.shape, f32)
    n_wide = qi // per_wide
    scores_to(s0a, s0b, 0)

    def pair(jj, _):
        t0 = 2 * jj * wide
        scores_to(s1a, s1b, t0 + wide)
        update_both(s0a, s0b, t0)
        scores_to(s0a, s0b, t0 + 2 * wide)
        update_both(s1a, s1b, t0 + wide)
        return 0

    lax.fori_loop(0, n_wide // 2, pair, 0)

    @pl.when(n_wide % 2 == 1)
    def _():
        update_both(s0a, s0b, (n_wide - 1) * wide)
        scores_to(s0a, s0b, n_wide * wide)

    for r in range(per_wide):
        @pl.when(qi % per_wide == r)
        def _(r=r):
            width = (r + 1) * tq
            mask = mask_ref[:, wide - width:]
            update_both(s0a, s0b, n_wide * wide, width, mask)
            finish(acc_a[...], acc_b[...])


def _fox_kernel(q_ref, k_ref, v_ref, cum_ref, mask_ref, o_ref, ka_s, kb_s, v_s, *flash_scratch,
                seq, tq):
    hp = pl.program_id(1)
    qi = pl.program_id(2)
    lane = lax.broadcasted_iota(i32, (1, LANES), 1)

    def head_cums(cm):
        c_a = jnp.sum(jnp.where(lane == 2 * hp, cm, 0.0), axis=-1, keepdims=True)
        c_b = jnp.sum(jnp.where(lane == 2 * hp + 1, cm, 0.0), axis=-1, keepdims=True)
        return c_a, c_b

    @pl.when(qi == 0)
    def _():
        for c in range(seq // 512):
            rows = pl.ds(c * 512, 512)
            kk = k_ref[0, rows, :].astype(f32)
            c_a, c_b = head_cums(cum_ref[0, rows, :])
            ka_s[rows, :] = _k_aug(kk, lane, True, c_a)
            kb_s[rows, :] = _k_aug(kk, lane, False, c_b)
            v_s[rows, :] = _v_aug(v_ref[0, rows, :])

    q = q_ref[0].astype(f32)
    c_a, c_b = head_cums(cum_ref[0, pl.ds(pl.multiple_of(qi * tq, tq), tq), :])
    q_a = _q_aug(q, lane, True, c_a)
    q_b = _q_aug(q, lane, False, c_b)

    def finish(acc_a, acc_b):
        o_a = acc_a[:, :LANES] / acc_a[:, LANES:LANES + 1]
        o_b = acc_b[:, :LANES] / acc_b[:, LANES:LANES + 1]
        o_ref[0] = jnp.where(lane < 64, o_a, o_b).astype(bf16)

    _flash_pair(q_a, q_b, ka_s, kb_s, v_s, mask_ref, qi, tq, finish, flash_scratch)


def _tail_mask(diag):
    pad = [(0, 0)] * (diag.ndim - 1) + [(ATT_WIDE - diag.shape[-1], 0)]
    return jnp.pad(diag, pad)


def _fox_attention(proj3, cum3):
    bsz, seq, _ = proj3.shape
    tq, tk = ATT_TQ, ATT_WIDE
    n_pairs = FOX_HEADS // 2
    r = jnp.arange(tq)
    mask = _tail_mask(jnp.where(r[:, None] >= r[None, :], 0.0, NEG).astype(f32))
    return pl.pallas_call(
        functools.partial(_fox_kernel, seq=seq, tq=tq),
        grid=(bsz, n_pairs, seq // tq),
        in_specs=[pl.BlockSpec((1, tq, LANES), lambda b, h, q: (b, q, h)),
                  pl.BlockSpec((1, seq, LANES), lambda b, h, q: (b, 0, 4 + h)),
                  pl.BlockSpec((1, seq, LANES), lambda b, h, q: (b, 0, 8 + h)),
                  pl.BlockSpec((1, seq, LANES), lambda b, h, q: (b, 0, 0)),
                  pl.BlockSpec((tq, tk), lambda b, h, q: (0, 0))],
        out_specs=pl.BlockSpec((1, tq, LANES), lambda b, h, q: (b, q, h)),
        out_shape=jax.ShapeDtypeStruct((bsz, seq, FOX_WIDTH), bf16),
        scratch_shapes=[pltpu.VMEM((seq, LANES), bf16), pltpu.VMEM((seq, LANES), bf16),
                        pltpu.VMEM((seq, 2 * LANES), bf16)] + _flash_scratch(tq),
        compiler_params=_cparams("parallel", "parallel", "arbitrary"),
        name="fox_attn",
    )(proj3, proj3, proj3, cum3, mask)


def _diff_kernel(slope_ref, q_ref, k_ref, v_ref, mask_ref, lamv_ref, g_ref, o_ref,
                 ka_s, kb_s, v_s, *flash_scratch, seq, tq, lam_init):
    h = pl.program_id(1)
    qi = pl.program_id(2)
    lane = lax.broadcasted_iota(i32, (1, LANES), 1)
    slope = slope_ref[h]

    def pos_bias(start, n):
        pos = (start + lax.broadcasted_iota(i32, (n, 1), 0)).astype(f32)
        return -(slope * pos)

    @pl.when(qi == 0)
    def _():
        for c in range(seq // 512):
            rows = pl.ds(c * 512, 512)
            kk = k_ref[0, rows, :].astype(f32)
            cb = pos_bias(c * 512, 512)
            ka_s[rows, :] = _k_aug(kk, lane, True, cb)
            kb_s[rows, :] = _k_aug(kk, lane, False, cb)
            v_s[rows, :] = _v_aug(v_ref[0, rows, :])

    q = q_ref[0].astype(f32)
    cq = pos_bias(qi * tq, tq)
    q_a = _q_aug(q, lane, True, cq)
    q_b = _q_aug(q, lane, False, cq)

    def finish(acc_a, acc_b):
        lv = lamv_ref[...]
        s1 = jnp.sum(lv[0:1, :] * lv[1:2, :], axis=-1, keepdims=True)
        s2 = jnp.sum(lv[2:3, :] * lv[3:4, :], axis=-1, keepdims=True)
        lam = jnp.exp(s1) - jnp.exp(s2) + lam_init
        o_a = acc_a[:, :LANES] / acc_a[:, LANES:LANES + 1]
        o_b = acc_b[:, :LANES] / acc_b[:, LANES:LANES + 1]
        d = o_a - lam * o_b
        y = d * lax.rsqrt(jnp.mean(d * d, axis=-1, keepdims=True) + SUBLN_EPS) * g_ref[...]
        o_ref[0] = (y * (1.0 - lam_init)).astype(bf16)

    _flash_pair(q_a, q_b, ka_s, kb_s, v_s, mask_ref.at[0], qi, tq, finish, flash_scratch)


def _diff_attention(proj3, lam_vecs, subln_g, lam_init):
    bsz, seq, _ = proj3.shape
    tq, tk = ATT_TQ, ATT_WIDE
    slopes = jnp.asarray([2.0 ** (-8.0 * (i + 1) / DIFF_HEADS) for i in range(DIFF_HEADS)], f32) * LOG2E
    r = jnp.arange(tq)
    tq_i, tk_i = r[:, None], r[None, :]
    chunk_ok = (tq_i // CHUNK) >= (tk_i // CHUNK)
    ahead = jnp.maximum(tk_i - tq_i, 0).astype(f32)
    mask = _tail_mask(jnp.where(chunk_ok[None], -2.0 * slopes[:, None, None] * ahead[None], NEG).astype(f32))
    grid_spec = pltpu.PrefetchScalarGridSpec(
        num_scalar_prefetch=1,
        grid=(bsz, DIFF_HEADS, seq // tq),
        in_specs=[pl.BlockSpec((1, tq, LANES), lambda b, h, q, s: (b, q, 12 + h)),
                  pl.BlockSpec((1, seq, LANES), lambda b, h, q, s: (b, 0, 16 + h)),
                  pl.BlockSpec((1, seq, LANES), lambda b, h, q, s: (b, 0, 20 + h)),
                  pl.BlockSpec((1, tq, tk), lambda b, h, q, s: (h, 0, 0)),
                  pl.BlockSpec((8, LANES), lambda b, h, q, s: (0, 0)),
                  pl.BlockSpec((1, LANES), lambda b, h, q, s: (0, 0))],
        out_specs=pl.BlockSpec((1, tq, LANES), lambda b, h, q, s: (b, q, h)),
        scratch_shapes=[pltpu.VMEM((seq, LANES), bf16), pltpu.VMEM((seq, LANES), bf16),
                        pltpu.VMEM((seq, 2 * LANES), bf16)] + _flash_scratch(tq),
    )
    return pl.pallas_call(
        functools.partial(_diff_kernel, seq=seq, tq=tq, lam_init=lam_init),
        grid_spec=grid_spec,
        out_shape=jax.ShapeDtypeStruct((bsz, seq, DIFF_WIDTH), bf16),
        compiler_params=_cparams("parallel", "parallel", "arbitrary"),
        name="diff_attn",
    )(slopes, proj3, proj3, proj3, mask, lam_vecs, subln_g)


ROUTER_ROWS = 8 + N_EXPERTS


def _outproj_kernel(fox_ref, diff_ref, x_ref, mod_ref, g_ref, wo_ref, wr_hi_ref, wr_lo_ref,
                    br_ref, triu_ref, x1_ref, hp_ref, route_ref, wcol_ref, cnt_ref):
    tm = x_ref.shape[0]
    y = jnp.dot(fox_ref[...], wo_ref[0:FOX_WIDTH, :], preferred_element_type=f32)
    y += jnp.dot(diff_ref[...], wo_ref[FOX_WIDTH:, :], preferred_element_type=f32)
    x1 = x_ref[...] + mod_ref[0, 2:3, :] * y
    x1_ref[...] = x1
    ms = jnp.mean(x1 * x1, axis=-1, keepdims=True)
    h2 = (x1 * lax.rsqrt(ms + NORM_EPS) * g_ref[...]) * (1.0 + mod_ref[0, 4:5, :]) + mod_ref[0, 3:4, :]
    _store_slabs(hp_ref, _pack_rows(h2))

    nt = (((1,), (1,)), ((), ()))
    h_hi = h2.astype(bf16)
    h_lo = (h2 - h_hi.astype(f32)).astype(bf16)
    logits = lax.dot_general(wr_hi_ref[...], h_hi, nt, preferred_element_type=f32)
    logits += lax.dot_general(wr_lo_ref[...], h_hi, nt, preferred_element_type=f32)
    logits += lax.dot_general(wr_hi_ref[...], h_lo, nt, preferred_element_type=f32)
    logits = logits + br_ref[...]

    row8 = lax.broadcasted_iota(i32, (8, tm), 0)
    gl = jnp.where(row8 < N_GROUPS, logits[0:8, :], NEG)
    gmax = jnp.max(gl, axis=0, keepdims=True)
    grp = jnp.min(jnp.where(gl == gmax, row8, 8), axis=0, keepdims=True)
    p_g = 1.0 / jnp.sum(jnp.exp(gl - gmax), axis=0, keepdims=True)
    sel = logits[8:16, :]
    for g in range(1, N_GROUPS):
        sel = jnp.where(grp == g, logits[8 + 8 * g:16 + 8 * g, :], sel)
    v1 = jnp.max(sel, axis=0, keepdims=True)
    i1 = jnp.min(jnp.where(sel == v1, row8, 8), axis=0, keepdims=True)
    sel2 = jnp.where(row8 == i1, -jnp.inf, sel)
    v2 = jnp.max(sel2, axis=0, keepdims=True)
    i2 = jnp.min(jnp.where(sel2 == v2, row8, 8), axis=0, keepdims=True)
    e21 = jnp.exp(v2 - v1)
    w1 = p_g / (1.0 + e21)
    w2 = w1 * e21
    e1 = grp * EXPERTS_PER_GROUP + i1
    e2 = grp * EXPERTS_PER_GROUP + i2

    row32 = lax.broadcasted_iota(i32, (N_EXPERTS, tm), 0)
    oh1 = row32 == e1
    oh2 = row32 == e2
    both = jnp.where(oh1 | oh2, 1.0, 0.0)
    prefix = jnp.dot(both.astype(bf16), triu_ref[...], preferred_element_type=f32)
    r1 = jnp.sum(jnp.where(oh1, prefix, 0.0), axis=0, keepdims=True).astype(i32)
    r2 = jnp.sum(jnp.where(oh2, prefix, 0.0), axis=0, keepdims=True).astype(i32)
    route = jnp.where(row8 == 0, e1, jnp.where(row8 == 1, e2,
                      jnp.where(row8 == 2, r1, jnp.where(row8 == 3, r2, 0))))
    route_ref[...] = route
    cnt = jnp.sum(both, axis=1, keepdims=True)
    cnt_ref[0] = jnp.broadcast_to(cnt, (N_EXPERTS, LANES)).astype(i32)
    row128 = lax.broadcasted_iota(i32, (LANES, tm), 0)
    w_rows = jnp.where(row128 == 0, w1, jnp.where(row128 == 1, w2, 0.0))
    wcol_ref[...] = w_rows.T


def _outproj(fox2, diff2, x2, mod, g2, w_o, wr_hi, wr_lo, b_r, seq):
    t, d = x2.shape
    tm = ROW_TILE
    tiles_per_batch = seq // tm
    n_tiles = t // tm
    triu = (jnp.arange(tm)[:, None] < jnp.arange(tm)[None, :]).astype(bf16)
    row = lambda i: (i, 0)
    const = lambda i: (0, 0)
    return pl.pallas_call(
        _outproj_kernel,
        grid=(n_tiles,),
        in_specs=[pl.BlockSpec((tm, FOX_WIDTH), row),
                  pl.BlockSpec((tm, DIFF_WIDTH), row),
                  pl.BlockSpec((tm, d), row),
                  pl.BlockSpec((1, 6, d), lambda i: (i // tiles_per_batch, 0, 0)),
                  pl.BlockSpec((1, d), const),
                  pl.BlockSpec((d, d), const),
                  pl.BlockSpec((ROUTER_ROWS, d), const),
                  pl.BlockSpec((ROUTER_ROWS, d), const),
                  pl.BlockSpec((ROUTER_ROWS, 1), const),
                  pl.BlockSpec((tm, tm), const)],
        out_specs=[pl.BlockSpec((tm, d), row),
                   pl.BlockSpec((ROW_CHUNKS * tm, LANES), row),
                   pl.BlockSpec((8, tm), lambda i: (0, i)),
                   pl.BlockSpec((tm, LANES), row),
                   pl.BlockSpec((1, N_EXPERTS, LANES), lambda i: (i, 0, 0))],
        out_shape=[jax.ShapeDtypeStruct((t, d), f32),
                   jax.ShapeDtypeStruct((ROW_CHUNKS * t, LANES), i32),
                   jax.ShapeDtypeStruct((8, t), i32),
                   jax.ShapeDtypeStruct((t, LANES), f32),
                   jax.ShapeDtypeStruct((n_tiles, N_EXPERTS, LANES), i32)],
        compiler_params=_cparams("parallel"),
        name="outproj_router",
    )(fox2, diff2, x2, mod, g2, w_o, wr_hi, wr_lo, b_r, triu)


def _dispatch_kernel(dest_ref, h_ref, xs_in_ref, xs_ref, sem, *, tm):
    del xs_in_ref
    i = pl.program_id(0)
    base = i * (2 * tm)

    def row_copy(r, k):
        d = dest_ref[base + 2 * r + k]
        src = h_ref.at[pl.ds(pl.multiple_of(ROW_CHUNKS * r, ROW_CHUNKS), ROW_CHUNKS), :]
        return pltpu.make_async_copy(src, xs_ref.at[d], sem)

    def issue(r, _):
        row_copy(r, 0).start()
        row_copy(r, 1).start()
        return 0

    lax.fori_loop(0, tm, issue, 0, unroll=8)

    def drain(r, _):
        row_copy(r, 0).wait()
        row_copy(r, 1).wait()
        return 0

    lax.fori_loop(0, tm, drain, 0, unroll=8)


def _dispatch(dest, h_packed, n_slots):
    t = h_packed.shape[0] // ROW_CHUNKS
    tm = GATHER_TILE
    xs0 = jnp.zeros((n_slots, ROW_CHUNKS, LANES), i32)
    grid_spec = pltpu.PrefetchScalarGridSpec(
        num_scalar_prefetch=1,
        grid=(t // tm,),
        in_specs=[pl.BlockSpec((ROW_CHUNKS * tm, LANES), lambda i, d: (i, 0)),
                  pl.BlockSpec(memory_space=pl.ANY)],
        out_specs=pl.BlockSpec(memory_space=pl.ANY),
        scratch_shapes=[pltpu.SemaphoreType.DMA(())],
    )
    return pl.pallas_call(
        functools.partial(_dispatch_kernel, tm=tm),
        grid_spec=grid_spec,
        out_shape=jax.ShapeDtypeStruct((n_slots, ROW_CHUNKS, LANES), i32),
        input_output_aliases={2: 0},
        compiler_params=_cparams("arbitrary"),
        name="moe_dispatch",
    )(dest, h_packed, xs0)


def _experts_kernel(be_ref, nu_ref, xs_ref, wg_ref, wu_ref, wd_ref, y_ref, wg_s, wu_s, wd_s):
    i = pl.program_id(0)
    prev = be_ref[jnp.maximum(i - 1, 0)]
    fresh = jnp.logical_or(i == 0, be_ref[i] != prev)

    @pl.when(jnp.logical_and(fresh, i < nu_ref[0]))
    def _():
        wg_s[...] = wg_ref[0].astype(bf16)
        wu_s[...] = wu_ref[0].astype(bf16)
        wd_s[...] = wd_ref[0].astype(bf16)

    @pl.when(i < nu_ref[0])
    def _():
        lo, hi = _unpack_rows(_load_slabs(xs_ref, MOE_BLOCK))
        xb = jnp.concatenate([lo.astype(bf16), hi.astype(bf16)], axis=-1)
        g = jnp.dot(xb, wg_s[...], preferred_element_type=f32)
        u = jnp.dot(xb, wu_s[...], preferred_element_type=f32)
        hid = (g * (1.0 / (1.0 + jnp.exp(-g)))) * u
        y = jnp.dot(hid.astype(bf16), wd_s[...], preferred_element_type=f32)
        _store_slabs(y_ref, _pack_rows(y))

    @pl.when(i >= nu_ref[0])
    def _():
        y_ref[...] = jnp.zeros_like(y_ref)


def _experts(block_e, n_used, xs, w_gate, w_up, w_down):
    n_slots = xs.shape[0]
    bk = MOE_BLOCK
    grid_spec = pltpu.PrefetchScalarGridSpec(
        num_scalar_prefetch=2,
        grid=(n_slots // bk,),
        in_specs=[pl.BlockSpec((ROW_CHUNKS * bk, LANES), lambda i, be, nu: (i, 0)),
                  pl.BlockSpec((1, D_MODEL, D_EXPERT), lambda i, be, nu: (be[i], 0, 0)),
                  pl.BlockSpec((1, D_MODEL, D_EXPERT), lambda i, be, nu: (be[i], 0, 0)),
                  pl.BlockSpec((1, D_EXPERT, D_MODEL), lambda i, be, nu: (be[i], 0, 0))],
        out_specs=pl.BlockSpec((ROW_CHUNKS * bk, LANES), lambda i, be, nu: (i, 0)),
        scratch_shapes=[pltpu.VMEM((D_MODEL, D_EXPERT), bf16),
                        pltpu.VMEM((D_MODEL, D_EXPERT), bf16),
                        pltpu.VMEM((D_EXPERT, D_MODEL), bf16)],
    )
    return pl.pallas_call(
        _experts_kernel,
        grid_spec=grid_spec,
        out_shape=jax.ShapeDtypeStruct((ROW_CHUNKS * n_slots, LANES), i32),
        compiler_params=_cparams("arbitrary"),
        name="moe_experts",
    )(block_e, n_used, xs.reshape(-1, LANES), w_gate, w_up, w_down)


def _combine_kernel(dest_ref, y_ref, x1_ref, wcol_ref, mod_ref, g_ref, o_ref, buf, sems, *, tm):
    i = pl.program_id(0)
    n = pl.num_programs(0)

    def row_copy(tile, slot, r, k):
        d = dest_ref[tile * (2 * tm) + 2 * r + k]
        dst = buf.at[slot, k, pl.ds(pl.multiple_of(ROW_CHUNKS * r, ROW_CHUNKS), ROW_CHUNKS), :]
        return pltpu.make_async_copy(y_ref.at[d], dst, sems.at[slot])

    def issue_tile(tile, slot):
        def body(r, _):
            row_copy(tile, slot, r, 0).start()
            row_copy(tile, slot, r, 1).start()
            return 0
        lax.fori_loop(0, tm, body, 0, unroll=8)

    @pl.when(i == 0)
    def _():
        issue_tile(0, 0)

    @pl.when(i + 1 < n)
    def _():
        issue_tile(i + 1, (i + 1) % 2)

    slot = i % 2

    def drain(r, _):
        row_copy(i, slot, r, 0).wait()
        row_copy(i, slot, r, 1).wait()
        return 0

    lax.fori_loop(0, tm, drain, 0, unroll=8)

    wc = wcol_ref[...]
    w0 = wc[:, 0:1]
    w1 = wc[:, 1:2]
    lo0, hi0 = _unpack_rows(_load_slabs(buf.at[slot, 0], tm))
    lo1, hi1 = _unpack_rows(_load_slabs(buf.at[slot, 1], tm))
    moe = jnp.concatenate([w0 * lo0 + w1 * lo1, w0 * hi0 + w1 * hi1], axis=-1)
    x = x1_ref[...] + mod_ref[0, 5:6, :] * moe
    ms = jnp.mean(x * x, axis=-1, keepdims=True)
    o_ref[...] = x * lax.rsqrt(ms + NORM_EPS) * g_ref[...]


def _combine(dest, y_packed, x1, wcol, mod, g_f, seq):
    t, d = x1.shape
    tm = GATHER_TILE
    tiles_per_batch = seq // tm
    grid_spec = pltpu.PrefetchScalarGridSpec(
        num_scalar_prefetch=1,
        grid=(t // tm,),
        in_specs=[pl.BlockSpec(memory_space=pl.ANY),
                  pl.BlockSpec((tm, d), lambda i, ds: (i, 0)),
                  pl.BlockSpec((tm, LANES), lambda i, ds: (i, 0)),
                  pl.BlockSpec((1, 6, d), lambda i, ds: (i // tiles_per_batch, 0, 0)),
                  pl.BlockSpec((1, d), lambda i, ds: (0, 0))],
        out_specs=pl.BlockSpec((tm, d), lambda i, ds: (i, 0)),
        scratch_shapes=[pltpu.VMEM((2, 2, ROW_CHUNKS * tm, LANES), i32),
                        pltpu.SemaphoreType.DMA((2,))],
    )
    return pl.pallas_call(
        functools.partial(_combine_kernel, tm=tm),
        grid_spec=grid_spec,
        out_shape=jax.ShapeDtypeStruct((t, d), f32),
        compiler_params=_cparams("arbitrary"),
        name="moe_combine",
    )(dest, y_packed.reshape(-1, ROW_CHUNKS, LANES), x1, wcol, mod, g_f)


def _route_tables(route, cnt, tm):
    n_tiles = cnt.shape[0]
    t = route.shape[1]
    bk = MOE_BLOCK
    n_blocks = (2 * t) // bk + N_EXPERTS
    ti = jnp.arange(n_tiles)
    ei = jnp.arange(N_EXPERTS)
    tile_base = jnp.sum(jnp.where((ti[:, None] > ti[None, :])[:, :, None], cnt[None], 0), axis=1)
    total = jnp.sum(cnt, axis=0)
    padded = (total + bk - 1) // bk * bk
    pends = jnp.sum(jnp.where(ei[None, :] <= ei[:, None], padded[None, :], 0), axis=1)
    base = (pends - padded)[None, :] + tile_base
    base_tok = jnp.broadcast_to(base[:, None, :], (n_tiles, tm, N_EXPERTS)).reshape(t, N_EXPERTS)

    def slot_base(e):
        return jnp.sum(jnp.where(ei[None, :] == e[:, None], base_tok, 0), axis=1)

    d0 = slot_base(route[0]) + route[2]
    d1 = slot_base(route[1]) + route[3]
    dest = jnp.stack([d0, d1], axis=1).reshape(-1).astype(i32)
    block_start = jnp.arange(n_blocks, dtype=i32) * bk
    block_e = jnp.clip(jnp.sum(pends[None, :] <= block_start[:, None], axis=1),
                       0, N_EXPERTS - 1).astype(i32)
    n_used = (pends[-1] // bk).astype(i32).reshape(1)
    return dest, block_e, n_used, n_blocks * bk


def _layer(x2, c, seq, l, ada_w, ada_b, norm1_g, w_in, b_f, lam_q1, lam_k1, lam_q2, lam_k2,
           subln_g, w_o, norm2_g, w_rg, b_rg, w_re, b_re, w_gate, w_up, w_down):
    t, d = x2.shape
    bsz = t // seq
    lam_init = 0.8 - 0.6 * math.exp(-0.3 * l)
    mod = _ada(c, ada_w, ada_b).reshape(bsz, 6, d)

    z0 = 3 * FOX_WIDTH
    w_main = jnp.concatenate([w_in[:, :z0], w_in[:, z0 + FOX_HEADS:]], axis=1).astype(bf16)
    w_fz = jnp.pad(w_in[:, z0:z0 + FOX_HEADS], ((0, 0), (0, LANES - FOX_HEADS))).astype(bf16)
    b_fz = jnp.pad(b_f, (0, LANES - FOX_HEADS)).reshape(1, LANES)
    proj, cum = _inproj(x2, mod, norm1_g.reshape(1, d), w_main, w_fz, b_fz, seq)
    proj3 = proj.reshape(bsz, seq, -1)

    fox = _fox_attention(proj3, cum.reshape(bsz, seq, LANES))
    lam_vecs = jnp.pad(jnp.stack([lam_q1, lam_k1, lam_q2, lam_k2]).astype(f32),
                       ((0, 4), (0, LANES - HEAD_DIM)))
    diff = _diff_attention(proj3, lam_vecs, subln_g.reshape(1, LANES), lam_init)

    w_r = jnp.concatenate([w_rg.T, jnp.zeros((8 - N_GROUPS, d), f32), w_re.T], axis=0)
    wr_hi = w_r.astype(bf16)
    wr_lo = (w_r - wr_hi.astype(f32)).astype(bf16)
    b_r = jnp.concatenate([b_rg, jnp.zeros((8 - N_GROUPS,), f32), b_re]).reshape(ROUTER_ROWS, 1)
    x1, h_packed, route, wcol, cnt = _outproj(
        fox.reshape(t, FOX_WIDTH), diff.reshape(t, DIFF_WIDTH), x2, mod, norm2_g.reshape(1, d),
        w_o.astype(bf16), wr_hi, wr_lo, b_r, seq)

    dest, block_e, n_used, n_slots = _route_tables(route, cnt[:, :, 0], ROW_TILE)
    xs = _dispatch(dest, h_packed, n_slots)
    y_packed = _experts(block_e, n_used, xs, w_gate, w_up, w_down)
    return dest, y_packed, x1, wcol, mod


def kernel(x, c, ada_w, ada_b, norm1_g, w_in, b_f, lam_q1, lam_k1, lam_q2, lam_k2, subln_g, w_o,
           norm2_g, w_rg, b_rg, w_re, b_re, w_gate, w_up, w_down, norm_f_g):
    bsz, seq, d = x.shape
    depth = ada_w.shape[0]
    assert depth == 1 and d == D_MODEL and seq % ROW_TILE == 0 and seq % ATT_WIDE == 0
    x2 = x.reshape(bsz * seq, d)
    dest, y_packed, x1, wcol, mod = _layer(
        x2, c, seq, 0, ada_w[0], ada_b[0], norm1_g[0], w_in[0], b_f[0], lam_q1[0], lam_k1[0],
        lam_q2[0], lam_k2[0], subln_g[0], w_o[0], norm2_g[0], w_rg[0], b_rg[0], w_re[0], b_re[0],
        w_gate[0], w_up[0], w_down[0])
    out = _combine(dest, y_packed, x1, wcol, mod, norm_f_g.reshape(1, d), seq)
    return out.reshape(bsz, seq, d)
```

```python
import functools
import math

import jax
import jax.numpy as jnp
from jax import lax
from jax.experimental import pallas as pl
from jax.experimental.pallas import tpu as pltpu

f32 = jnp.float32
bf16 = jnp.bfloat16
i32 = jnp.int32

D_MODEL = 1024
HEAD_DIM = 64
FOX_HEADS = 8
FOX_WIDTH = FOX_HEADS * HEAD_DIM
DIFF_HEADS = 4
DIFF_QK_WIDTH = DIFF_HEADS * 2 * HEAD_DIM
DIFF_WIDTH = DIFF_HEADS * 2 * HEAD_DIM
CHUNK = 64
N_GROUPS = 4
EXPERTS_PER_GROUP = 8
N_EXPERTS = N_GROUPS * EXPERTS_PER_GROUP
D_EXPERT = 512
NORM_EPS = 1e-6
SUBLN_EPS = 1e-5

LANES = 128
LOG2E = 1.4426950408889634
Q_SCALE = HEAD_DIM ** -0.5 * LOG2E
NEG = -1e30
HALF = D_MODEL // 2
ROW_CHUNKS = HALF // LANES

ROW_TILE = 512
ATT_TQ = 512
ATT_WIDE = 1024
ATT_COLS = 256
MOE_BLOCK = 256
GATHER_TILE = 256
VMEM_LIMIT = 48 * 1024 * 1024


def _cparams(*sem):
    return pltpu.CompilerParams(dimension_semantics=sem, vmem_limit_bytes=VMEM_LIMIT)


def _split3(c):
    hi = c.astype(bf16).astype(f32)
    r = c - hi
    mid = r.astype(bf16).astype(f32)
    lo = r - mid
    return hi, mid, lo


def _pack_rows(y):
    a = pltpu.bitcast(y[:, :HALF].astype(bf16).astype(f32), i32)
    b = pltpu.bitcast(y[:, HALF:].astype(bf16).astype(f32), i32)
    return lax.shift_right_logical(a, 16) | (b & jnp.int32(-65536))


def _store_slabs(ref, packed):
    n = packed.shape[0]
    for j in range(ROW_CHUNKS):
        ref[pl.ds(j, n, stride=ROW_CHUNKS), :] = packed[:, j * LANES:(j + 1) * LANES]


def _load_slabs(ref, n):
    return jnp.concatenate([ref[pl.ds(j, n, stride=ROW_CHUNKS), :] for j in range(ROW_CHUNKS)], axis=-1)


def _unpack_rows(w):
    lo = pltpu.bitcast(lax.shift_left(w, 16), f32)
    hi = pltpu.bitcast(w & jnp.int32(-65536), f32)
    return lo, hi


def _ada_kernel(c_ref, w_ref, b_ref, o_ref):
    c = c_ref[...]
    w = w_ref[...]
    c_hi = c.astype(bf16)
    c_lo = (c - c_hi.astype(f32)).astype(bf16)
    w_hi = w.astype(bf16)
    w_lo = (w - w_hi.astype(f32)).astype(bf16)
    acc = jnp.dot(c_hi, w_hi, preferred_element_type=f32)
    acc += jnp.dot(c_hi, w_lo, preferred_element_type=f32)
    acc += jnp.dot(c_lo, w_hi, preferred_element_type=f32)
    o_ref[...] = acc + b_ref[...]


def _ada(c, w, b):
    bsz, d = c.shape
    n = w.shape[1]
    tn = 1024
    return pl.pallas_call(
        _ada_kernel,
        grid=(n // tn,),
        in_specs=[pl.BlockSpec((bsz, d), lambda j: (0, 0)),
                  pl.BlockSpec((d, tn), lambda j: (0, j)),
                  pl.BlockSpec((1, tn), lambda j: (0, j))],
        out_specs=pl.BlockSpec((bsz, tn), lambda j: (0, j)),
        out_shape=jax.ShapeDtypeStruct((bsz, n), f32),
        compiler_params=_cparams("parallel"),
        name="ada_mod",
    )(c, w, b.reshape(1, n))


def _inproj_kernel(x_ref, mod_ref, g_ref, wm_ref, wz_ref, bf_ref, tril_ref,
                   proj_ref, cum_ref, carry_ref, *, tiles_per_batch):
    i = pl.program_id(0)
    x = x_ref[...]
    ms = jnp.mean(x * x, axis=-1, keepdims=True)
    y = x * lax.rsqrt(ms + NORM_EPS) * g_ref[...]
    h = (y * (1.0 + mod_ref[0, 1:2, :]) + mod_ref[0, 0:1, :]).astype(bf16)
    n_chunks = proj_ref.shape[1] // 512
    for j in range(n_chunks):
        acc = jnp.dot(h, wm_ref[:, j * 512:(j + 1) * 512], preferred_element_type=f32)
        if j in (0, 3):
            acc = acc * Q_SCALE
        proj_ref[:, j * 512:(j + 1) * 512] = acc.astype(bf16)
    fz = jnp.dot(h, wz_ref[...], preferred_element_type=f32) + bf_ref[...]
    ls = (jnp.minimum(fz, 0.0) - jnp.log(1.0 + jnp.exp(-jnp.abs(fz)))) * LOG2E
    hi, mid, lo = _split3(ls)
    tril = tril_ref[...]
    local = jnp.dot(tril, hi.astype(bf16), preferred_element_type=f32)
    local += jnp.dot(tril, mid.astype(bf16), preferred_element_type=f32)
    local += jnp.dot(tril, lo.astype(bf16), preferred_element_type=f32)

    @pl.when(i % tiles_per_batch == 0)
    def _():
        carry_ref[...] = jnp.zeros_like(carry_ref)

    cum = local + carry_ref[0:1, :]
    cum_ref[...] = cum
    tm = x.shape[0]
    carry_ref[0:1, :] = cum[tm - 1:tm, :]


def _inproj(x2, mod, g1, w_main, w_fz, b_fz, seq):
    t, d = x2.shape
    tm = ROW_TILE
    tiles_per_batch = seq // tm
    n_main = w_main.shape[1]
    tril = (jnp.arange(tm)[:, None] >= jnp.arange(tm)[None, :]).astype(bf16)
    return pl.pallas_call(
        functools.partial(_inproj_kernel, tiles_per_batch=tiles_per_batch),
        grid=(t // tm,),
        in_specs=[pl.BlockSpec((tm, d), lambda i: (i, 0)),
                  pl.BlockSpec((1, 6, d), lambda i: (i // tiles_per_batch, 0, 0)),
                  pl.BlockSpec((1, d), lambda i: (0, 0)),
                  pl.BlockSpec((d, n_main), lambda i: (0, 0)),
                  pl.BlockSpec((d, LANES), lambda i: (0, 0)),
                  pl.BlockSpec((1, LANES), lambda i: (0, 0)),
                  pl.BlockSpec((tm, tm), lambda i: (0, 0))],
        out_specs=[pl.BlockSpec((tm, n_main), lambda i: (i, 0)),
                   pl.BlockSpec((tm, LANES), lambda i: (i, 0))],
        out_shape=[jax.ShapeDtypeStruct((t, n_main), bf16),
                   jax.ShapeDtypeStruct((t, LANES), f32)],
        scratch_shapes=[pltpu.VMEM((8, LANES), f32)],
        compiler_params=_cparams("arbitrary"),
        name="inproj",
    )(x2, mod, g1, w_main, w_fz, b_fz, tril)


def _aug(data, lane, low_map, first, last):
    base = 64 if low_map else 0
    out = jnp.zeros_like(data)
    for n, val in enumerate(tuple(first) + tuple(last)):
        out = jnp.where(lane == base + n, val, out)
    keep = (lane < 64) if low_map else (lane >= 64)
    return jnp.where(keep, data, out)


_ONES3 = (1.0, 1.0, 1.0)


def _q_aug(q, lane, low_map, c):
    return _aug(q, lane, low_map, _split3(c), _ONES3).astype(bf16)


def _k_aug(k, lane, low_map, c):
    hi, mid, lo = _split3(c)
    return _aug(k, lane, low_map, _ONES3, (-hi, -mid, -lo)).astype(bf16)


ONES_ROWS = 16


def _vt_aug(vt):
    row = lax.broadcasted_iota(i32, (ONES_ROWS, vt.shape[1]), 0)
    extra = jnp.where(row == 0, 1.0, 0.0).astype(vt.dtype)
    return jnp.concatenate([vt, extra], axis=0).astype(bf16)


def _flash_scratch(tq, acc_rows):
    return ([pltpu.VMEM((ATT_WIDE, tq), f32)] * 4
            + [pltpu.VMEM((acc_rows, tq), f32)] * 2 + [pltpu.VMEM((1, tq), f32)] * 2)


def _flash_pair(q_a, q_b, ka_s, kb_s, vta_s, vtb_s, mask_ref, qi, tq, finish, scratch):
    s0a, s0b, s1a, s1b, acc_a, acc_b, m_a, m_b = scratch
    nt = (((1,), (1,)), ((), ()))
    wide = ATT_WIDE
    per_wide = wide // tq

    maps = ((q_a, ka_s, vta_s, m_a, acc_a), (q_b, kb_s, vtb_s, m_b, acc_b))
    items = [(x, pl.ds(c * ATT_COLS, ATT_COLS)) for x in range(2) for c in range(tq // ATT_COLS)]

    def scores_to(bufs, item, start):
        x, cols = item
        q, k_s = maps[x][0], maps[x][1]
        rows = pl.ds(pl.multiple_of(start, wide), wide)
        q_cols = q[cols.start:cols.start + cols.size, :]
        bufs[x][:, cols] = lax.dot_general(k_s[rows, :], q_cols, nt, preferred_element_type=f32)

    def update(bufs, item, start, width=wide, mask=None):
        x, cols = item
        _, _, vt_s, m_ref, acc_ref = maps[x]
        s = bufs[x][:width, cols]
        if mask is not None:
            s = s + mask[:, cols.start:cols.start + cols.size]
        vt = vt_s[:, pl.ds(pl.multiple_of(start, wide), width)]
        m = m_ref[:, cols]
        m_new = jnp.maximum(m, jnp.max(s, axis=0, keepdims=True))
        p = jnp.exp2(s - m_new)
        acc_ref[:, cols] = (jnp.exp2(m - m_new) * acc_ref[:, cols]
                            + jnp.dot(vt, p.astype(bf16), preferred_element_type=f32))
        m_ref[:, cols] = m_new

    buf0, buf1 = (s0a, s0b), (s1a, s1b)
    m_a[...] = jnp.full(m_a.shape, NEG, f32)
    m_b[...] = jnp.full(m_b.shape, NEG, f32)
    acc_a[...] = jnp.zeros(acc_a.shape, f32)
    acc_b[...] = jnp.zeros(acc_b.shape, f32)
    n_wide = qi // per_wide
    for item in items:
        scores_to(buf0, item, 0)

    def pair(jj, _):
        t0 = 2 * jj * wide
        for item in items:
            update(buf0, item, t0)
            scores_to(buf1, item, t0 + wide)
        for item in items:
            update(buf1, item, t0 + wide)
            scores_to(buf0, item, t0 + 2 * wide)
        return 0

    lax.fori_loop(0, n_wide // 2, pair, 0)

    @pl.when(n_wide % 2 == 1)
    def _():
        for item in items:
            update(buf0, item, (n_wide - 1) * wide)
            scores_to(buf0, item, n_wide * wide)

    for r in range(per_wide):
        @pl.when(qi % per_wide == r)
        def _(r=r):
            width = (r + 1) * tq
            mask = mask_ref[wide - width:, :]
            for item in items:
                update(buf0, item, n_wide * wide, width, mask)
            finish(acc_a[...], acc_b[...])


def _fox_kernel(q_ref, k_ref, v_ref, cum_ref, mask_ref, o_ref, ka_s, kb_s, vta_s, vtb_s,
                *flash_scratch, seq, tq):
    hp = pl.program_id(1)
    qi = pl.program_id(2)
    lane = lax.broadcasted_iota(i32, (1, LANES), 1)

    def head_cums(cm):
        c_a = jnp.sum(jnp.where(lane == 2 * hp, cm, 0.0), axis=-1, keepdims=True)
        c_b = jnp.sum(jnp.where(lane == 2 * hp + 1, cm, 0.0), axis=-1, keepdims=True)
        return c_a, c_b

    @pl.when(qi == 0)
    def _():
        for c in range(seq // 512):
            rows = pl.ds(c * 512, 512)
            kk = k_ref[0, rows, :].astype(f32)
            c_a, c_b = head_cums(cum_ref[0, rows, :])
            ka_s[rows, :] = _k_aug(kk, lane, True, c_a)
            kb_s[rows, :] = _k_aug(kk, lane, False, c_b)
            vt = v_ref[0, rows, :].astype(f32).T
            vta_s[:, rows] = _vt_aug(vt[:HEAD_DIM])
            vtb_s[:, rows] = _vt_aug(vt[HEAD_DIM:])

    q = q_ref[0].astype(f32)
    c_a, c_b = head_cums(cum_ref[0, pl.ds(pl.multiple_of(qi * tq, tq), tq), :])
    q_a = _q_aug(q, lane, True, c_a)
    q_b = _q_aug(q, lane, False, c_b)

    def finish(acc_a, acc_b):
        o_a = acc_a[:HEAD_DIM] / acc_a[HEAD_DIM:HEAD_DIM + 1]
        o_b = acc_b[:HEAD_DIM] / acc_b[HEAD_DIM:HEAD_DIM + 1]
        o_ref[0] = jnp.concatenate([o_a, o_b], axis=0).T.astype(bf16)

    _flash_pair(q_a, q_b, ka_s, kb_s, vta_s, vtb_s, mask_ref, qi, tq, finish, flash_scratch)


def _tail_mask(diag):
    diag_t = jnp.swapaxes(diag, -1, -2)
    pad = [(0, 0)] * (diag.ndim - 2) + [(ATT_WIDE - diag.shape[-1], 0), (0, 0)]
    return jnp.pad(diag_t, pad)


def _fox_attention(proj3, cum3):
    bsz, seq, _ = proj3.shape
    tq, tk = ATT_TQ, ATT_WIDE
    n_pairs = FOX_HEADS // 2
    r = jnp.arange(tq)
    mask = _tail_mask(jnp.where(r[:, None] >= r[None, :], 0.0, NEG).astype(f32))
    return pl.pallas_call(
        functools.partial(_fox_kernel, seq=seq, tq=tq),
        grid=(bsz, n_pairs, seq // tq),
        in_specs=[pl.BlockSpec((1, tq, LANES), lambda b, h, q: (b, q, h)),
                  pl.BlockSpec((1, seq, LANES), lambda b, h, q: (b, 0, 4 + h)),
                  pl.BlockSpec((1, seq, LANES), lambda b, h, q: (b, 0, 8 + h)),
                  pl.BlockSpec((1, seq, LANES), lambda b, h, q: (b, 0, 0)),
                  pl.BlockSpec((tk, tq), lambda b, h, q: (0, 0))],
        out_specs=pl.BlockSpec((1, tq, LANES), lambda b, h, q: (b, q, h)),
        out_shape=jax.ShapeDtypeStruct((bsz, seq, FOX_WIDTH), bf16),
        scratch_shapes=[pltpu.VMEM((seq, LANES), bf16), pltpu.VMEM((seq, LANES), bf16),
                        pltpu.VMEM((HEAD_DIM + ONES_ROWS, seq), bf16),
                        pltpu.VMEM((HEAD_DIM + ONES_ROWS, seq), bf16)]
        + _flash_scratch(tq, HEAD_DIM + ONES_ROWS),
        compiler_params=_cparams("parallel", "parallel", "arbitrary"),
        name="fox_attn",
    )(proj3, proj3, proj3, cum3, mask)


def _diff_kernel(slope_ref, q_ref, k_ref, v_ref, mask_ref, lamv_ref, g_ref, o_ref,
                 ka_s, kb_s, vt_s, *flash_scratch, seq, tq, lam_init):
    h = pl.program_id(1)
    qi = pl.program_id(2)
    lane = lax.broadcasted_iota(i32, (1, LANES), 1)
    slope = slope_ref[h]

    def pos_bias(start, n):
        pos = (start + lax.broadcasted_iota(i32, (n, 1), 0)).astype(f32)
        return -(slope * pos)

    @pl.when(qi == 0)
    def _():
        for c in range(seq // 512):
            rows = pl.ds(c * 512, 512)
            kk = k_ref[0, rows, :].astype(f32)
            cb = pos_bias(c * 512, 512)
            ka_s[rows, :] = _k_aug(kk, lane, True, cb)
            kb_s[rows, :] = _k_aug(kk, lane, False, cb)
            vt_s[:, rows] = _vt_aug(v_ref[0, rows, :].astype(f32).T)

    q = q_ref[0].astype(f32)
    cq = pos_bias(qi * tq, tq)
    q_a = _q_aug(q, lane, True, cq)
    q_b = _q_aug(q, lane, False, cq)

    def finish(acc_a, acc_b):
        lv = lamv_ref[...]
        s1 = jnp.sum(lv[0:1, :] * lv[1:2, :], axis=-1, keepdims=True)
        s2 = jnp.sum(lv[2:3, :] * lv[3:4, :], axis=-1, keepdims=True)
        lam = jnp.exp(s1) - jnp.exp(s2) + lam_init
        o_a = acc_a[:LANES] / acc_a[LANES:LANES + 1]
        o_b = acc_b[:LANES] / acc_b[LANES:LANES + 1]
        d = (o_a - lam * o_b).T
        y = d * lax.rsqrt(jnp.mean(d * d, axis=-1, keepdims=True) + SUBLN_EPS) * g_ref[...]
        o_ref[0] = (y * (1.0 - lam_init)).astype(bf16)

    _flash_pair(q_a, q_b, ka_s, kb_s, vt_s, vt_s, mask_ref.at[0], qi, tq, finish, flash_scratch)


def _diff_attention(proj3, lam_vecs, subln_g, lam_init):
    bsz, seq, _ = proj3.shape
    tq, tk = ATT_TQ, ATT_WIDE
    slopes = jnp.asarray([2.0 ** (-8.0 * (i + 1) / DIFF_HEADS) for i in range(DIFF_HEADS)], f32) * LOG2E
    r = jnp.arange(tq)
    tq_i, tk_i = r[:, None], r[None, :]
    chunk_ok = (tq_i // CHUNK) >= (tk_i // CHUNK)
    ahead = jnp.maximum(tk_i - tq_i, 0).astype(f32)
    mask = _tail_mask(jnp.where(chunk_ok[None], -2.0 * slopes[:, None, None] * ahead[None], NEG).astype(f32))
    grid_spec = pltpu.PrefetchScalarGridSpec(
        num_scalar_prefetch=1,
        grid=(bsz, DIFF_HEADS, seq // tq),
        in_specs=[pl.BlockSpec((1, tq, LANES), lambda b, h, q, s: (b, q, 12 + h)),
                  pl.BlockSpec((1, seq, LANES), lambda b, h, q, s: (b, 0, 16 + h)),
                  pl.BlockSpec((1, seq, LANES), lambda b, h, q, s: (b, 0, 20 + h)),
                  pl.BlockSpec((1, tk, tq), lambda b, h, q, s: (h, 0, 0)),
                  pl.BlockSpec((8, LANES), lambda b, h, q, s: (0, 0)),
                  pl.BlockSpec((1, LANES), lambda b, h, q, s: (0, 0))],
        out_specs=pl.BlockSpec((1, tq, LANES), lambda b, h, q, s: (b, q, h)),
        scratch_shapes=[pltpu.VMEM((seq, LANES), bf16), pltpu.VMEM((seq, LANES), bf16),
                        pltpu.VMEM((LANES + ONES_ROWS, seq), bf16)]
        + _flash_scratch(tq, LANES + ONES_ROWS),
    )
    return pl.pallas_call(
        functools.partial(_diff_kernel, seq=seq, tq=tq, lam_init=lam_init),
        grid_spec=grid_spec,
        out_shape=jax.ShapeDtypeStruct((bsz, seq, DIFF_WIDTH), bf16),
        compiler_params=_cparams("parallel", "parallel", "arbitrary"),
        name="diff_attn",
    )(slopes, proj3, proj3, proj3, mask, lam_vecs, subln_g)


ROUTER_ROWS = 8 + N_EXPERTS


def _outproj_kernel(fox_ref, diff_ref, x_ref, mod_ref, g_ref, wo_ref, wr_hi_ref, wr_lo_ref,
                    br_ref, triu_ref, x1_ref, hp_ref, route_ref, wcol_ref, cnt_ref):
    tm = x_ref.shape[0]
    y = jnp.dot(fox_ref[...], wo_ref[0:FOX_WIDTH, :], preferred_element_type=f32)
    y += jnp.dot(diff_ref[...], wo_ref[FOX_WIDTH:, :], preferred_element_type=f32)
    x1 = x_ref[...] + mod_ref[0, 2:3, :] * y
    x1_ref[...] = x1
    ms = jnp.mean(x1 * x1, axis=-1, keepdims=True)
    h2 = (x1 * lax.rsqrt(ms + NORM_EPS) * g_ref[...]) * (1.0 + mod_ref[0, 4:5, :]) + mod_ref[0, 3:4, :]
    _store_slabs(hp_ref, _pack_rows(h2))

    nt = (((1,), (1,)), ((), ()))
    h_hi = h2.astype(bf16)
    h_lo = (h2 - h_hi.astype(f32)).astype(bf16)
    logits = lax.dot_general(wr_hi_ref[...], h_hi, nt, preferred_element_type=f32)
    logits += lax.dot_general(wr_lo_ref[...], h_hi, nt, preferred_element_type=f32)
    logits += lax.dot_general(wr_hi_ref[...], h_lo, nt, preferred_element_type=f32)
    logits = logits + br_ref[...]

    row8 = lax.broadcasted_iota(i32, (8, tm), 0)
    gl = jnp.where(row8 < N_GROUPS, logits[0:8, :], NEG)
    gmax = jnp.max(gl, axis=0, keepdims=True)
    grp = jnp.min(jnp.where(gl == gmax, row8, 8), axis=0, keepdims=True)
    p_g = 1.0 / jnp.sum(jnp.exp(gl - gmax), axis=0, keepdims=True)
    sel = logits[8:16, :]
    for g in range(1, N_GROUPS):
        sel = jnp.where(grp == g, logits[8 + 8 * g:16 + 8 * g, :], sel)
    v1 = jnp.max(sel, axis=0, keepdims=True)
    i1 = jnp.min(jnp.where(sel == v1, row8, 8), axis=0, keepdims=True)
    sel2 = jnp.where(row8 == i1, -jnp.inf, sel)
    v2 = jnp.max(sel2, axis=0, keepdims=True)
    i2 = jnp.min(jnp.where(sel2 == v2, row8, 8), axis=0, keepdims=True)
    e21 = jnp.exp(v2 - v1)
    w1 = p_g / (1.0 + e21)
    w2 = w1 * e21
    e1 = grp * EXPERTS_PER_GROUP + i1
    e2 = grp * EXPERTS_PER_GROUP + i2

    row32 = lax.broadcasted_iota(i32, (N_EXPERTS, tm), 0)
    oh1 = row32 == e1
    oh2 = row32 == e2
    both = jnp.where(oh1 | oh2, 1.0, 0.0)
    prefix = jnp.dot(both.astype(bf16), triu_ref[...], preferred_element_type=f32)
    r1 = jnp.sum(jnp.where(oh1, prefix, 0.0), axis=0, keepdims=True).astype(i32)
    r2 = jnp.sum(jnp.where(oh2, prefix, 0.0), axis=0, keepdims=True).astype(i32)
    route = jnp.where(row8 == 0, e1, jnp.where(row8 == 1, e2,
                      jnp.where(row8 == 2, r1, jnp.where(row8 == 3, r2, 0))))
    route_ref[...] = route
    cnt = jnp.sum(both, axis=1, keepdims=True)
    cnt_ref[0] = jnp.broadcast_to(cnt, (N_EXPERTS, LANES)).astype(i32)
    row128 = lax.broadcasted_iota(i32, (LANES, tm), 0)
    w_rows = jnp.where(row128 == 0, w1, jnp.where(row128 == 1, w2, 0.0))
    wcol_ref[...] = w_rows.T


def _outproj(fox2, diff2, x2, mod, g2, w_o, wr_hi, wr_lo, b_r, seq):
    t, d = x2.shape
    tm = ROW_TILE
    tiles_per_batch = seq // tm
    n_tiles = t // tm
    triu = (jnp.arange(tm)[:, None] < jnp.arange(tm)[None, :]).astype(bf16)
    row = lambda i: (i, 0)
    const = lambda i: (0, 0)
    return pl.pallas_call(
        _outproj_kernel,
        grid=(n_tiles,),
        in_specs=[pl.BlockSpec((tm, FOX_WIDTH), row),
                  pl.BlockSpec((tm, DIFF_WIDTH), row),
                  pl.BlockSpec((tm, d), row),
                  pl.BlockSpec((1, 6, d), lambda i: (i // tiles_per_batch, 0, 0)),
                  pl.BlockSpec((1, d), const),
                  pl.BlockSpec((d, d), const),
                  pl.BlockSpec((ROUTER_ROWS, d), const),
                  pl.BlockSpec((ROUTER_ROWS, d), const),
                  pl.BlockSpec((ROUTER_ROWS, 1), const),
                  pl.BlockSpec((tm, tm), const)],
        out_specs=[pl.BlockSpec((tm, d), row),
                   pl.BlockSpec((ROW_CHUNKS * tm, LANES), row),
                   pl.BlockSpec((8, tm), lambda i: (0, i)),
                   pl.BlockSpec((tm, LANES), row),
                   pl.BlockSpec((1, N_EXPERTS, LANES), lambda i: (i, 0, 0))],
        out_shape=[jax.ShapeDtypeStruct((t, d), f32),
                   jax.ShapeDtypeStruct((ROW_CHUNKS * t, LANES), i32),
                   jax.ShapeDtypeStruct((8, t), i32),
                   jax.ShapeDtypeStruct((t, LANES), f32),
                   jax.ShapeDtypeStruct((n_tiles, N_EXPERTS, LANES), i32)],
        compiler_params=_cparams("parallel"),
        name="outproj_router",
    )(fox2, diff2, x2, mod, g2, w_o, wr_hi, wr_lo, b_r, triu)


def _dispatch_kernel(dest_ref, h_ref, xs_in_ref, xs_ref, sem, *, tm):
    del xs_in_ref
    i = pl.program_id(0)
    base = i * (2 * tm)

    def row_copy(r, k):
        d = dest_ref[base + 2 * r + k]
        src = h_ref.at[pl.ds(pl.multiple_of(ROW_CHUNKS * r, ROW_CHUNKS), ROW_CHUNKS), :]
        return pltpu.make_async_copy(src, xs_ref.at[d], sem)

    def issue(r, _):
        row_copy(r, 0).start()
        row_copy(r, 1).start()
        return 0

    lax.fori_loop(0, tm, issue, 0, unroll=8)

    def drain(r, _):
        row_copy(r, 0).wait()
        row_copy(r, 1).wait()
        return 0

    lax.fori_loop(0, tm, drain, 0, unroll=8)


def _dispatch(dest, h_packed, n_slots):
    t = h_packed.shape[0] // ROW_CHUNKS
    tm = GATHER_TILE
    xs0 = jnp.zeros((n_slots, ROW_CHUNKS, LANES), i32)
    grid_spec = pltpu.PrefetchScalarGridSpec(
        num_scalar_prefetch=1,
        grid=(t // tm,),
        in_specs=[pl.BlockSpec((ROW_CHUNKS * tm, LANES), lambda i, d: (i, 0)),
                  pl.BlockSpec(memory_space=pl.ANY)],
        out_specs=pl.BlockSpec(memory_space=pl.ANY),
        scratch_shapes=[pltpu.SemaphoreType.DMA(())],
    )
    return pl.pallas_call(
        functools.partial(_dispatch_kernel, tm=tm),
        grid_spec=grid_spec,
        out_shape=jax.ShapeDtypeStruct((n_slots, ROW_CHUNKS, LANES), i32),
        input_output_aliases={2: 0},
        compiler_params=_cparams("arbitrary"),
        name="moe_dispatch",
    )(dest, h_packed, xs0)


def _experts_kernel(be_ref, nu_ref, xs_ref, wg_ref, wu_ref, wd_ref, y_ref, wg_s, wu_s, wd_s):
    i = pl.program_id(0)
    prev = be_ref[jnp.maximum(i - 1, 0)]
    fresh = jnp.logical_or(i == 0, be_ref[i] != prev)

    @pl.when(jnp.logical_and(fresh, i < nu_ref[0]))
    def _():
        wg_s[...] = wg_ref[0].astype(bf16)
        wu_s[...] = wu_ref[0].astype(bf16)
        wd_s[...] = wd_ref[0].astype(bf16)

    @pl.when(i < nu_ref[0])
    def _():
        lo, hi = _unpack_rows(_load_slabs(xs_ref, MOE_BLOCK))
        xb = jnp.concatenate([lo.astype(bf16), hi.astype(bf16)], axis=-1)
        g = jnp.dot(xb, wg_s[...], preferred_element_type=f32)
        u = jnp.dot(xb, wu_s[...], preferred_element_type=f32)
        hid = (g * (1.0 / (1.0 + jnp.exp(-g)))) * u
        y = jnp.dot(hid.astype(bf16), wd_s[...], preferred_element_type=f32)
        _store_slabs(y_ref, _pack_rows(y))

    @pl.when(i >= nu_ref[0])
    def _():
        y_ref[...] = jnp.zeros_like(y_ref)


def _experts(block_e, n_used, xs, w_gate, w_up, w_down):
    n_slots = xs.shape[0]
    bk = MOE_BLOCK
    grid_spec = pltpu.PrefetchScalarGridSpec(
        num_scalar_prefetch=2,
        grid=(n_slots // bk,),
        in_specs=[pl.BlockSpec((ROW_CHUNKS * bk, LANES), lambda i, be, nu: (i, 0)),
                  pl.BlockSpec((1, D_MODEL, D_EXPERT), lambda i, be, nu: (be[i], 0, 0)),
                  pl.BlockSpec((1, D_MODEL, D_EXPERT), lambda i, be, nu: (be[i], 0, 0)),
                  pl.BlockSpec((1, D_EXPERT, D_MODEL), lambda i, be, nu: (be[i], 0, 0))],
        out_specs=pl.BlockSpec((ROW_CHUNKS * bk, LANES), lambda i, be, nu: (i, 0)),
        scratch_shapes=[pltpu.VMEM((D_MODEL, D_EXPERT), bf16),
                        pltpu.VMEM((D_MODEL, D_EXPERT), bf16),
                        pltpu.VMEM((D_EXPERT, D_MODEL), bf16)],
    )
    return pl.pallas_call(
        _experts_kernel,
        grid_spec=grid_spec,
        out_shape=jax.ShapeDtypeStruct((ROW_CHUNKS * n_slots, LANES), i32),
        compiler_params=_cparams("arbitrary"),
        name="moe_experts",
    )(block_e, n_used, xs.reshape(-1, LANES), w_gate, w_up, w_down)


def _combine_kernel(dest_ref, y_ref, x1_ref, wcol_ref, mod_ref, g_ref, o_ref, buf, sems, *, tm):
    i = pl.program_id(0)
    n = pl.num_programs(0)

    def row_copy(tile, slot, r, k):
        d = dest_ref[tile * (2 * tm) + 2 * r + k]
        dst = buf.at[slot, k, pl.ds(pl.multiple_of(ROW_CHUNKS * r, ROW_CHUNKS), ROW_CHUNKS), :]
        return pltpu.make_async_copy(y_ref.at[d], dst, sems.at[slot])

    def issue_tile(tile, slot):
        def body(r, _):
            row_copy(tile, slot, r, 0).start()
            row_copy(tile, slot, r, 1).start()
            return 0
        lax.fori_loop(0, tm, body, 0, unroll=8)

    @pl.when(i == 0)
    def _():
        issue_tile(0, 0)

    @pl.when(i + 1 < n)
    def _():
        issue_tile(i + 1, (i + 1) % 2)

    slot = i % 2

    def drain(r, _):
        row_copy(i, slot, r, 0).wait()
        row_copy(i, slot, r, 1).wait()
        return 0

    lax.fori_loop(0, tm, drain, 0, unroll=8)

    wc = wcol_ref[...]
    w0 = wc[:, 0:1]
    w1 = wc[:, 1:2]
    lo0, hi0 = _unpack_rows(_load_slabs(buf.at[slot, 0], tm))
    lo1, hi1 = _unpack_rows(_load_slabs(buf.at[slot, 1], tm))
    moe = jnp.concatenate([w0 * lo0 + w1 * lo1, w0 * hi0 + w1 * hi1], axis=-1)
    x = x1_ref[...] + mod_ref[0, 5:6, :] * moe
    ms = jnp.mean(x * x, axis=-1, keepdims=True)
    o_ref[...] = x * lax.rsqrt(ms + NORM_EPS) * g_ref[...]


def _combine(dest, y_packed, x1, wcol, mod, g_f, seq):
    t, d = x1.shape
    tm = GATHER_TILE
    tiles_per_batch = seq // tm
    grid_spec = pltpu.PrefetchScalarGridSpec(
        num_scalar_prefetch=1,
        grid=(t // tm,),
        in_specs=[pl.BlockSpec(memory_space=pl.ANY),
                  pl.BlockSpec((tm, d), lambda i, ds: (i, 0)),
                  pl.BlockSpec((tm, LANES), lambda i, ds: (i, 0)),
                  pl.BlockSpec((1, 6, d), lambda i, ds: (i // tiles_per_batch, 0, 0)),
                  pl.BlockSpec((1, d), lambda i, ds: (0, 0))],
        out_specs=pl.BlockSpec((tm, d), lambda i, ds: (i, 0)),
        scratch_shapes=[pltpu.VMEM((2, 2, ROW_CHUNKS * tm, LANES), i32),
                        pltpu.SemaphoreType.DMA((2,))],
    )
    return pl.pallas_call(
        functools.partial(_combine_kernel, tm=tm),
        grid_spec=grid_spec,
        out_shape=jax.ShapeDtypeStruct((t, d), f32),
        compiler_params=_cparams("arbitrary"),
        name="moe_combine",
    )(dest, y_packed.reshape(-1, ROW_CHUNKS, LANES), x1, wcol, mod, g_f)


def _route_tables(route, cnt, tm):
    n_tiles = cnt.shape[0]
    t = route.shape[1]
    bk = MOE_BLOCK
    n_blocks = (2 * t) // bk + N_EXPERTS
    ti = jnp.arange(n_tiles)
    ei = jnp.arange(N_EXPERTS)
    tile_base = jnp.sum(jnp.where((ti[:, None] > ti[None, :])[:, :, None], cnt[None], 0), axis=1)
    total = jnp.sum(cnt, axis=0)
    padded = (total + bk - 1) // bk * bk
    pends = jnp.sum(jnp.where(ei[None, :] <= ei[:, None], padded[None, :], 0), axis=1)
    base = (pends - padded)[None, :] + tile_base
    base_tok = jnp.broadcast_to(base[:, None, :], (n_tiles, tm, N_EXPERTS)).reshape(t, N_EXPERTS)

    def slot_base(e):
        return jnp.sum(jnp.where(ei[None, :] == e[:, None], base_tok, 0), axis=1)

    d0 = slot_base(route[0]) + route[2]
    d1 = slot_base(route[1]) + route[3]
    dest = jnp.stack([d0, d1], axis=1).reshape(-1).astype(i32)
    block_start = jnp.arange(n_blocks, dtype=i32) * bk
    block_e = jnp.clip(jnp.sum(pends[None, :] <= block_start[:, None], axis=1),
                       0, N_EXPERTS - 1).astype(i32)
    n_used = (pends[-1] // bk).astype(i32).reshape(1)
    return dest, block_e, n_used, n_blocks * bk


def _layer(x2, c, seq, l, ada_w, ada_b, norm1_g, w_in, b_f, lam_q1, lam_k1, lam_q2, lam_k2,
           subln_g, w_o, norm2_g, w_rg, b_rg, w_re, b_re, w_gate, w_up, w_down):
    t, d = x2.shape
    bsz = t // seq
    lam_init = 0.8 - 0.6 * math.exp(-0.3 * l)
    mod = _ada(c, ada_w, ada_b).reshape(bsz, 6, d)

    z0 = 3 * FOX_WIDTH
    w_main = jnp.concatenate([w_in[:, :z0], w_in[:, z0 + FOX_HEADS:]], axis=1).astype(bf16)
    w_fz = jnp.pad(w_in[:, z0:z0 + FOX_HEADS], ((0, 0), (0, LANES - FOX_HEADS))).astype(bf16)
    b_fz = jnp.pad(b_f, (0, LANES - FOX_HEADS)).reshape(1, LANES)
    proj, cum = _inproj(x2, mod, norm1_g.reshape(1, d), w_main, w_fz, b_fz, seq)
    proj3 = proj.reshape(bsz, seq, -1)

    fox = _fox_attention(proj3, cum.reshape(bsz, seq, LANES))
    lam_vecs = jnp.pad(jnp.stack([lam_q1, lam_k1, lam_q2, lam_k2]).astype(f32),
                       ((0, 4), (0, LANES - HEAD_DIM)))
    diff = _diff_attention(proj3, lam_vecs, subln_g.reshape(1, LANES), lam_init)

    w_r = jnp.concatenate([w_rg.T, jnp.zeros((8 - N_GROUPS, d), f32), w_re.T], axis=0)
    wr_hi = w_r.astype(bf16)
    wr_lo = (w_r - wr_hi.astype(f32)).astype(bf16)
    b_r = jnp.concatenate([b_rg, jnp.zeros((8 - N_GROUPS,), f32), b_re]).reshape(ROUTER_ROWS, 1)
    x1, h_packed, route, wcol, cnt = _outproj(
        fox.reshape(t, FOX_WIDTH), diff.reshape(t, DIFF_WIDTH), x2, mod, norm2_g.reshape(1, d),
        w_o.astype(bf16), wr_hi, wr_lo, b_r, seq)

    dest, block_e, n_used, n_slots = _route_tables(route, cnt[:, :, 0], ROW_TILE)
    xs = _dispatch(dest, h_packed, n_slots)
    y_packed = _experts(block_e, n_used, xs, w_gate, w_up, w_down)
    return dest, y_packed, x1, wcol, mod


def kernel(x, c, ada_w, ada_b, norm1_g, w_in, b_f, lam_q1, lam_k1, lam_q2, lam_k2, subln_g, w_o,
           norm2_g, w_rg, b_rg, w_re, b_re, w_gate, w_up, w_down, norm_f_g):
    bsz, seq, d = x.shape
    depth = ada_w.shape[0]
    assert depth == 1 and d == D_MODEL and seq % ROW_TILE == 0 and seq % ATT_WIDE == 0
    x2 = x.reshape(bsz * seq, d)
    dest, y_packed, x1, wcol, mod = _layer(
        x2, c, seq, 0, ada_w[0], ada_b[0], norm1_g[0], w_in[0], b_f[0], lam_q1[0], lam_k1[0],
        lam_q2[0], lam_k2[0], subln_g[0], w_o[0], norm2_g[0], w_rg[0], b_rg[0], w_re[0], b_re[0],
        w_gate[0], w_up[0], w_down[0])
    out = _combine(dest, y_packed, x1, wcol, mod, norm_f_g.reshape(1, d), seq)
    return out.reshape(bsz, seq, d)
```

```python
import functools
import math

import jax
import jax.numpy as jnp
from jax import lax
from jax.experimental import pallas as pl
from jax.experimental.pallas import tpu as pltpu

f32 = jnp.float32
bf16 = jnp.bfloat16
i32 = jnp.int32

D_MODEL = 1024
HEAD_DIM = 64
FOX_HEADS = 8
FOX_WIDTH = FOX_HEADS * HEAD_DIM
DIFF_HEADS = 4
DIFF_QK_WIDTH = DIFF_HEADS * 2 * HEAD_DIM
DIFF_WIDTH = DIFF_HEADS * 2 * HEAD_DIM
CHUNK = 64
N_GROUPS = 4
EXPERTS_PER_GROUP = 8
N_EXPERTS = N_GROUPS * EXPERTS_PER_GROUP
D_EXPERT = 512
NORM_EPS = 1e-6
SUBLN_EPS = 1e-5

LANES = 128
LOG2E = 1.4426950408889634
Q_SCALE = HEAD_DIM ** -0.5 * LOG2E
NEG = -1e30
HALF = D_MODEL // 2
ROW_CHUNKS = HALF // LANES

ROW_TILE = 512
ATT_TQ = 1024
ATT_WIDE = 1024
ATT_COLS = 256
MOE_BLOCK = 256
GATHER_TILE = 256
VMEM_LIMIT = 48 * 1024 * 1024


def _cparams(*sem):
    return pltpu.CompilerParams(dimension_semantics=sem, vmem_limit_bytes=VMEM_LIMIT)


def _split3(c):
    hi = c.astype(bf16).astype(f32)
    r = c - hi
    mid = r.astype(bf16).astype(f32)
    lo = r - mid
    return hi, mid, lo


def _pack_rows(y):
    a = pltpu.bitcast(y[:, :HALF].astype(bf16).astype(f32), i32)
    b = pltpu.bitcast(y[:, HALF:].astype(bf16).astype(f32), i32)
    return lax.shift_right_logical(a, 16) | (b & jnp.int32(-65536))


def _store_slabs(ref, packed):
    n = packed.shape[0]
    for j in range(ROW_CHUNKS):
        ref[pl.ds(j, n, stride=ROW_CHUNKS), :] = packed[:, j * LANES:(j + 1) * LANES]


def _load_slabs(ref, n):
    return jnp.concatenate([ref[pl.ds(j, n, stride=ROW_CHUNKS), :] for j in range(ROW_CHUNKS)], axis=-1)


def _unpack_rows(w):
    lo = pltpu.bitcast(lax.shift_left(w, 16), f32)
    hi = pltpu.bitcast(w & jnp.int32(-65536), f32)
    return lo, hi


def _ada_kernel(c_ref, w_ref, b_ref, o_ref):
    c = c_ref[...]
    w = w_ref[...]
    c_hi = c.astype(bf16)
    c_lo = (c - c_hi.astype(f32)).astype(bf16)
    w_hi = w.astype(bf16)
    w_lo = (w - w_hi.astype(f32)).astype(bf16)
    acc = jnp.dot(c_hi, w_hi, preferred_element_type=f32)
    acc += jnp.dot(c_hi, w_lo, preferred_element_type=f32)
    acc += jnp.dot(c_lo, w_hi, preferred_element_type=f32)
    o_ref[...] = acc + b_ref[...]


def _ada(c, w, b):
    bsz, d = c.shape
    n = w.shape[1]
    tn = 1024
    return pl.pallas_call(
        _ada_kernel,
        grid=(n // tn,),
        in_specs=[pl.BlockSpec((bsz, d), lambda j: (0, 0)),
                  pl.BlockSpec((d, tn), lambda j: (0, j)),
                  pl.BlockSpec((1, tn), lambda j: (0, j))],
        out_specs=pl.BlockSpec((bsz, tn), lambda j: (0, j)),
        out_shape=jax.ShapeDtypeStruct((bsz, n), f32),
        compiler_params=_cparams("parallel"),
        name="ada_mod",
    )(c, w, b.reshape(1, n))


def _inproj_kernel(x_ref, mod_ref, g_ref, wm_ref, wz_ref, bf_ref, tril_ref,
                   proj_ref, cum_ref, carry_ref, *, tiles_per_batch):
    i = pl.program_id(0)
    x = x_ref[...]
    ms = jnp.mean(x * x, axis=-1, keepdims=True)
    y = x * lax.rsqrt(ms + NORM_EPS) * g_ref[...]
    h = (y * (1.0 + mod_ref[0, 1:2, :]) + mod_ref[0, 0:1, :]).astype(bf16)
    n_chunks = proj_ref.shape[1] // 512
    for j in range(n_chunks):
        acc = jnp.dot(h, wm_ref[:, j * 512:(j + 1) * 512], preferred_element_type=f32)
        if j in (0, 3):
            acc = acc * Q_SCALE
        proj_ref[:, j * 512:(j + 1) * 512] = acc.astype(bf16)
    fz = jnp.dot(h, wz_ref[...], preferred_element_type=f32) + bf_ref[...]
    ls = (jnp.minimum(fz, 0.0) - jnp.log(1.0 + jnp.exp(-jnp.abs(fz)))) * LOG2E
    hi, mid, lo = _split3(ls)
    tril = tril_ref[...]
    local = jnp.dot(tril, hi.astype(bf16), preferred_element_type=f32)
    local += jnp.dot(tril, mid.astype(bf16), preferred_element_type=f32)
    local += jnp.dot(tril, lo.astype(bf16), preferred_element_type=f32)

    @pl.when(i % tiles_per_batch == 0)
    def _():
        carry_ref[...] = jnp.zeros_like(carry_ref)

    cum = local + carry_ref[0:1, :]
    cum_ref[...] = cum
    tm = x.shape[0]
    carry_ref[0:1, :] = cum[tm - 1:tm, :]


def _inproj(x2, mod, g1, w_main, w_fz, b_fz, seq):
    t, d = x2.shape
    tm = ROW_TILE
    tiles_per_batch = seq // tm
    n_main = w_main.shape[1]
    tril = (jnp.arange(tm)[:, None] >= jnp.arange(tm)[None, :]).astype(bf16)
    return pl.pallas_call(
        functools.partial(_inproj_kernel, tiles_per_batch=tiles_per_batch),
        grid=(t // tm,),
        in_specs=[pl.BlockSpec((tm, d), lambda i: (i, 0)),
                  pl.BlockSpec((1, 6, d), lambda i: (i // tiles_per_batch, 0, 0)),
                  pl.BlockSpec((1, d), lambda i: (0, 0)),
                  pl.BlockSpec((d, n_main), lambda i: (0, 0)),
                  pl.BlockSpec((d, LANES), lambda i: (0, 0)),
                  pl.BlockSpec((1, LANES), lambda i: (0, 0)),
                  pl.BlockSpec((tm, tm), lambda i: (0, 0))],
        out_specs=[pl.BlockSpec((tm, n_main), lambda i: (i, 0)),
                   pl.BlockSpec((tm, LANES), lambda i: (i, 0))],
        out_shape=[jax.ShapeDtypeStruct((t, n_main), bf16),
                   jax.ShapeDtypeStruct((t, LANES), f32)],
        scratch_shapes=[pltpu.VMEM((8, LANES), f32)],
        compiler_params=_cparams("arbitrary"),
        name="inproj",
    )(x2, mod, g1, w_main, w_fz, b_fz, tril)


def _aug(data, lane, low_map, first, last):
    base = 64 if low_map else 0
    out = jnp.zeros_like(data)
    for n, val in enumerate(tuple(first) + tuple(last)):
        out = jnp.where(lane == base + n, val, out)
    keep = (lane < 64) if low_map else (lane >= 64)
    return jnp.where(keep, data, out)


_ONES3 = (1.0, 1.0, 1.0)


def _q_aug(q, lane, low_map, c):
    return _aug(q, lane, low_map, _split3(c), _ONES3).astype(bf16)


def _k_aug(k, lane, low_map, c):
    hi, mid, lo = _split3(c)
    return _aug(k, lane, low_map, _ONES3, (-hi, -mid, -lo)).astype(bf16)


ONES_ROWS = 16


def _vt_aug(vt):
    row = lax.broadcasted_iota(i32, (ONES_ROWS, vt.shape[1]), 0)
    extra = jnp.where(row == 0, 1.0, 0.0).astype(vt.dtype)
    return jnp.concatenate([vt, extra], axis=0).astype(bf16)


def _flash_scratch(tq, acc_rows):
    return ([pltpu.VMEM((ATT_WIDE, tq), f32)] * 2
            + [pltpu.VMEM((acc_rows, tq), f32)] * 2 + [pltpu.VMEM((1, tq), f32)] * 4)


def _flash_pair(q_a, q_b, ka_s, kb_s, vta_s, vtb_s, mask_ref, qi, tq, finish, scratch):
    s_a, s_b, acc_a, acc_b, m_a, m_b, tmax_a, tmax_b = scratch
    nt = (((1,), (1,)), ((), ()))
    wide = ATT_WIDE
    assert tq == wide

    maps = ((q_a, ka_s, vta_s, m_a, acc_a), (q_b, kb_s, vtb_s, m_b, acc_b))
    items = [(x, pl.ds(c * ATT_COLS, ATT_COLS)) for x in range(2) for c in range(tq // ATT_COLS)]

    def scores_to(bufs, item, start):
        x, cols = item
        q, k_s = maps[x][0], maps[x][1]
        rows = pl.ds(pl.multiple_of(start, wide), wide)
        q_cols = q[cols.start:cols.start + cols.size, :]
        s = lax.dot_general(k_s[rows, :], q_cols, nt, preferred_element_type=f32)
        bufs[x][:, cols] = s
        bufs[2 + x][:, cols] = jnp.max(s, axis=0, keepdims=True)

    def update(bufs, item, start, diagonal=False):
        x, cols = item
        _, _, vt_s, m_ref, acc_ref = maps[x]
        if diagonal:
            width = cols.start + cols.size
            s = bufs[x][:width, cols] + mask_ref[:width, cols]
            tile_max = jnp.max(s, axis=0, keepdims=True)
        else:
            width = wide
            s = bufs[x][:, cols]
            tile_max = bufs[2 + x][:, cols]
        vt = vt_s[:, pl.ds(pl.multiple_of(start, wide), width)]
        m = m_ref[:, cols]
        m_new = jnp.maximum(m, tile_max)
        p = jnp.exp2(s - m_new)
        acc_ref[:, cols] = (jnp.exp2(m - m_new) * acc_ref[:, cols]
                            + jnp.dot(vt, p.astype(bf16), preferred_element_type=f32))
        m_ref[:, cols] = m_new

    bufs = (s_a, s_b, tmax_a, tmax_b)
    m_a[...] = jnp.full(m_a.shape, NEG, f32)
    m_b[...] = jnp.full(m_b.shape, NEG, f32)
    acc_a[...] = jnp.zeros(acc_a.shape, f32)
    acc_b[...] = jnp.zeros(acc_b.shape, f32)
    for item in items:
        scores_to(bufs, item, 0)

    def sweep(j, _):
        for item in items:
            update(bufs, item, j * wide)
            scores_to(bufs, item, (j + 1) * wide)
        return 0

    lax.fori_loop(0, qi, sweep, 0)
    for item in items:
        update(bufs, item, qi * wide, diagonal=True)
    finish(acc_a[...], acc_b[...])


def _fox_kernel(q_ref, k_ref, v_ref, cum_ref, mask_ref, o_ref, ka_s, kb_s, vta_s, vtb_s,
                *flash_scratch, seq, tq):
    hp = pl.program_id(1)
    qi = pl.program_id(2)
    lane = lax.broadcasted_iota(i32, (1, LANES), 1)

    def head_cums(cm):
        c_a = jnp.sum(jnp.where(lane == 2 * hp, cm, 0.0), axis=-1, keepdims=True)
        c_b = jnp.sum(jnp.where(lane == 2 * hp + 1, cm, 0.0), axis=-1, keepdims=True)
        return c_a, c_b

    @pl.when(qi == 0)
    def _():
        for c in range(seq // 512):
            rows = pl.ds(c * 512, 512)
            kk = k_ref[0, rows, :].astype(f32)
            c_a, c_b = head_cums(cum_ref[0, rows, :])
            ka_s[rows, :] = _k_aug(kk, lane, True, c_a)
            kb_s[rows, :] = _k_aug(kk, lane, False, c_b)
            vt = v_ref[0, rows, :].astype(f32).T
            vta_s[:, rows] = _vt_aug(vt[:HEAD_DIM])
            vtb_s[:, rows] = _vt_aug(vt[HEAD_DIM:])

    q = q_ref[0].astype(f32)
    c_a, c_b = head_cums(cum_ref[0, pl.ds(pl.multiple_of(qi * tq, tq), tq), :])
    q_a = _q_aug(q, lane, True, c_a)
    q_b = _q_aug(q, lane, False, c_b)

    def finish(acc_a, acc_b):
        o_a = acc_a[:HEAD_DIM] / acc_a[HEAD_DIM:HEAD_DIM + 1]
        o_b = acc_b[:HEAD_DIM] / acc_b[HEAD_DIM:HEAD_DIM + 1]
        o_ref[0] = jnp.concatenate([o_a, o_b], axis=0).T.astype(bf16)

    _flash_pair(q_a, q_b, ka_s, kb_s, vta_s, vtb_s, mask_ref, qi, tq, finish, flash_scratch)


def _tail_mask(diag):
    diag_t = jnp.swapaxes(diag, -1, -2)
    pad = [(0, 0)] * (diag.ndim - 2) + [(ATT_WIDE - diag.shape[-1], 0), (0, 0)]
    return jnp.pad(diag_t, pad)


def _fox_attention(proj3, cum3):
    bsz, seq, _ = proj3.shape
    tq, tk = ATT_TQ, ATT_WIDE
    n_pairs = FOX_HEADS // 2
    r = jnp.arange(tq)
    mask = _tail_mask(jnp.where(r[:, None] >= r[None, :], 0.0, NEG).astype(f32))
    return pl.pallas_call(
        functools.partial(_fox_kernel, seq=seq, tq=tq),
        grid=(bsz, n_pairs, seq // tq),
        in_specs=[pl.BlockSpec((1, tq, LANES), lambda b, h, q: (b, q, h)),
                  pl.BlockSpec((1, seq, LANES), lambda b, h, q: (b, 0, 4 + h)),
                  pl.BlockSpec((1, seq, LANES), lambda b, h, q: (b, 0, 8 + h)),
                  pl.BlockSpec((1, seq, LANES), lambda b, h, q: (b, 0, 0)),
                  pl.BlockSpec((tk, tq), lambda b, h, q: (0, 0))],
        out_specs=pl.BlockSpec((1, tq, LANES), lambda b, h, q: (b, q, h)),
        out_shape=jax.ShapeDtypeStruct((bsz, seq, FOX_WIDTH), bf16),
        scratch_shapes=[pltpu.VMEM((seq, LANES), bf16), pltpu.VMEM((seq, LANES), bf16),
                        pltpu.VMEM((HEAD_DIM + ONES_ROWS, seq), bf16),
                        pltpu.VMEM((HEAD_DIM + ONES_ROWS, seq), bf16)]
        + _flash_scratch(tq, HEAD_DIM + ONES_ROWS),
        compiler_params=_cparams("parallel", "parallel", "arbitrary"),
        name="fox_attn",
    )(proj3, proj3, proj3, cum3, mask)


def _diff_kernel(slope_ref, q_ref, k_ref, v_ref, mask_ref, lamv_ref, g_ref, o_ref,
                 ka_s, kb_s, vt_s, *flash_scratch, seq, tq, lam_init):
    h = pl.program_id(1)
    qi = pl.program_id(2)
    lane = lax.broadcasted_iota(i32, (1, LANES), 1)
    slope = slope_ref[h]

    def pos_bias(start, n):
        pos = (start + lax.broadcasted_iota(i32, (n, 1), 0)).astype(f32)
        return -(slope * pos)

    @pl.when(qi == 0)
    def _():
        for c in range(seq // 512):
            rows = pl.ds(c * 512, 512)
            kk = k_ref[0, rows, :].astype(f32)
            cb = pos_bias(c * 512, 512)
            ka_s[rows, :] = _k_aug(kk, lane, True, cb)
            kb_s[rows, :] = _k_aug(kk, lane, False, cb)
            vt_s[:, rows] = _vt_aug(v_ref[0, rows, :].astype(f32).T)

    q = q_ref[0].astype(f32)
    cq = pos_bias(qi * tq, tq)
    q_a = _q_aug(q, lane, True, cq)
    q_b = _q_aug(q, lane, False, cq)

    def finish(acc_a, acc_b):
        lv = lamv_ref[...]
        s1 = jnp.sum(lv[0:1, :] * lv[1:2, :], axis=-1, keepdims=True)
        s2 = jnp.sum(lv[2:3, :] * lv[3:4, :], axis=-1, keepdims=True)
        lam = jnp.exp(s1) - jnp.exp(s2) + lam_init
        o_a = acc_a[:LANES] / acc_a[LANES:LANES + 1]
        o_b = acc_b[:LANES] / acc_b[LANES:LANES + 1]
        d = (o_a - lam * o_b).T
        y = d * lax.rsqrt(jnp.mean(d * d, axis=-1, keepdims=True) + SUBLN_EPS) * g_ref[...]
        o_ref[0] = (y * (1.0 - lam_init)).astype(bf16)

    _flash_pair(q_a, q_b, ka_s, kb_s, vt_s, vt_s, mask_ref.at[0], qi, tq, finish, flash_scratch)


def _diff_attention(proj3, lam_vecs, subln_g, lam_init):
    bsz, seq, _ = proj3.shape
    tq, tk = ATT_TQ, ATT_WIDE
    slopes = jnp.asarray([2.0 ** (-8.0 * (i + 1) / DIFF_HEADS) for i in range(DIFF_HEADS)], f32) * LOG2E
    r = jnp.arange(tq)
    tq_i, tk_i = r[:, None], r[None, :]
    chunk_ok = (tq_i // CHUNK) >= (tk_i // CHUNK)
    ahead = jnp.maximum(tk_i - tq_i, 0).astype(f32)
    mask = _tail_mask(jnp.where(chunk_ok[None], -2.0 * slopes[:, None, None] * ahead[None], NEG).astype(f32))
    grid_spec = pltpu.PrefetchScalarGridSpec(
        num_scalar_prefetch=1,
        grid=(bsz, DIFF_HEADS, seq // tq),
        in_specs=[pl.BlockSpec((1, tq, LANES), lambda b, h, q, s: (b, q, 12 + h)),
                  pl.BlockSpec((1, seq, LANES), lambda b, h, q, s: (b, 0, 16 + h)),
                  pl.BlockSpec((1, seq, LANES), lambda b, h, q, s: (b, 0, 20 + h)),
                  pl.BlockSpec((1, tk, tq), lambda b, h, q, s: (h, 0, 0)),
                  pl.BlockSpec((8, LANES), lambda b, h, q, s: (0, 0)),
                  pl.BlockSpec((1, LANES), lambda b, h, q, s: (0, 0))],
        out_specs=pl.BlockSpec((1, tq, LANES), lambda b, h, q, s: (b, q, h)),
        scratch_shapes=[pltpu.VMEM((seq, LANES), bf16), pltpu.VMEM((seq, LANES), bf16),
                        pltpu.VMEM((LANES + ONES_ROWS, seq), bf16)]
        + _flash_scratch(tq, LANES + ONES_ROWS),
    )
    return pl.pallas_call(
        functools.partial(_diff_kernel, seq=seq, tq=tq, lam_init=lam_init),
        grid_spec=grid_spec,
        out_shape=jax.ShapeDtypeStruct((bsz, seq, DIFF_WIDTH), bf16),
        compiler_params=_cparams("parallel", "parallel", "arbitrary"),
        name="diff_attn",
    )(slopes, proj3, proj3, proj3, mask, lam_vecs, subln_g)


ROUTER_ROWS = 8 + N_EXPERTS


def _outproj_kernel(fox_ref, diff_ref, x_ref, mod_ref, g_ref, wo_ref, wr_hi_ref, wr_lo_ref,
                    br_ref, triu_ref, x1_ref, hp_ref, route_ref, wcol_ref, cnt_ref):
    tm = x_ref.shape[0]
    y = jnp.dot(fox_ref[...], wo_ref[0:FOX_WIDTH, :], preferred_element_type=f32)
    y += jnp.dot(diff_ref[...], wo_ref[FOX_WIDTH:, :], preferred_element_type=f32)
    x1 = x_ref[...] + mod_ref[0, 2:3, :] * y
    x1_ref[...] = x1
    ms = jnp.mean(x1 * x1, axis=-1, keepdims=True)
    h2 = (x1 * lax.rsqrt(ms + NORM_EPS) * g_ref[...]) * (1.0 + mod_ref[0, 4:5, :]) + mod_ref[0, 3:4, :]
    _store_slabs(hp_ref, _pack_rows(h2))

    nt = (((1,), (1,)), ((), ()))
    h_hi = h2.astype(bf16)
    h_lo = (h2 - h_hi.astype(f32)).astype(bf16)
    logits = lax.dot_general(wr_hi_ref[...], h_hi, nt, preferred_element_type=f32)
    logits += lax.dot_general(wr_lo_ref[...], h_hi, nt, preferred_element_type=f32)
    logits += lax.dot_general(wr_hi_ref[...], h_lo, nt, preferred_element_type=f32)
    logits = logits + br_ref[...]

    row8 = lax.broadcasted_iota(i32, (8, tm), 0)
    gl = jnp.where(row8 < N_GROUPS, logits[0:8, :], NEG)
    gmax = jnp.max(gl, axis=0, keepdims=True)
    grp = jnp.min(jnp.where(gl == gmax, row8, 8), axis=0, keepdims=True)
    p_g = 1.0 / jnp.sum(jnp.exp(gl - gmax), axis=0, keepdims=True)
    sel = logits[8:16, :]
    for g in range(1, N_GROUPS):
        sel = jnp.where(grp == g, logits[8 + 8 * g:16 + 8 * g, :], sel)
    v1 = jnp.max(sel, axis=0, keepdims=True)
    i1 = jnp.min(jnp.where(sel == v1, row8, 8), axis=0, keepdims=True)
    sel2 = jnp.where(row8 == i1, -jnp.inf, sel)
    v2 = jnp.max(sel2, axis=0, keepdims=True)
    i2 = jnp.min(jnp.where(sel2 == v2, row8, 8), axis=0, keepdims=True)
    e21 = jnp.exp(v2 - v1)
    w1 = p_g / (1.0 + e21)
    w2 = w1 * e21
    e1 = grp * EXPERTS_PER_GROUP + i1
    e2 = grp * EXPERTS_PER_GROUP + i2

    row32 = lax.broadcasted_iota(i32, (N_EXPERTS, tm), 0)
    oh1 = row32 == e1
    oh2 = row32 == e2
    both = jnp.where(oh1 | oh2, 1.0, 0.0)
    prefix = jnp.dot(both.astype(bf16), triu_ref[...], preferred_element_type=f32)
    r1 = jnp.sum(jnp.where(oh1, prefix, 0.0), axis=0, keepdims=True).astype(i32)
    r2 = jnp.sum(jnp.where(oh2, prefix, 0.0), axis=0, keepdims=True).astype(i32)
    route = jnp.where(row8 == 0, e1, jnp.where(row8 == 1, e2,
                      jnp.where(row8 == 2, r1, jnp.where(row8 == 3, r2, 0))))
    route_ref[...] = route
    cnt = jnp.sum(both, axis=1, keepdims=True)
    cnt_ref[0] = jnp.broadcast_to(cnt, (N_EXPERTS, LANES)).astype(i32)
    row128 = lax.broadcasted_iota(i32, (LANES, tm), 0)
    w_rows = jnp.where(row128 == 0, w1, jnp.where(row128 == 1, w2, 0.0))
    wcol_ref[...] = w_rows.T


def _outproj(fox2, diff2, x2, mod, g2, w_o, wr_hi, wr_lo, b_r, seq):
    t, d = x2.shape
    tm = ROW_TILE
    tiles_per_batch = seq // tm
    n_tiles = t // tm
    triu = (jnp.arange(tm)[:, None] < jnp.arange(tm)[None, :]).astype(bf16)
    row = lambda i: (i, 0)
    const = lambda i: (0, 0)
    return pl.pallas_call(
        _outproj_kernel,
        grid=(n_tiles,),
        in_specs=[pl.BlockSpec((tm, FOX_WIDTH), row),
                  pl.BlockSpec((tm, DIFF_WIDTH), row),
                  pl.BlockSpec((tm, d), row),
                  pl.BlockSpec((1, 6, d), lambda i: (i // tiles_per_batch, 0, 0)),
                  pl.BlockSpec((1, d), const),
                  pl.BlockSpec((d, d), const),
                  pl.BlockSpec((ROUTER_ROWS, d), const),
                  pl.BlockSpec((ROUTER_ROWS, d), const),
                  pl.BlockSpec((ROUTER_ROWS, 1), const),
                  pl.BlockSpec((tm, tm), const)],
        out_specs=[pl.BlockSpec((tm, d), row),
                   pl.BlockSpec((ROW_CHUNKS * tm, LANES), row),
                   pl.BlockSpec((8, tm), lambda i: (0, i)),
                   pl.BlockSpec((tm, LANES), row),
                   pl.BlockSpec((1, N_EXPERTS, LANES), lambda i: (i, 0, 0))],
        out_shape=[jax.ShapeDtypeStruct((t, d), f32),
                   jax.ShapeDtypeStruct((ROW_CHUNKS * t, LANES), i32),
                   jax.ShapeDtypeStruct((8, t), i32),
                   jax.ShapeDtypeStruct((t, LANES), f32),
                   jax.ShapeDtypeStruct((n_tiles, N_EXPERTS, LANES), i32)],
        compiler_params=_cparams("parallel"),
        name="outproj_router",
    )(fox2, diff2, x2, mod, g2, w_o, wr_hi, wr_lo, b_r, triu)


def _dispatch_kernel(dest_ref, h_ref, xs_in_ref, xs_ref, sem, *, tm):
    del xs_in_ref
    i = pl.program_id(0)
    base = i * (2 * tm)

    def row_copy(r, k):
        d = dest_ref[base + 2 * r + k]
        src = h_ref.at[pl.ds(pl.multiple_of(ROW_CHUNKS * r, ROW_CHUNKS), ROW_CHUNKS), :]
        return pltpu.make_async_copy(src, xs_ref.at[d], sem)

    def issue(r, _):
        row_copy(r, 0).start()
        row_copy(r, 1).start()
        return 0

    lax.fori_loop(0, tm, issue, 0, unroll=8)

    def drain(r, _):
        row_copy(r, 0).wait()
        row_copy(r, 1).wait()
        return 0

    lax.fori_loop(0, tm, drain, 0, unroll=8)


def _dispatch(dest, h_packed, n_slots):
    t = h_packed.shape[0] // ROW_CHUNKS
    tm = GATHER_TILE
    xs0 = jnp.zeros((n_slots, ROW_CHUNKS, LANES), i32)
    grid_spec = pltpu.PrefetchScalarGridSpec(
        num_scalar_prefetch=1,
        grid=(t // tm,),
        in_specs=[pl.BlockSpec((ROW_CHUNKS * tm, LANES), lambda i, d: (i, 0)),
                  pl.BlockSpec(memory_space=pl.ANY)],
        out_specs=pl.BlockSpec(memory_space=pl.ANY),
        scratch_shapes=[pltpu.SemaphoreType.DMA(())],
    )
    return pl.pallas_call(
        functools.partial(_dispatch_kernel, tm=tm),
        grid_spec=grid_spec,
        out_shape=jax.ShapeDtypeStruct((n_slots, ROW_CHUNKS, LANES), i32),
        input_output_aliases={2: 0},
        compiler_params=_cparams("arbitrary"),
        name="moe_dispatch",
    )(dest, h_packed, xs0)


def _experts_kernel(be_ref, nu_ref, xs_ref, wg_ref, wu_ref, wd_ref, y_ref, wg_s, wu_s, wd_s):
    i = pl.program_id(0)
    prev = be_ref[jnp.maximum(i - 1, 0)]
    fresh = jnp.logical_or(i == 0, be_ref[i] != prev)

    @pl.when(jnp.logical_and(fresh, i < nu_ref[0]))
    def _():
        wg_s[...] = wg_ref[0].astype(bf16)
        wu_s[...] = wu_ref[0].astype(bf16)
        wd_s[...] = wd_ref[0].astype(bf16)

    @pl.when(i < nu_ref[0])
    def _():
        lo, hi = _unpack_rows(_load_slabs(xs_ref, MOE_BLOCK))
        xb = jnp.concatenate([lo.astype(bf16), hi.astype(bf16)], axis=-1)
        g = jnp.dot(xb, wg_s[...], preferred_element_type=f32)
        u = jnp.dot(xb, wu_s[...], preferred_element_type=f32)
        hid = (g * (1.0 / (1.0 + jnp.exp(-g)))) * u
        y = jnp.dot(hid.astype(bf16), wd_s[...], preferred_element_type=f32)
        _store_slabs(y_ref, _pack_rows(y))

    @pl.when(i >= nu_ref[0])
    def _():
        y_ref[...] = jnp.zeros_like(y_ref)


def _experts(block_e, n_used, xs, w_gate, w_up, w_down):
    n_slots = xs.shape[0]
    bk = MOE_BLOCK
    grid_spec = pltpu.PrefetchScalarGridSpec(
        num_scalar_prefetch=2,
        grid=(n_slots // bk,),
        in_specs=[pl.BlockSpec((ROW_CHUNKS * bk, LANES), lambda i, be, nu: (i, 0)),
                  pl.BlockSpec((1, D_MODEL, D_EXPERT), lambda i, be, nu: (be[i], 0, 0)),
                  pl.BlockSpec((1, D_MODEL, D_EXPERT), lambda i, be, nu: (be[i], 0, 0)),
                  pl.BlockSpec((1, D_EXPERT, D_MODEL), lambda i, be, nu: (be[i], 0, 0))],
        out_specs=pl.BlockSpec((ROW_CHUNKS * bk, LANES), lambda i, be, nu: (i, 0)),
        scratch_shapes=[pltpu.VMEM((D_MODEL, D_EXPERT), bf16),
                        pltpu.VMEM((D_MODEL, D_EXPERT), bf16),
                        pltpu.VMEM((D_EXPERT, D_MODEL), bf16)],
    )
    return pl.pallas_call(
        _experts_kernel,
        grid_spec=grid_spec,
        out_shape=jax.ShapeDtypeStruct((ROW_CHUNKS * n_slots, LANES), i32),
        compiler_params=_cparams("arbitrary"),
        name="moe_experts",
    )(block_e, n_used, xs.reshape(-1, LANES), w_gate, w_up, w_down)


def _combine_kernel(dest_ref, y_ref, x1_ref, wcol_ref, mod_ref, g_ref, o_ref, buf, sems, *, tm):
    i = pl.program_id(0)
    n = pl.num_programs(0)

    def row_copy(tile, slot, r, k):
        d = dest_ref[tile * (2 * tm) + 2 * r + k]
        dst = buf.at[slot, k, pl.ds(pl.multiple_of(ROW_CHUNKS * r, ROW_CHUNKS), ROW_CHUNKS), :]
        return pltpu.make_async_copy(y_ref.at[d], dst, sems.at[slot])

    def issue_tile(tile, slot):
        def body(r, _):
            row_copy(tile, slot, r, 0).start()
            row_copy(tile, slot, r, 1).start()
            return 0
        lax.fori_loop(0, tm, body, 0, unroll=8)

    @pl.when(i == 0)
    def _():
        issue_tile(0, 0)

    @pl.when(i + 1 < n)
    def _():
        issue_tile(i + 1, (i + 1) % 2)

    slot = i % 2

    def drain(r, _):
        row_copy(i, slot, r, 0).wait()
        row_copy(i, slot, r, 1).wait()
        return 0

    lax.fori_loop(0, tm, drain, 0, unroll=8)

    wc = wcol_ref[...]
    w0 = wc[:, 0:1]
    w1 = wc[:, 1:2]
    lo0, hi0 = _unpack_rows(_load_slabs(buf.at[slot, 0], tm))
    lo1, hi1 = _unpack_rows(_load_slabs(buf.at[slot, 1], tm))
    moe = jnp.concatenate([w0 * lo0 + w1 * lo1, w0 * hi0 + w1 * hi1], axis=-1)
    x = x1_ref[...] + mod_ref[0, 5:6, :] * moe
    ms = jnp.mean(x * x, axis=-1, keepdims=True)
    o_ref[...] = x * lax.rsqrt(ms + NORM_EPS) * g_ref[...]


def _combine(dest, y_packed, x1, wcol, mod, g_f, seq):
    t, d = x1.shape
    tm = GATHER_TILE
    tiles_per_batch = seq // tm
    grid_spec = pltpu.PrefetchScalarGridSpec(
        num_scalar_prefetch=1,
        grid=(t // tm,),
        in_specs=[pl.BlockSpec(memory_space=pl.ANY),
                  pl.BlockSpec((tm, d), lambda i, ds: (i, 0)),
                  pl.BlockSpec((tm, LANES), lambda i, ds: (i, 0)),
                  pl.BlockSpec((1, 6, d), lambda i, ds: (i // tiles_per_batch, 0, 0)),
                  pl.BlockSpec((1, d), lambda i, ds: (0, 0))],
        out_specs=pl.BlockSpec((tm, d), lambda i, ds: (i, 0)),
        scratch_shapes=[pltpu.VMEM((2, 2, ROW_CHUNKS * tm, LANES), i32),
                        pltpu.SemaphoreType.DMA((2,))],
    )
    return pl.pallas_call(
        functools.partial(_combine_kernel, tm=tm),
        grid_spec=grid_spec,
        out_shape=jax.ShapeDtypeStruct((t, d), f32),
        compiler_params=_cparams("arbitrary"),
        name="moe_combine",
    )(dest, y_packed.reshape(-1, ROW_CHUNKS, LANES), x1, wcol, mod, g_f)


def _route_tables(route, cnt, tm):
    n_tiles = cnt.shape[0]
    t = route.shape[1]
    bk = MOE_BLOCK
    n_blocks = (2 * t) // bk + N_EXPERTS
    ti = jnp.arange(n_tiles)
    ei = jnp.arange(N_EXPERTS)
    tile_base = jnp.sum(jnp.where((ti[:, None] > ti[None, :])[:, :, None], cnt[None], 0), axis=1)
    total = jnp.sum(cnt, axis=0)
    padded = (total + bk - 1) // bk * bk
    pends = jnp.sum(jnp.where(ei[None, :] <= ei[:, None], padded[None, :], 0), axis=1)
    base = (pends - padded)[None, :] + tile_base
    base_tok = jnp.broadcast_to(base[:, None, :], (n_tiles, tm, N_EXPERTS)).reshape(t, N_EXPERTS)

    def slot_base(e):
        return jnp.sum(jnp.where(ei[None, :] == e[:, None], base_tok, 0), axis=1)

    d0 = slot_base(route[0]) + route[2]
    d1 = slot_base(route[1]) + route[3]
    dest = jnp.stack([d0, d1], axis=1).reshape(-1).astype(i32)
    block_start = jnp.arange(n_blocks, dtype=i32) * bk
    block_e = jnp.clip(jnp.sum(pends[None, :] <= block_start[:, None], axis=1),
                       0, N_EXPERTS - 1).astype(i32)
    n_used = (pends[-1] // bk).astype(i32).reshape(1)
    return dest, block_e, n_used, n_blocks * bk


def _layer(x2, c, seq, l, ada_w, ada_b, norm1_g, w_in, b_f, lam_q1, lam_k1, lam_q2, lam_k2,
           subln_g, w_o, norm2_g, w_rg, b_rg, w_re, b_re, w_gate, w_up, w_down):
    t, d = x2.shape
    bsz = t // seq
    lam_init = 0.8 - 0.6 * math.exp(-0.3 * l)
    mod = _ada(c, ada_w, ada_b).reshape(bsz, 6, d)

    z0 = 3 * FOX_WIDTH
    w_main = jnp.concatenate([w_in[:, :z0], w_in[:, z0 + FOX_HEADS:]], axis=1).astype(bf16)
    w_fz = jnp.pad(w_in[:, z0:z0 + FOX_HEADS], ((0, 0), (0, LANES - FOX_HEADS))).astype(bf16)
    b_fz = jnp.pad(b_f, (0, LANES - FOX_HEADS)).reshape(1, LANES)
    proj, cum = _inproj(x2, mod, norm1_g.reshape(1, d), w_main, w_fz, b_fz, seq)
    proj3 = proj.reshape(bsz, seq, -1)

    fox = _fox_attention(proj3, cum.reshape(bsz, seq, LANES))
    lam_vecs = jnp.pad(jnp.stack([lam_q1, lam_k1, lam_q2, lam_k2]).astype(f32),
                       ((0, 4), (0, LANES - HEAD_DIM)))
    diff = _diff_attention(proj3, lam_vecs, subln_g.reshape(1, LANES), lam_init)

    w_r = jnp.concatenate([w_rg.T, jnp.zeros((8 - N_GROUPS, d), f32), w_re.T], axis=0)
    wr_hi = w_r.astype(bf16)
    wr_lo = (w_r - wr_hi.astype(f32)).astype(bf16)
    b_r = jnp.concatenate([b_rg, jnp.zeros((8 - N_GROUPS,), f32), b_re]).reshape(ROUTER_ROWS, 1)
    x1, h_packed, route, wcol, cnt = _outproj(
        fox.reshape(t, FOX_WIDTH), diff.reshape(t, DIFF_WIDTH), x2, mod, norm2_g.reshape(1, d),
        w_o.astype(bf16), wr_hi, wr_lo, b_r, seq)

    dest, block_e, n_used, n_slots = _route_tables(route, cnt[:, :, 0], ROW_TILE)
    xs = _dispatch(dest, h_packed, n_slots)
    y_packed = _experts(block_e, n_used, xs, w_gate, w_up, w_down)
    return dest, y_packed, x1, wcol, mod


def kernel(x, c, ada_w, ada_b, norm1_g, w_in, b_f, lam_q1, lam_k1, lam_q2, lam_k2, subln_g, w_o,
           norm2_g, w_rg, b_rg, w_re, b_re, w_gate, w_up, w_down, norm_f_g):
    bsz, seq, d = x.shape
    depth = ada_w.shape[0]
    assert depth == 1 and d == D_MODEL and seq % ROW_TILE == 0 and seq % ATT_WIDE == 0
    x2 = x.reshape(bsz * seq, d)
    dest, y_packed, x1, wcol, mod = _layer(
        x2, c, seq, 0, ada_w[0], ada_b[0], norm1_g[0], w_in[0], b_f[0], lam_q1[0], lam_k1[0],
        lam_q2[0], lam_k2[0], subln_g[0], w_o[0], norm2_g[0], w_rg[0], b_rg[0], w_re[0], b_re[0],
        w_gate[0], w_up[0], w_down[0])
    out = _combine(dest, y_packed, x1, wcol, mod, norm_f_g.reshape(1, d), seq)
    return out.reshape(bsz, seq, d)
```

```python
import functools
import math

import jax
import jax.numpy as jnp
from jax import lax
from jax.experimental import pallas as pl
from jax.experimental.pallas import tpu as pltpu

f32 = jnp.float32
bf16 = jnp.bfloat16
i32 = jnp.int32

D_MODEL = 1024
HEAD_DIM = 64
FOX_HEADS = 8
FOX_WIDTH = FOX_HEADS * HEAD_DIM
DIFF_HEADS = 4
DIFF_QK_WIDTH = DIFF_HEADS * 2 * HEAD_DIM
DIFF_WIDTH = DIFF_HEADS * 2 * HEAD_DIM
CHUNK = 64
N_GROUPS = 4
EXPERTS_PER_GROUP = 8
N_EXPERTS = N_GROUPS * EXPERTS_PER_GROUP
D_EXPERT = 512
NORM_EPS = 1e-6
SUBLN_EPS = 1e-5

LANES = 128
LOG2E = 1.4426950408889634
Q_SCALE = HEAD_DIM ** -0.5 * LOG2E
NEG = -1e30
HALF = D_MODEL // 2
ROW_CHUNKS = HALF // LANES

ROW_TILE = 512
ATT_TQ = 1024
ATT_WIDE = 1024
ATT_COLS = 256
MOE_BLOCK = 256
GATHER_TILE = 256
VMEM_LIMIT = 48 * 1024 * 1024


def _cparams(*sem):
    return pltpu.CompilerParams(dimension_semantics=sem, vmem_limit_bytes=VMEM_LIMIT)


def _split3(c):
    hi = c.astype(bf16).astype(f32)
    r = c - hi
    mid = r.astype(bf16).astype(f32)
    lo = r - mid
    return hi, mid, lo


def _pack_rows(y):
    a = pltpu.bitcast(y[:, :HALF].astype(bf16).astype(f32), i32)
    b = pltpu.bitcast(y[:, HALF:].astype(bf16).astype(f32), i32)
    return lax.shift_right_logical(a, 16) | (b & jnp.int32(-65536))


def _store_slabs(ref, packed):
    n = packed.shape[0]
    for j in range(ROW_CHUNKS):
        ref[pl.ds(j, n, stride=ROW_CHUNKS), :] = packed[:, j * LANES:(j + 1) * LANES]


def _load_slabs(ref, n):
    return jnp.concatenate([ref[pl.ds(j, n, stride=ROW_CHUNKS), :] for j in range(ROW_CHUNKS)], axis=-1)


def _unpack_rows(w):
    lo = pltpu.bitcast(lax.shift_left(w, 16), f32)
    hi = pltpu.bitcast(w & jnp.int32(-65536), f32)
    return lo, hi


def _ada_kernel(c_ref, w_ref, b_ref, o_ref):
    c = c_ref[...]
    w = w_ref[...]
    c_hi = c.astype(bf16)
    c_lo = (c - c_hi.astype(f32)).astype(bf16)
    w_hi = w.astype(bf16)
    w_lo = (w - w_hi.astype(f32)).astype(bf16)
    acc = jnp.dot(c_hi, w_hi, preferred_element_type=f32)
    acc += jnp.dot(c_hi, w_lo, preferred_element_type=f32)
    acc += jnp.dot(c_lo, w_hi, preferred_element_type=f32)
    o_ref[...] = acc + b_ref[...]


def _ada(c, w, b):
    bsz, d = c.shape
    n = w.shape[1]
    tn = 1024
    return pl.pallas_call(
        _ada_kernel,
        grid=(n // tn,),
        in_specs=[pl.BlockSpec((bsz, d), lambda j: (0, 0)),
                  pl.BlockSpec((d, tn), lambda j: (0, j)),
                  pl.BlockSpec((1, tn), lambda j: (0, j))],
        out_specs=pl.BlockSpec((bsz, tn), lambda j: (0, j)),
        out_shape=jax.ShapeDtypeStruct((bsz, n), f32),
        compiler_params=_cparams("parallel"),
        name="ada_mod",
    )(c, w, b.reshape(1, n))


def _inproj_kernel(x_ref, mod_ref, g_ref, wm_ref, wz_ref, bf_ref, tril_ref,
                   proj_ref, cum_ref, carry_ref, *, tiles_per_batch):
    i = pl.program_id(0)
    x = x_ref[...]
    ms = jnp.mean(x * x, axis=-1, keepdims=True)
    y = x * lax.rsqrt(ms + NORM_EPS) * g_ref[...]
    h = (y * (1.0 + mod_ref[0, 1:2, :]) + mod_ref[0, 0:1, :]).astype(bf16)
    n_chunks = proj_ref.shape[1] // 512
    for j in range(n_chunks):
        acc = jnp.dot(h, wm_ref[:, j * 512:(j + 1) * 512], preferred_element_type=f32)
        if j in (0, 3):
            acc = acc * Q_SCALE
        proj_ref[:, j * 512:(j + 1) * 512] = acc.astype(bf16)
    fz = jnp.dot(h, wz_ref[...], preferred_element_type=f32) + bf_ref[...]
    ls = (jnp.minimum(fz, 0.0) - jnp.log(1.0 + jnp.exp(-jnp.abs(fz)))) * LOG2E
    hi, mid, lo = _split3(ls)
    tril = tril_ref[...]
    local = jnp.dot(tril, hi.astype(bf16), preferred_element_type=f32)
    local += jnp.dot(tril, mid.astype(bf16), preferred_element_type=f32)
    local += jnp.dot(tril, lo.astype(bf16), preferred_element_type=f32)

    @pl.when(i % tiles_per_batch == 0)
    def _():
        carry_ref[...] = jnp.zeros_like(carry_ref)

    cum = local + carry_ref[0:1, :]
    cum_ref[...] = cum
    tm = x.shape[0]
    carry_ref[0:1, :] = cum[tm - 1:tm, :]


def _inproj(x2, mod, g1, w_main, w_fz, b_fz, seq):
    t, d = x2.shape
    tm = ROW_TILE
    tiles_per_batch = seq // tm
    n_main = w_main.shape[1]
    tril = (jnp.arange(tm)[:, None] >= jnp.arange(tm)[None, :]).astype(bf16)
    return pl.pallas_call(
        functools.partial(_inproj_kernel, tiles_per_batch=tiles_per_batch),
        grid=(t // tm,),
        in_specs=[pl.BlockSpec((tm, d), lambda i: (i, 0)),
                  pl.BlockSpec((1, 6, d), lambda i: (i // tiles_per_batch, 0, 0)),
                  pl.BlockSpec((1, d), lambda i: (0, 0)),
                  pl.BlockSpec((d, n_main), lambda i: (0, 0)),
                  pl.BlockSpec((d, LANES), lambda i: (0, 0)),
                  pl.BlockSpec((1, LANES), lambda i: (0, 0)),
                  pl.BlockSpec((tm, tm), lambda i: (0, 0))],
        out_specs=[pl.BlockSpec((tm, n_main), lambda i: (i, 0)),
                   pl.BlockSpec((tm, LANES), lambda i: (i, 0))],
        out_shape=[jax.ShapeDtypeStruct((t, n_main), bf16),
                   jax.ShapeDtypeStruct((t, LANES), f32)],
        scratch_shapes=[pltpu.VMEM((8, LANES), f32)],
        compiler_params=_cparams("arbitrary"),
        name="inproj",
    )(x2, mod, g1, w_main, w_fz, b_fz, tril)


def _aug(data, lane, low_map, first, last):
    base = 64 if low_map else 0
    out = jnp.zeros_like(data)
    for n, val in enumerate(tuple(first) + tuple(last)):
        out = jnp.where(lane == base + n, val, out)
    keep = (lane < 64) if low_map else (lane >= 64)
    return jnp.where(keep, data, out)


_ONES3 = (1.0, 1.0, 1.0)


def _q_aug(q, lane, low_map, c):
    return _aug(q, lane, low_map, _split3(c), _ONES3).astype(bf16)


def _k_aug(k, lane, low_map, c):
    hi, mid, lo = _split3(c)
    return _aug(k, lane, low_map, _ONES3, (-hi, -mid, -lo)).astype(bf16)


ONES_ROWS = 16


def _vt_aug(vt):
    row = lax.broadcasted_iota(i32, (ONES_ROWS, vt.shape[1]), 0)
    extra = jnp.where(row == 0, 1.0, 0.0).astype(vt.dtype)
    return jnp.concatenate([vt, extra], axis=0).astype(bf16)


def _flash_scratch(tq, acc_rows):
    return ([pltpu.VMEM((ATT_WIDE, tq), f32)] * 2
            + [pltpu.VMEM((acc_rows, tq), f32)] * 2 + [pltpu.VMEM((1, tq), f32)] * 4)


def _flash_pair(q_a, q_b, ka_s, kb_s, vta_s, vtb_s, mask_ref, qi, tq, finish, scratch):
    s_a, s_b, acc_a, acc_b, m_a, m_b, tmax_a, tmax_b = scratch
    nt = (((1,), (1,)), ((), ()))
    wide = ATT_WIDE
    assert tq == wide

    maps = ((q_a, ka_s, vta_s, m_a, acc_a), (q_b, kb_s, vtb_s, m_b, acc_b))
    items = [(x, pl.ds(c * ATT_COLS, ATT_COLS)) for x in range(2) for c in range(tq // ATT_COLS)]

    def scores_to(bufs, item, start):
        x, cols = item
        q, k_s = maps[x][0], maps[x][1]
        rows = pl.ds(pl.multiple_of(start, wide), wide)
        q_cols = q[cols.start:cols.start + cols.size, :]
        s = lax.dot_general(k_s[rows, :], q_cols, nt, preferred_element_type=f32)
        bufs[x][:, cols] = s
        bufs[2 + x][:, cols] = jnp.max(s, axis=0, keepdims=True)

    def update(bufs, item, start, diagonal=False):
        x, cols = item
        _, _, vt_s, m_ref, acc_ref = maps[x]
        if diagonal:
            width = cols.start + cols.size
            s = bufs[x][:width, cols] + mask_ref[:width, cols]
            tile_max = jnp.max(s, axis=0, keepdims=True)
        else:
            width = wide
            s = bufs[x][:, cols]
            tile_max = bufs[2 + x][:, cols]
        vt = vt_s[:, pl.ds(pl.multiple_of(start, wide), width)]
        m = m_ref[:, cols]
        m_new = jnp.maximum(m, tile_max)
        p = jnp.exp2(s - m_new)
        acc_ref[:, cols] = (jnp.exp2(m - m_new) * acc_ref[:, cols]
                            + jnp.dot(vt, p.astype(bf16), preferred_element_type=f32))
        m_ref[:, cols] = m_new

    bufs = (s_a, s_b, tmax_a, tmax_b)
    m_a[...] = jnp.full(m_a.shape, NEG, f32)
    m_b[...] = jnp.full(m_b.shape, NEG, f32)
    acc_a[...] = jnp.zeros(acc_a.shape, f32)
    acc_b[...] = jnp.zeros(acc_b.shape, f32)
    for item in items:
        scores_to(bufs, item, 0)

    def sweep(j, _):
        for item in items:
            update(bufs, item, j * wide)
            scores_to(bufs, item, (j + 1) * wide)
        return 0

    lax.fori_loop(0, qi, sweep, 0)
    for item in items:
        update(bufs, item, qi * wide, diagonal=True)
    finish(acc_a[...], acc_b[...])


def _fox_kernel(q_ref, k_ref, v_ref, cum_ref, mask_ref, o_ref, ka_s, kb_s, vta_s, vtb_s,
                *flash_scratch, seq, tq):
    hp = pl.program_id(1)
    qi = pl.program_id(2)
    lane = lax.broadcasted_iota(i32, (1, LANES), 1)

    def head_cums(cm):
        c_a = jnp.sum(jnp.where(lane == 2 * hp, cm, 0.0), axis=-1, keepdims=True)
        c_b = jnp.sum(jnp.where(lane == 2 * hp + 1, cm, 0.0), axis=-1, keepdims=True)
        return c_a, c_b

    @pl.when(qi == 0)
    def _():
        for c in range(seq // 512):
            rows = pl.ds(c * 512, 512)
            kk = k_ref[0, rows, :].astype(f32)
            c_a, c_b = head_cums(cum_ref[0, rows, :])
            ka_s[rows, :] = _k_aug(kk, lane, True, c_a)
            kb_s[rows, :] = _k_aug(kk, lane, False, c_b)
            vt = v_ref[0, rows, :].astype(f32).T
            vta_s[:, rows] = _vt_aug(vt[:HEAD_DIM])
            vtb_s[:, rows] = _vt_aug(vt[HEAD_DIM:])

    q = q_ref[0].astype(f32)
    c_a, c_b = head_cums(cum_ref[0, pl.ds(pl.multiple_of(qi * tq, tq), tq), :])
    q_a = _q_aug(q, lane, True, c_a)
    q_b = _q_aug(q, lane, False, c_b)

    def finish(acc_a, acc_b):
        o_a = acc_a[:HEAD_DIM] / acc_a[HEAD_DIM:HEAD_DIM + 1]
        o_b = acc_b[:HEAD_DIM] / acc_b[HEAD_DIM:HEAD_DIM + 1]
        o_ref[0] = jnp.concatenate([o_a, o_b], axis=0).T.astype(bf16)

    _flash_pair(q_a, q_b, ka_s, kb_s, vta_s, vtb_s, mask_ref, qi, tq, finish, flash_scratch)


def _tail_mask(diag):
    diag_t = jnp.swapaxes(diag, -1, -2)
    pad = [(0, 0)] * (diag.ndim - 2) + [(ATT_WIDE - diag.shape[-1], 0), (0, 0)]
    return jnp.pad(diag_t, pad)


def _fox_attention(proj3, cum3):
    bsz, seq, _ = proj3.shape
    tq, tk = ATT_TQ, ATT_WIDE
    n_pairs = FOX_HEADS // 2
    r = jnp.arange(tq)
    mask = _tail_mask(jnp.where(r[:, None] >= r[None, :], 0.0, NEG).astype(f32))
    return pl.pallas_call(
        functools.partial(_fox_kernel, seq=seq, tq=tq),
        grid=(bsz, n_pairs, seq // tq),
        in_specs=[pl.BlockSpec((1, tq, LANES), lambda b, h, q: (b, q, h)),
                  pl.BlockSpec((1, seq, LANES), lambda b, h, q: (b, 0, 4 + h)),
                  pl.BlockSpec((1, seq, LANES), lambda b, h, q: (b, 0, 8 + h)),
                  pl.BlockSpec((1, seq, LANES), lambda b, h, q: (b, 0, 0)),
                  pl.BlockSpec((tk, tq), lambda b, h, q: (0, 0))],
        out_specs=pl.BlockSpec((1, tq, LANES), lambda b, h, q: (b, q, h)),
        out_shape=jax.ShapeDtypeStruct((bsz, seq, FOX_WIDTH), bf16),
        scratch_shapes=[pltpu.VMEM((seq, LANES), bf16), pltpu.VMEM((seq, LANES), bf16),
                        pltpu.VMEM((HEAD_DIM + ONES_ROWS, seq), bf16),
                        pltpu.VMEM((HEAD_DIM + ONES_ROWS, seq), bf16)]
        + _flash_scratch(tq, HEAD_DIM + ONES_ROWS),
        compiler_params=_cparams("parallel", "parallel", "arbitrary"),
        name="fox_attn",
    )(proj3, proj3, proj3, cum3, mask)


def _diff_kernel(slope_ref, q_ref, k_ref, v_ref, mask_ref, lamv_ref, g_ref, o_ref,
                 ka_s, kb_s, vt_s, *flash_scratch, seq, tq, lam_init):
    h = pl.program_id(1)
    qi = pl.program_id(2)
    lane = lax.broadcasted_iota(i32, (1, LANES), 1)
    slope = slope_ref[h]

    def pos_bias(start, n):
        pos = (start + lax.broadcasted_iota(i32, (n, 1), 0)).astype(f32)
        return -(slope * pos)

    @pl.when(qi == 0)
    def _():
        for c in range(seq // 512):
            rows = pl.ds(c * 512, 512)
            kk = k_ref[0, rows, :].astype(f32)
            cb = pos_bias(c * 512, 512)
            ka_s[rows, :] = _k_aug(kk, lane, True, cb)
            kb_s[rows, :] = _k_aug(kk, lane, False, cb)
            vt_s[:, rows] = _vt_aug(v_ref[0, rows, :].astype(f32).T)

    q = q_ref[0].astype(f32)
    cq = pos_bias(qi * tq, tq)
    q_a = _q_aug(q, lane, True, cq)
    q_b = _q_aug(q, lane, False, cq)

    def finish(acc_a, acc_b):
        lv = lamv_ref[...]
        s1 = jnp.sum(lv[0:1, :] * lv[1:2, :], axis=-1, keepdims=True)
        s2 = jnp.sum(lv[2:3, :] * lv[3:4, :], axis=-1, keepdims=True)
        lam = jnp.exp(s1) - jnp.exp(s2) + lam_init
        o_a = acc_a[:LANES] / acc_a[LANES:LANES + 1]
        o_b = acc_b[:LANES] / acc_b[LANES:LANES + 1]
        d = (o_a - lam * o_b).T
        y = d * lax.rsqrt(jnp.mean(d * d, axis=-1, keepdims=True) + SUBLN_EPS) * g_ref[...]
        o_ref[0] = (y * (1.0 - lam_init)).astype(bf16)

    _flash_pair(q_a, q_b, ka_s, kb_s, vt_s, vt_s, mask_ref.at[0], qi, tq, finish, flash_scratch)


def _diff_attention(proj3, lam_vecs, subln_g, lam_init):
    bsz, seq, _ = proj3.shape
    tq, tk = ATT_TQ, ATT_WIDE
    slopes = jnp.asarray([2.0 ** (-8.0 * (i + 1) / DIFF_HEADS) for i in range(DIFF_HEADS)], f32) * LOG2E
    r = jnp.arange(tq)
    tq_i, tk_i = r[:, None], r[None, :]
    chunk_ok = (tq_i // CHUNK) >= (tk_i // CHUNK)
    ahead = jnp.maximum(tk_i - tq_i, 0).astype(f32)
    mask = _tail_mask(jnp.where(chunk_ok[None], -2.0 * slopes[:, None, None] * ahead[None], NEG).astype(f32))
    grid_spec = pltpu.PrefetchScalarGridSpec(
        num_scalar_prefetch=1,
        grid=(bsz, DIFF_HEADS, seq // tq),
        in_specs=[pl.BlockSpec((1, tq, LANES), lambda b, h, q, s: (b, q, 12 + h)),
                  pl.BlockSpec((1, seq, LANES), lambda b, h, q, s: (b, 0, 16 + h)),
                  pl.BlockSpec((1, seq, LANES), lambda b, h, q, s: (b, 0, 20 + h)),
                  pl.BlockSpec((1, tk, tq), lambda b, h, q, s: (h, 0, 0)),
                  pl.BlockSpec((8, LANES), lambda b, h, q, s: (0, 0)),
                  pl.BlockSpec((1, LANES), lambda b, h, q, s: (0, 0))],
        out_specs=pl.BlockSpec((1, tq, LANES), lambda b, h, q, s: (b, q, h)),
        scratch_shapes=[pltpu.VMEM((seq, LANES), bf16), pltpu.VMEM((seq, LANES), bf16),
                        pltpu.VMEM((LANES + ONES_ROWS, seq), bf16)]
        + _flash_scratch(tq, LANES + ONES_ROWS),
    )
    return pl.pallas_call(
        functools.partial(_diff_kernel, seq=seq, tq=tq, lam_init=lam_init),
        grid_spec=grid_spec,
        out_shape=jax.ShapeDtypeStruct((bsz, seq, DIFF_WIDTH), bf16),
        compiler_params=_cparams("parallel", "parallel", "arbitrary"),
        name="diff_attn",
    )(slopes, proj3, proj3, proj3, mask, lam_vecs, subln_g)


ROUTER_ROWS = 8 + N_EXPERTS


def _outproj_kernel(fox_ref, diff_ref, x_ref, mod_ref, g_ref, wo_ref, wr_hi_ref, wr_lo_ref,
                    br_ref, triu_ref, x1_ref, hp_ref, route_ref, wcol_ref, cnt_ref):
    tm = x_ref.shape[0]
    y = jnp.dot(fox_ref[...], wo_ref[0:FOX_WIDTH, :], preferred_element_type=f32)
    y += jnp.dot(diff_ref[...], wo_ref[FOX_WIDTH:, :], preferred_element_type=f32)
    x1 = x_ref[...] + mod_ref[0, 2:3, :] * y
    x1_ref[...] = x1
    ms = jnp.mean(x1 * x1, axis=-1, keepdims=True)
    h2 = (x1 * lax.rsqrt(ms + NORM_EPS) * g_ref[...]) * (1.0 + mod_ref[0, 4:5, :]) + mod_ref[0, 3:4, :]
    _store_slabs(hp_ref, _pack_rows(h2))

    nt = (((1,), (1,)), ((), ()))
    h_hi = h2.astype(bf16)
    h_lo = (h2 - h_hi.astype(f32)).astype(bf16)
    logits = lax.dot_general(wr_hi_ref[...], h_hi, nt, preferred_element_type=f32)
    logits += lax.dot_general(wr_lo_ref[...], h_hi, nt, preferred_element_type=f32)
    logits += lax.dot_general(wr_hi_ref[...], h_lo, nt, preferred_element_type=f32)
    logits = logits + br_ref[...]

    row8 = lax.broadcasted_iota(i32, (8, tm), 0)
    gl = jnp.where(row8 < N_GROUPS, logits[0:8, :], NEG)
    gmax = jnp.max(gl, axis=0, keepdims=True)
    grp = jnp.min(jnp.where(gl == gmax, row8, 8), axis=0, keepdims=True)
    p_g = 1.0 / jnp.sum(jnp.exp(gl - gmax), axis=0, keepdims=True)
    sel = logits[8:16, :]
    for g in range(1, N_GROUPS):
        sel = jnp.where(grp == g, logits[8 + 8 * g:16 + 8 * g, :], sel)
    v1 = jnp.max(sel, axis=0, keepdims=True)
    i1 = jnp.min(jnp.where(sel == v1, row8, 8), axis=0, keepdims=True)
    sel2 = jnp.where(row8 == i1, -jnp.inf, sel)
    v2 = jnp.max(sel2, axis=0, keepdims=True)
    i2 = jnp.min(jnp.where(sel2 == v2, row8, 8), axis=0, keepdims=True)
    e21 = jnp.exp(v2 - v1)
    w1 = p_g / (1.0 + e21)
    w2 = w1 * e21
    e1 = grp * EXPERTS_PER_GROUP + i1
    e2 = grp * EXPERTS_PER_GROUP + i2

    row32 = lax.broadcasted_iota(i32, (N_EXPERTS, tm), 0)
    oh1 = row32 == e1
    oh2 = row32 == e2
    both = jnp.where(oh1 | oh2, 1.0, 0.0)
    prefix = jnp.dot(both.astype(bf16), triu_ref[...], preferred_element_type=f32)
    r1 = jnp.sum(jnp.where(oh1, prefix, 0.0), axis=0, keepdims=True).astype(i32)
    r2 = jnp.sum(jnp.where(oh2, prefix, 0.0), axis=0, keepdims=True).astype(i32)
    route = jnp.where(row8 == 0, e1, jnp.where(row8 == 1, e2,
                      jnp.where(row8 == 2, r1, jnp.where(row8 == 3, r2, 0))))
    route_ref[...] = route
    cnt = jnp.sum(both, axis=1, keepdims=True)
    cnt_ref[0] = jnp.broadcast_to(cnt, (N_EXPERTS, LANES)).astype(i32)
    row128 = lax.broadcasted_iota(i32, (LANES, tm), 0)
    w_rows = jnp.where(row128 == 0, w1, jnp.where(row128 == 1, w2, 0.0))
    wcol_ref[...] = w_rows.T


def _outproj(fox2, diff2, x2, mod, g2, w_o, wr_hi, wr_lo, b_r, seq):
    t, d = x2.shape
    tm = ROW_TILE
    tiles_per_batch = seq // tm
    n_tiles = t // tm
    triu = (jnp.arange(tm)[:, None] < jnp.arange(tm)[None, :]).astype(bf16)
    row = lambda i: (i, 0)
    const = lambda i: (0, 0)
    return pl.pallas_call(
        _outproj_kernel,
        grid=(n_tiles,),
        in_specs=[pl.BlockSpec((tm, FOX_WIDTH), row),
                  pl.BlockSpec((tm, DIFF_WIDTH), row),
                  pl.BlockSpec((tm, d), row),
                  pl.BlockSpec((1, 6, d), lambda i: (i // tiles_per_batch, 0, 0)),
                  pl.BlockSpec((1, d), const),
                  pl.BlockSpec((d, d), const),
                  pl.BlockSpec((ROUTER_ROWS, d), const),
                  pl.BlockSpec((ROUTER_ROWS, d), const),
                  pl.BlockSpec((ROUTER_ROWS, 1), const),
                  pl.BlockSpec((tm, tm), const)],
        out_specs=[pl.BlockSpec((tm, d), row),
                   pl.BlockSpec((ROW_CHUNKS * tm, LANES), row),
                   pl.BlockSpec((8, tm), lambda i: (0, i)),
                   pl.BlockSpec((tm, LANES), row),
                   pl.BlockSpec((1, N_EXPERTS, LANES), lambda i: (i, 0, 0))],
        out_shape=[jax.ShapeDtypeStruct((t, d), f32),
                   jax.ShapeDtypeStruct((ROW_CHUNKS * t, LANES), i32),
                   jax.ShapeDtypeStruct((8, t), i32),
                   jax.ShapeDtypeStruct((t, LANES), f32),
                   jax.ShapeDtypeStruct((n_tiles, N_EXPERTS, LANES), i32)],
        compiler_params=_cparams("parallel"),
        name="outproj_router",
    )(fox2, diff2, x2, mod, g2, w_o, wr_hi, wr_lo, b_r, triu)


def _slot_table_kernel(dest_ref, pad_lo_ref, pad_hi_ref, slot_ref, *, n_assign, pad_value):
    def fill(s, _):
        slot_ref[s] = pad_value
        return 0

    for g in range(pad_lo_ref.shape[0]):
        lax.fori_loop(pad_lo_ref[g], pad_hi_ref[g], fill, 0)

    def place(a, _):
        slot_ref[dest_ref[a]] = a
        return 0

    lax.fori_loop(0, n_assign, place, 0, unroll=8)


def _slot_table(dest, pad_lo, pad_hi, n_slots, pad_value):
    n_assign = dest.shape[0]
    smem = pl.BlockSpec(memory_space=pltpu.SMEM)
    return pl.pallas_call(
        functools.partial(_slot_table_kernel, n_assign=n_assign, pad_value=pad_value),
        in_specs=[smem, smem, smem],
        out_specs=smem,
        out_shape=jax.ShapeDtypeStruct((n_slots,), i32),
        name="moe_slot_table",
    )(dest, pad_lo, pad_hi)


def _experts_kernel(be_ref, nu_ref, slot_ref, hp_ref, wg_ref, wu_ref, wd_ref, y_ref,
                    wg_s, wu_s, wd_s, xbuf, gsem, *, n_tok, n_blocks):
    bk = MOE_BLOCK
    i = pl.program_id(0)
    cur = i % 2
    nxt = jnp.minimum(i + 1, n_blocks - 1)
    prev = be_ref[jnp.maximum(i - 1, 0)]
    fresh = jnp.logical_or(i == 0, be_ref[i] != prev)
    live = i < nu_ref[0]

    def row_gather(blk, buf, r):
        tok = jnp.minimum(lax.shift_right_logical(slot_ref[blk * bk + r], 1), n_tok - 1)
        dst = xbuf.at[buf, pl.ds(ROW_CHUNKS * r, ROW_CHUNKS), :]
        return pltpu.make_async_copy(hp_ref.at[tok], dst, gsem.at[buf])

    def start_gather(blk, buf):
        for r in range(bk):
            row_gather(blk, buf, r).start()

    def wait_gather(blk, buf):
        for r in range(bk):
            row_gather(blk, buf, r).wait()

    @pl.when(i == 0)
    def _():
        start_gather(0, 0)

    @pl.when(jnp.logical_and(fresh, live))
    def _():
        wg_s[...] = wg_ref[0].astype(bf16)
        wu_s[...] = wu_ref[0].astype(bf16)
        wd_s[...] = wd_ref[0].astype(bf16)

    wait_gather(i, cur)

    @pl.when(live)
    def _():
        lo, hi = _unpack_rows(_load_slabs(xbuf.at[cur], bk))
        xb = jnp.concatenate([lo.astype(bf16), hi.astype(bf16)], axis=-1)
        start_gather(nxt, 1 - cur)
        g = jnp.dot(xb, wg_s[...], preferred_element_type=f32)
        u = jnp.dot(xb, wu_s[...], preferred_element_type=f32)
        hid = (g * (1.0 / (1.0 + jnp.exp(-g)))) * u
        y = jnp.dot(hid.astype(bf16), wd_s[...], preferred_element_type=f32)
        _store_slabs(y_ref, _pack_rows(y))

    @pl.when(jnp.logical_not(live))
    def _():
        start_gather(nxt, 1 - cur)
        y_ref[...] = jnp.zeros_like(y_ref)

    @pl.when(i == n_blocks - 1)
    def _():
        wait_gather(nxt, 1 - cur)


def _experts(block_e, n_used, slots, h_packed, w_gate, w_up, w_down):
    bk = MOE_BLOCK
    n_blocks = slots.shape[0] // bk
    n_tok = h_packed.shape[0] // ROW_CHUNKS
    grid_spec = pltpu.PrefetchScalarGridSpec(
        num_scalar_prefetch=3,
        grid=(n_blocks,),
        in_specs=[pl.BlockSpec(memory_space=pl.ANY),
                  pl.BlockSpec((1, D_MODEL, D_EXPERT), lambda i, be, nu, sl: (be[i], 0, 0)),
                  pl.BlockSpec((1, D_MODEL, D_EXPERT), lambda i, be, nu, sl: (be[i], 0, 0)),
                  pl.BlockSpec((1, D_EXPERT, D_MODEL), lambda i, be, nu, sl: (be[i], 0, 0))],
        out_specs=pl.BlockSpec((ROW_CHUNKS * bk, LANES), lambda i, be, nu, sl: (i, 0)),
        scratch_shapes=[pltpu.VMEM((D_MODEL, D_EXPERT), bf16),
                        pltpu.VMEM((D_MODEL, D_EXPERT), bf16),
                        pltpu.VMEM((D_EXPERT, D_MODEL), bf16),
                        pltpu.VMEM((2, ROW_CHUNKS * bk, LANES), i32),
                        pltpu.SemaphoreType.DMA((2,))],
    )
    return pl.pallas_call(
        functools.partial(_experts_kernel, n_tok=n_tok, n_blocks=n_blocks),
        grid_spec=grid_spec,
        out_shape=jax.ShapeDtypeStruct((ROW_CHUNKS * n_blocks * bk, LANES), i32),
        compiler_params=_cparams("arbitrary"),
        name="moe_experts",
    )(block_e, n_used, slots, h_packed.reshape(n_tok, ROW_CHUNKS, LANES), w_gate, w_up, w_down)


def _combine_kernel(dest_ref, y_ref, x1_ref, wcol_ref, mod_ref, g_ref, o_ref, buf, sems, *, tm):
    i = pl.program_id(0)
    n = pl.num_programs(0)

    def row_copy(tile, slot, r, k):
        d = dest_ref[tile * (2 * tm) + 2 * r + k]
        dst = buf.at[slot, k, pl.ds(pl.multiple_of(ROW_CHUNKS * r, ROW_CHUNKS), ROW_CHUNKS), :]
        return pltpu.make_async_copy(y_ref.at[d], dst, sems.at[slot])

    def issue_tile(tile, slot):
        def body(r, _):
            row_copy(tile, slot, r, 0).start()
            row_copy(tile, slot, r, 1).start()
            return 0
        lax.fori_loop(0, tm, body, 0, unroll=8)

    @pl.when(i == 0)
    def _():
        issue_tile(0, 0)

    @pl.when(i + 1 < n)
    def _():
        issue_tile(i + 1, (i + 1) % 2)

    slot = i % 2

    def drain(r, _):
        row_copy(i, slot, r, 0).wait()
        row_copy(i, slot, r, 1).wait()
        return 0

    lax.fori_loop(0, tm, drain, 0, unroll=8)

    wc = wcol_ref[...]
    w0 = wc[:, 0:1]
    w1 = wc[:, 1:2]
    lo0, hi0 = _unpack_rows(_load_slabs(buf.at[slot, 0], tm))
    lo1, hi1 = _unpack_rows(_load_slabs(buf.at[slot, 1], tm))
    moe = jnp.concatenate([w0 * lo0 + w1 * lo1, w0 * hi0 + w1 * hi1], axis=-1)
    x = x1_ref[...] + mod_ref[0, 5:6, :] * moe
    ms = jnp.mean(x * x, axis=-1, keepdims=True)
    o_ref[...] = x * lax.rsqrt(ms + NORM_EPS) * g_ref[...]


def _combine(dest, y_packed, x1, wcol, mod, g_f, seq):
    t, d = x1.shape
    tm = GATHER_TILE
    tiles_per_batch = seq // tm
    grid_spec = pltpu.PrefetchScalarGridSpec(
        num_scalar_prefetch=1,
        grid=(t // tm,),
        in_specs=[pl.BlockSpec(memory_space=pl.ANY),
                  pl.BlockSpec((tm, d), lambda i, ds: (i, 0)),
                  pl.BlockSpec((tm, LANES), lambda i, ds: (i, 0)),
                  pl.BlockSpec((1, 6, d), lambda i, ds: (i // tiles_per_batch, 0, 0)),
                  pl.BlockSpec((1, d), lambda i, ds: (0, 0))],
        out_specs=pl.BlockSpec((tm, d), lambda i, ds: (i, 0)),
        scratch_shapes=[pltpu.VMEM((2, 2, ROW_CHUNKS * tm, LANES), i32),
                        pltpu.SemaphoreType.DMA((2,))],
    )
    return pl.pallas_call(
        functools.partial(_combine_kernel, tm=tm),
        grid_spec=grid_spec,
        out_shape=jax.ShapeDtypeStruct((t, d), f32),
        compiler_params=_cparams("arbitrary"),
        name="moe_combine",
    )(dest, y_packed.reshape(-1, ROW_CHUNKS, LANES), x1, wcol, mod, g_f)


def _route_tables(route, cnt, tm):
    n_tiles = cnt.shape[0]
    t = route.shape[1]
    bk = MOE_BLOCK
    n_blocks = (2 * t) // bk + N_EXPERTS
    ti = jnp.arange(n_tiles)
    ei = jnp.arange(N_EXPERTS)
    tile_base = jnp.sum(jnp.where((ti[:, None] > ti[None, :])[:, :, None], cnt[None], 0), axis=1)
    total = jnp.sum(cnt, axis=0)
    padded = (total + bk - 1) // bk * bk
    pends = jnp.sum(jnp.where(ei[None, :] <= ei[:, None], padded[None, :], 0), axis=1)
    base = (pends - padded)[None, :] + tile_base
    base_tok = jnp.broadcast_to(base[:, None, :], (n_tiles, tm, N_EXPERTS)).reshape(t, N_EXPERTS)

    def slot_base(e):
        return jnp.sum(jnp.where(ei[None, :] == e[:, None], base_tok, 0), axis=1)

    d0 = slot_base(route[0]) + route[2]
    d1 = slot_base(route[1]) + route[3]
    dest = jnp.stack([d0, d1], axis=1).reshape(-1).astype(i32)
    block_start = jnp.arange(n_blocks, dtype=i32) * bk
    block_e = jnp.clip(jnp.sum(pends[None, :] <= block_start[:, None], axis=1),
                       0, N_EXPERTS - 1).astype(i32)
    n_used = (pends[-1] // bk).astype(i32).reshape(1)
    n_slots = n_blocks * bk
    pad_lo = jnp.concatenate([pends - padded + total, pends[-1:]]).astype(i32)
    pad_hi = jnp.concatenate([pends, jnp.full((1,), n_slots)]).astype(i32)
    return dest, block_e, n_used, n_slots, pad_lo, pad_hi


def _layer(x2, c, seq, l, ada_w, ada_b, norm1_g, w_in, b_f, lam_q1, lam_k1, lam_q2, lam_k2,
           subln_g, w_o, norm2_g, w_rg, b_rg, w_re, b_re, w_gate, w_up, w_down):
    t, d = x2.shape
    bsz = t // seq
    lam_init = 0.8 - 0.6 * math.exp(-0.3 * l)
    mod = _ada(c, ada_w, ada_b).reshape(bsz, 6, d)

    z0 = 3 * FOX_WIDTH
    w_main = jnp.concatenate([w_in[:, :z0], w_in[:, z0 + FOX_HEADS:]], axis=1).astype(bf16)
    w_fz = jnp.pad(w_in[:, z0:z0 + FOX_HEADS], ((0, 0), (0, LANES - FOX_HEADS))).astype(bf16)
    b_fz = jnp.pad(b_f, (0, LANES - FOX_HEADS)).reshape(1, LANES)
    proj, cum = _inproj(x2, mod, norm1_g.reshape(1, d), w_main, w_fz, b_fz, seq)
    proj3 = proj.reshape(bsz, seq, -1)

    fox = _fox_attention(proj3, cum.reshape(bsz, seq, LANES))
    lam_vecs = jnp.pad(jnp.stack([lam_q1, lam_k1, lam_q2, lam_k2]).astype(f32),
                       ((0, 4), (0, LANES - HEAD_DIM)))
    diff = _diff_attention(proj3, lam_vecs, subln_g.reshape(1, LANES), lam_init)

    w_r = jnp.concatenate([w_rg.T, jnp.zeros((8 - N_GROUPS, d), f32), w_re.T], axis=0)
    wr_hi = w_r.astype(bf16)
    wr_lo = (w_r - wr_hi.astype(f32)).astype(bf16)
    b_r = jnp.concatenate([b_rg, jnp.zeros((8 - N_GROUPS,), f32), b_re]).reshape(ROUTER_ROWS, 1)
    x1, h_packed, route, wcol, cnt = _outproj(
        fox.reshape(t, FOX_WIDTH), diff.reshape(t, DIFF_WIDTH), x2, mod, norm2_g.reshape(1, d),
        w_o.astype(bf16), wr_hi, wr_lo, b_r, seq)

    dest, block_e, n_used, n_slots, pad_lo, pad_hi = _route_tables(route, cnt[:, :, 0], ROW_TILE)
    slots = _slot_table(dest, pad_lo, pad_hi, n_slots, 2 * t)
    y_packed = _experts(block_e, n_used, slots, h_packed, w_gate, w_up, w_down)
    return dest, y_packed, x1, wcol, mod


def kernel(x, c, ada_w, ada_b, norm1_g, w_in, b_f, lam_q1, lam_k1, lam_q2, lam_k2, subln_g, w_o,
           norm2_g, w_rg, b_rg, w_re, b_re, w_gate, w_up, w_down, norm_f_g):
    bsz, seq, d = x.shape
    depth = ada_w.shape[0]
    assert depth == 1 and d == D_MODEL and seq % ROW_TILE == 0 and seq % ATT_WIDE == 0
    x2 = x.reshape(bsz * seq, d)
    dest, y_packed, x1, wcol, mod = _layer(
        x2, c, seq, 0, ada_w[0], ada_b[0], norm1_g[0], w_in[0], b_f[0], lam_q1[0], lam_k1[0],
        lam_q2[0], lam_k2[0], subln_g[0], w_o[0], norm2_g[0], w_rg[0], b_rg[0], w_re[0], b_re[0],
        w_gate[0], w_up[0], w_down[0])
    out = _combine(dest, y_packed, x1, wcol, mod, norm_f_g.reshape(1, d), seq)
    return out.reshape(bsz, seq, d)
```

```python
import functools
import math

import jax
import jax.numpy as jnp
from jax import lax
from jax.experimental import pallas as pl
from jax.experimental.pallas import tpu as pltpu

f32 = jnp.float32
bf16 = jnp.bfloat16
i32 = jnp.int32

D_MODEL = 1024
HEAD_DIM = 64
FOX_HEADS = 8
FOX_WIDTH = FOX_HEADS * HEAD_DIM
DIFF_HEADS = 4
DIFF_QK_WIDTH = DIFF_HEADS * 2 * HEAD_DIM
DIFF_WIDTH = DIFF_HEADS * 2 * HEAD_DIM
CHUNK = 64
N_GROUPS = 4
EXPERTS_PER_GROUP = 8
N_EXPERTS = N_GROUPS * EXPERTS_PER_GROUP
D_EXPERT = 512
NORM_EPS = 1e-6
SUBLN_EPS = 1e-5

LANES = 128
LOG2E = 1.4426950408889634
Q_SCALE = HEAD_DIM ** -0.5 * LOG2E
NEG = -1e30
HALF = D_MODEL // 2
ROW_CHUNKS = HALF // LANES

ROW_TILE = 512
ATT_TQ = 1024
ATT_WIDE = 1024
ATT_COLS = 256
ATT_KEYS = 256
MOE_BLOCK = 256
GATHER_TILE = 256
VMEM_LIMIT = 48 * 1024 * 1024


def _cparams(*sem):
    return pltpu.CompilerParams(dimension_semantics=sem, vmem_limit_bytes=VMEM_LIMIT)


def _split3(c):
    hi = c.astype(bf16).astype(f32)
    r = c - hi
    mid = r.astype(bf16).astype(f32)
    lo = r - mid
    return hi, mid, lo


def _pack_rows(y):
    a = pltpu.bitcast(y[:, :HALF].astype(bf16).astype(f32), i32)
    b = pltpu.bitcast(y[:, HALF:].astype(bf16).astype(f32), i32)
    return lax.shift_right_logical(a, 16) | (b & jnp.int32(-65536))


def _store_slabs(ref, packed):
    n = packed.shape[0]
    for j in range(ROW_CHUNKS):
        ref[pl.ds(j, n, stride=ROW_CHUNKS), :] = packed[:, j * LANES:(j + 1) * LANES]


def _load_slabs(ref, n):
    return jnp.concatenate([ref[pl.ds(j, n, stride=ROW_CHUNKS), :] for j in range(ROW_CHUNKS)], axis=-1)


def _unpack_rows(w):
    lo = pltpu.bitcast(lax.shift_left(w, 16), f32)
    hi = pltpu.bitcast(w & jnp.int32(-65536), f32)
    return lo, hi


def _ada_kernel(c_ref, w_ref, b_ref, o_ref):
    c = c_ref[...]
    w = w_ref[...]
    c_hi = c.astype(bf16)
    c_lo = (c - c_hi.astype(f32)).astype(bf16)
    w_hi = w.astype(bf16)
    w_lo = (w - w_hi.astype(f32)).astype(bf16)
    acc = jnp.dot(c_hi, w_hi, preferred_element_type=f32)
    acc += jnp.dot(c_hi, w_lo, preferred_element_type=f32)
    acc += jnp.dot(c_lo, w_hi, preferred_element_type=f32)
    o_ref[...] = acc + b_ref[...]


def _ada(c, w, b):
    bsz, d = c.shape
    n = w.shape[1]
    tn = 1024
    return pl.pallas_call(
        _ada_kernel,
        grid=(n // tn,),
        in_specs=[pl.BlockSpec((bsz, d), lambda j: (0, 0)),
                  pl.BlockSpec((d, tn), lambda j: (0, j)),
                  pl.BlockSpec((1, tn), lambda j: (0, j))],
        out_specs=pl.BlockSpec((bsz, tn), lambda j: (0, j)),
        out_shape=jax.ShapeDtypeStruct((bsz, n), f32),
        compiler_params=_cparams("parallel"),
        name="ada_mod",
    )(c, w, b.reshape(1, n))


def _inproj_kernel(x_ref, mod_ref, g_ref, wm_ref, wz_ref, bf_ref, tril_ref,
                   proj_ref, cum_ref, carry_ref, *, tiles_per_batch):
    i = pl.program_id(0)
    x = x_ref[...]
    ms = jnp.mean(x * x, axis=-1, keepdims=True)
    y = x * lax.rsqrt(ms + NORM_EPS) * g_ref[...]
    h = (y * (1.0 + mod_ref[0, 1:2, :]) + mod_ref[0, 0:1, :]).astype(bf16)
    n_chunks = proj_ref.shape[1] // 512
    for j in range(n_chunks):
        acc = jnp.dot(h, wm_ref[:, j * 512:(j + 1) * 512], preferred_element_type=f32)
        if j in (0, 3):
            acc = acc * Q_SCALE
        proj_ref[:, j * 512:(j + 1) * 512] = acc.astype(bf16)
    fz = jnp.dot(h, wz_ref[...], preferred_element_type=f32) + bf_ref[...]
    ls = (jnp.minimum(fz, 0.0) - jnp.log(1.0 + jnp.exp(-jnp.abs(fz)))) * LOG2E
    hi, mid, lo = _split3(ls)
    tril = tril_ref[...]
    local = jnp.dot(tril, hi.astype(bf16), preferred_element_type=f32)
    local += jnp.dot(tril, mid.astype(bf16), preferred_element_type=f32)
    local += jnp.dot(tril, lo.astype(bf16), preferred_element_type=f32)

    @pl.when(i % tiles_per_batch == 0)
    def _():
        carry_ref[...] = jnp.zeros_like(carry_ref)

    cum = local + carry_ref[0:1, :]
    cum_ref[...] = cum
    tm = x.shape[0]
    carry_ref[0:1, :] = cum[tm - 1:tm, :]


def _inproj(x2, mod, g1, w_main, w_fz, b_fz, seq):
    t, d = x2.shape
    tm = ROW_TILE
    tiles_per_batch = seq // tm
    n_main = w_main.shape[1]
    tril = (jnp.arange(tm)[:, None] >= jnp.arange(tm)[None, :]).astype(bf16)
    return pl.pallas_call(
        functools.partial(_inproj_kernel, tiles_per_batch=tiles_per_batch),
        grid=(t // tm,),
        in_specs=[pl.BlockSpec((tm, d), lambda i: (i, 0)),
                  pl.BlockSpec((1, 6, d), lambda i: (i // tiles_per_batch, 0, 0)),
                  pl.BlockSpec((1, d), lambda i: (0, 0)),
                  pl.BlockSpec((d, n_main), lambda i: (0, 0)),
                  pl.BlockSpec((d, LANES), lambda i: (0, 0)),
                  pl.BlockSpec((1, LANES), lambda i: (0, 0)),
                  pl.BlockSpec((tm, tm), lambda i: (0, 0))],
        out_specs=[pl.BlockSpec((tm, n_main), lambda i: (i, 0)),
                   pl.BlockSpec((tm, LANES), lambda i: (i, 0))],
        out_shape=[jax.ShapeDtypeStruct((t, n_main), bf16),
                   jax.ShapeDtypeStruct((t, LANES), f32)],
        scratch_shapes=[pltpu.VMEM((8, LANES), f32)],
        compiler_params=_cparams("arbitrary"),
        name="inproj",
    )(x2, mod, g1, w_main, w_fz, b_fz, tril)


def _aug(data, lane, low_map, first, last):
    base = 64 if low_map else 0
    out = jnp.zeros_like(data)
    for n, val in enumerate(tuple(first) + tuple(last)):
        out = jnp.where(lane == base + n, val, out)
    keep = (lane < 64) if low_map else (lane >= 64)
    return jnp.where(keep, data, out)


_ONES3 = (1.0, 1.0, 1.0)


def _q_aug(q, lane, low_map, c):
    return _aug(q, lane, low_map, _split3(c), _ONES3).astype(bf16)


def _k_aug(k, lane, low_map, c):
    hi, mid, lo = _split3(c)
    return _aug(k, lane, low_map, _ONES3, (-hi, -mid, -lo)).astype(bf16)


ONES_ROWS = 16


def _vt_aug(vt):
    row = lax.broadcasted_iota(i32, (ONES_ROWS, vt.shape[1]), 0)
    extra = jnp.where(row == 0, 1.0, 0.0).astype(vt.dtype)
    return jnp.concatenate([vt, extra], axis=0).astype(bf16)


def _flash_scratch(tq, acc_rows):
    return ([pltpu.VMEM((ATT_WIDE, tq), f32)] * 2
            + [pltpu.VMEM((acc_rows, tq), f32)] * 2 + [pltpu.VMEM((1, tq), f32)] * 4)


def _flash_sweep(make_q, n_q, ka_s, kb_s, vta_s, vtb_s, mask_ref, tq, finish, scratch):
    s_a, s_b, acc_a, acc_b, m_a, m_b, tmax_a, tmax_b = scratch
    nt = (((1,), (1,)), ((), ()))
    wide = ATT_WIDE
    assert tq == wide
    s_refs, tmax_refs = (s_a, s_b), (tmax_a, tmax_b)
    k_refs, vt_refs, m_refs, acc_refs = (ka_s, kb_s), (vta_s, vtb_s), (m_a, m_b), (acc_a, acc_b)
    items = [(x, c * ATT_COLS) for x in range(2) for c in range(tq // ATT_COLS)]
    steps = [(qi, j) for qi in range(n_q) for j in range(qi + 1)]
    q_cache = {}

    def q_of(qi):
        if qi not in q_cache:
            q_cache[qi] = make_q(qi)
        return q_cache[qi]

    def key_chunks(width):
        return [(k0, ATT_KEYS) for k0 in range(0, width, ATT_KEYS)]

    def scores_to(step, item):
        (qi, j), (x, c0) = step, item
        cols = slice(c0, c0 + ATT_COLS)
        q_cols = q_of(qi)[x][cols, :]
        tile_max = None
        for k0, kn in key_chunks(wide):
            s = lax.dot_general(k_refs[x][j * wide + k0:j * wide + k0 + kn, :], q_cols, nt,
                                preferred_element_type=f32)
            s_refs[x][k0:k0 + kn, cols] = s
            c_max = jnp.max(s, axis=0, keepdims=True)
            tile_max = c_max if tile_max is None else jnp.maximum(tile_max, c_max)
        tmax_refs[x][:, cols] = tile_max

    def update(step, item):
        (qi, j), (x, c0) = step, item
        cols = slice(c0, c0 + ATT_COLS)
        diagonal = j == qi
        width = c0 + ATT_COLS if diagonal else wide

        def chunk(k0, kn):
            s = s_refs[x][k0:k0 + kn, cols]
            return s + mask_ref[k0:k0 + kn, cols] if diagonal else s

        if diagonal:
            tile_max = None
            for k0, kn in key_chunks(width):
                c_max = jnp.max(chunk(k0, kn), axis=0, keepdims=True)
                tile_max = c_max if tile_max is None else jnp.maximum(tile_max, c_max)
        else:
            tile_max = tmax_refs[x][:, cols]
        m = m_refs[x][:, cols]
        m_new = jnp.maximum(m, tile_max)
        pv = None
        for k0, kn in key_chunks(width):
            p = jnp.exp2(chunk(k0, kn) - m_new).astype(bf16)
            part = jnp.dot(vt_refs[x][:, j * wide + k0:j * wide + k0 + kn], p,
                           preferred_element_type=f32)
            pv = part if pv is None else pv + part
        acc_refs[x][:, cols] = jnp.exp2(m - m_new) * acc_refs[x][:, cols] + pv
        m_refs[x][:, cols] = m_new

    def reset():
        for x in range(2):
            m_refs[x][...] = jnp.full(m_refs[x].shape, NEG, f32)
            acc_refs[x][...] = jnp.zeros(acc_refs[x].shape, f32)

    reset()
    for item in items:
        scores_to(steps[0], item)
    for n, step in enumerate(steps):
        nxt = steps[n + 1] if n + 1 < len(steps) else None
        for item in items:
            update(step, item)
            if nxt is not None:
                scores_to(nxt, item)
        if step[1] == step[0]:
            finish(step[0], acc_a[...], acc_b[...])
            if nxt is not None:
                reset()


def _fox_kernel(q_ref, k_ref, v_ref, cum_ref, mask_ref, o_ref, ka_s, kb_s, vta_s, vtb_s,
                *flash_scratch, seq, tq):
    hp = pl.program_id(1)
    lane = lax.broadcasted_iota(i32, (1, LANES), 1)

    def head_cums(cm):
        c_a = jnp.sum(jnp.where(lane == 2 * hp, cm, 0.0), axis=-1, keepdims=True)
        c_b = jnp.sum(jnp.where(lane == 2 * hp + 1, cm, 0.0), axis=-1, keepdims=True)
        return c_a, c_b

    for c in range(seq // 512):
        rows = pl.ds(c * 512, 512)
        kk = k_ref[0, rows, :].astype(f32)
        c_a, c_b = head_cums(cum_ref[0, rows, :])
        ka_s[rows, :] = _k_aug(kk, lane, True, c_a)
        kb_s[rows, :] = _k_aug(kk, lane, False, c_b)
        vt = v_ref[0, rows, :].astype(f32).T
        vta_s[:, rows] = _vt_aug(vt[:HEAD_DIM])
        vtb_s[:, rows] = _vt_aug(vt[HEAD_DIM:])

    def make_q(qi):
        rows = pl.ds(qi * tq, tq)
        q = q_ref[0, rows, :].astype(f32)
        c_a, c_b = head_cums(cum_ref[0, rows, :])
        return _q_aug(q, lane, True, c_a), _q_aug(q, lane, False, c_b)

    def finish(qi, acc_a, acc_b):
        o_a = acc_a[:HEAD_DIM] / acc_a[HEAD_DIM:HEAD_DIM + 1]
        o_b = acc_b[:HEAD_DIM] / acc_b[HEAD_DIM:HEAD_DIM + 1]
        o_ref[0, pl.ds(qi * tq, tq), :] = jnp.concatenate([o_a, o_b], axis=0).T.astype(bf16)

    _flash_sweep(make_q, seq // tq, ka_s, kb_s, vta_s, vtb_s, mask_ref, tq, finish, flash_scratch)


def _tail_mask(diag):
    diag_t = jnp.swapaxes(diag, -1, -2)
    pad = [(0, 0)] * (diag.ndim - 2) + [(ATT_WIDE - diag.shape[-1], 0), (0, 0)]
    return jnp.pad(diag_t, pad)


def _fox_attention(proj3, cum3):
    bsz, seq, _ = proj3.shape
    tq, tk = ATT_TQ, ATT_WIDE
    n_pairs = FOX_HEADS // 2
    r = jnp.arange(tq)
    mask = _tail_mask(jnp.where(r[:, None] >= r[None, :], 0.0, NEG).astype(f32))
    return pl.pallas_call(
        functools.partial(_fox_kernel, seq=seq, tq=tq),
        grid=(bsz, n_pairs),
        in_specs=[pl.BlockSpec((1, seq, LANES), lambda b, h: (b, 0, h)),
                  pl.BlockSpec((1, seq, LANES), lambda b, h: (b, 0, 4 + h)),
                  pl.BlockSpec((1, seq, LANES), lambda b, h: (b, 0, 8 + h)),
                  pl.BlockSpec((1, seq, LANES), lambda b, h: (b, 0, 0)),
                  pl.BlockSpec((tk, tq), lambda b, h: (0, 0))],
        out_specs=pl.BlockSpec((1, seq, LANES), lambda b, h: (b, 0, h)),
        out_shape=jax.ShapeDtypeStruct((bsz, seq, FOX_WIDTH), bf16),
        scratch_shapes=[pltpu.VMEM((seq, LANES), bf16), pltpu.VMEM((seq, LANES), bf16),
                        pltpu.VMEM((HEAD_DIM + ONES_ROWS, seq), bf16),
                        pltpu.VMEM((HEAD_DIM + ONES_ROWS, seq), bf16)]
        + _flash_scratch(tq, HEAD_DIM + ONES_ROWS),
        compiler_params=_cparams("parallel", "parallel"),
        name="fox_attn",
    )(proj3, proj3, proj3, cum3, mask)


def _diff_kernel(slope_ref, q_ref, k_ref, v_ref, mask_ref, lamv_ref, g_ref, o_ref,
                 ka_s, kb_s, vt_s, *flash_scratch, seq, tq, lam_init):
    h = pl.program_id(1)
    lane = lax.broadcasted_iota(i32, (1, LANES), 1)
    slope = slope_ref[h]

    def pos_bias(start, n):
        pos = (start + lax.broadcasted_iota(i32, (n, 1), 0)).astype(f32)
        return -(slope * pos)

    for c in range(seq // 512):
        rows = pl.ds(c * 512, 512)
        kk = k_ref[0, rows, :].astype(f32)
        cb = pos_bias(c * 512, 512)
        ka_s[rows, :] = _k_aug(kk, lane, True, cb)
        kb_s[rows, :] = _k_aug(kk, lane, False, cb)
        vt_s[:, rows] = _vt_aug(v_ref[0, rows, :].astype(f32).T)

    def make_q(qi):
        q = q_ref[0, pl.ds(qi * tq, tq), :].astype(f32)
        cq = pos_bias(qi * tq, tq)
        return _q_aug(q, lane, True, cq), _q_aug(q, lane, False, cq)

    def finish(qi, acc_a, acc_b):
        lv = lamv_ref[...]
        s1 = jnp.sum(lv[0:1, :] * lv[1:2, :], axis=-1, keepdims=True)
        s2 = jnp.sum(lv[2:3, :] * lv[3:4, :], axis=-1, keepdims=True)
        lam = jnp.exp(s1) - jnp.exp(s2) + lam_init
        o_a = acc_a[:LANES] / acc_a[LANES:LANES + 1]
        o_b = acc_b[:LANES] / acc_b[LANES:LANES + 1]
        d = (o_a - lam * o_b).T
        y = d * lax.rsqrt(jnp.mean(d * d, axis=-1, keepdims=True) + SUBLN_EPS) * g_ref[...]
        o_ref[0, pl.ds(qi * tq, tq), :] = (y * (1.0 - lam_init)).astype(bf16)

    _flash_sweep(make_q, seq // tq, ka_s, kb_s, vt_s, vt_s, mask_ref.at[0], tq, finish, flash_scratch)


def _diff_attention(proj3, lam_vecs, subln_g, lam_init):
    bsz, seq, _ = proj3.shape
    tq, tk = ATT_TQ, ATT_WIDE
    slopes = jnp.asarray([2.0 ** (-8.0 * (i + 1) / DIFF_HEADS) for i in range(DIFF_HEADS)], f32) * LOG2E
    r = jnp.arange(tq)
    tq_i, tk_i = r[:, None], r[None, :]
    chunk_ok = (tq_i // CHUNK) >= (tk_i // CHUNK)
    ahead = jnp.maximum(tk_i - tq_i, 0).astype(f32)
    mask = _tail_mask(jnp.where(chunk_ok[None], -2.0 * slopes[:, None, None] * ahead[None], NEG).astype(f32))
    grid_spec = pltpu.PrefetchScalarGridSpec(
        num_scalar_prefetch=1,
        grid=(bsz, DIFF_HEADS),
        in_specs=[pl.BlockSpec((1, seq, LANES), lambda b, h, s: (b, 0, 12 + h)),
                  pl.BlockSpec((1, seq, LANES), lambda b, h, s: (b, 0, 16 + h)),
                  pl.BlockSpec((1, seq, LANES), lambda b, h, s: (b, 0, 20 + h)),
                  pl.BlockSpec((1, tk, tq), lambda b, h, s: (h, 0, 0)),
                  pl.BlockSpec((8, LANES), lambda b, h, s: (0, 0)),
                  pl.BlockSpec((1, LANES), lambda b, h, s: (0, 0))],
        out_specs=pl.BlockSpec((1, seq, LANES), lambda b, h, s: (b, 0, h)),
        scratch_shapes=[pltpu.VMEM((seq, LANES), bf16), pltpu.VMEM((seq, LANES), bf16),
                        pltpu.VMEM((LANES + ONES_ROWS, seq), bf16)]
        + _flash_scratch(tq, LANES + ONES_ROWS),
    )
    return pl.pallas_call(
        functools.partial(_diff_kernel, seq=seq, tq=tq, lam_init=lam_init),
        grid_spec=grid_spec,
        out_shape=jax.ShapeDtypeStruct((bsz, seq, DIFF_WIDTH), bf16),
        compiler_params=_cparams("parallel", "parallel"),
        name="diff_attn",
    )(slopes, proj3, proj3, proj3, mask, lam_vecs, subln_g)


ROUTER_ROWS = 8 + N_EXPERTS


def _outproj_kernel(fox_ref, diff_ref, x_ref, mod_ref, g_ref, wo_ref, wr_hi_ref, wr_lo_ref,
                    br_ref, triu_ref, x1_ref, hp_ref, route_ref, wcol_ref, cnt_ref):
    tm = x_ref.shape[0]
    y = jnp.dot(fox_ref[...], wo_ref[0:FOX_WIDTH, :], preferred_element_type=f32)
    y += jnp.dot(diff_ref[...], wo_ref[FOX_WIDTH:, :], preferred_element_type=f32)
    x1 = x_ref[...] + mod_ref[0, 2:3, :] * y
    x1_ref[...] = x1
    ms = jnp.mean(x1 * x1, axis=-1, keepdims=True)
    h2 = (x1 * lax.rsqrt(ms + NORM_EPS) * g_ref[...]) * (1.0 + mod_ref[0, 4:5, :]) + mod_ref[0, 3:4, :]
    _store_slabs(hp_ref, _pack_rows(h2))

    nt = (((1,), (1,)), ((), ()))
    h_hi = h2.astype(bf16)
    h_lo = (h2 - h_hi.astype(f32)).astype(bf16)
    logits = lax.dot_general(wr_hi_ref[...], h_hi, nt, preferred_element_type=f32)
    logits += lax.dot_general(wr_lo_ref[...], h_hi, nt, preferred_element_type=f32)
    logits += lax.dot_general(wr_hi_ref[...], h_lo, nt, preferred_element_type=f32)
    logits = logits + br_ref[...]

    row8 = lax.broadcasted_iota(i32, (8, tm), 0)
    gl = jnp.where(row8 < N_GROUPS, logits[0:8, :], NEG)
    gmax = jnp.max(gl, axis=0, keepdims=True)
    grp = jnp.min(jnp.where(gl == gmax, row8, 8), axis=0, keepdims=True)
    p_g = 1.0 / jnp.sum(jnp.exp(gl - gmax), axis=0, keepdims=True)
    sel = logits[8:16, :]
    for g in range(1, N_GROUPS):
        sel = jnp.where(grp == g, logits[8 + 8 * g:16 + 8 * g, :], sel)
    v1 = jnp.max(sel, axis=0, keepdims=True)
    i1 = jnp.min(jnp.where(sel == v1, row8, 8), axis=0, keepdims=True)
    sel2 = jnp.where(row8 == i1, -jnp.inf, sel)
    v2 = jnp.max(sel2, axis=0, keepdims=True)
    i2 = jnp.min(jnp.where(sel2 == v2, row8, 8), axis=0, keepdims=True)
    e21 = jnp.exp(v2 - v1)
    w1 = p_g / (1.0 + e21)
    w2 = w1 * e21
    e1 = grp * EXPERTS_PER_GROUP + i1
    e2 = grp * EXPERTS_PER_GROUP + i2

    row32 = lax.broadcasted_iota(i32, (N_EXPERTS, tm), 0)
    oh1 = row32 == e1
    oh2 = row32 == e2
    both = jnp.where(oh1 | oh2, 1.0, 0.0)
    prefix = jnp.dot(both.astype(bf16), triu_ref[...], preferred_element_type=f32)
    r1 = jnp.sum(jnp.where(oh1, prefix, 0.0), axis=0, keepdims=True).astype(i32)
    r2 = jnp.sum(jnp.where(oh2, prefix, 0.0), axis=0, keepdims=True).astype(i32)
    route = jnp.where(row8 == 0, e1, jnp.where(row8 == 1, e2,
                      jnp.where(row8 == 2, r1, jnp.where(row8 == 3, r2, 0))))
    route_ref[...] = route
    cnt = jnp.sum(both, axis=1, keepdims=True)
    cnt_ref[0] = jnp.broadcast_to(cnt, (N_EXPERTS, LANES)).astype(i32)
    row128 = lax.broadcasted_iota(i32, (LANES, tm), 0)
    w_rows = jnp.where(row128 == 0, w1, jnp.where(row128 == 1, w2, 0.0))
    wcol_ref[...] = w_rows.T


def _outproj(fox2, diff2, x2, mod, g2, w_o, wr_hi, wr_lo, b_r, seq):
    t, d = x2.shape
    tm = ROW_TILE
    tiles_per_batch = seq // tm
    n_tiles = t // tm
    triu = (jnp.arange(tm)[:, None] < jnp.arange(tm)[None, :]).astype(bf16)
    row = lambda i: (i, 0)
    const = lambda i: (0, 0)
    return pl.pallas_call(
        _outproj_kernel,
        grid=(n_tiles,),
        in_specs=[pl.BlockSpec((tm, FOX_WIDTH), row),
                  pl.BlockSpec((tm, DIFF_WIDTH), row),
                  pl.BlockSpec((tm, d), row),
                  pl.BlockSpec((1, 6, d), lambda i: (i // tiles_per_batch, 0, 0)),
                  pl.BlockSpec((1, d), const),
                  pl.BlockSpec((d, d), const),
                  pl.BlockSpec((ROUTER_ROWS, d), const),
                  pl.BlockSpec((ROUTER_ROWS, d), const),
                  pl.BlockSpec((ROUTER_ROWS, 1), const),
                  pl.BlockSpec((tm, tm), const)],
        out_specs=[pl.BlockSpec((tm, d), row),
                   pl.BlockSpec((ROW_CHUNKS * tm, LANES), row),
                   pl.BlockSpec((8, tm), lambda i: (0, i)),
                   pl.BlockSpec((tm, LANES), row),
                   pl.BlockSpec((1, N_EXPERTS, LANES), lambda i: (i, 0, 0))],
        out_shape=[jax.ShapeDtypeStruct((t, d), f32),
                   jax.ShapeDtypeStruct((ROW_CHUNKS * t, LANES), i32),
                   jax.ShapeDtypeStruct((8, t), i32),
                   jax.ShapeDtypeStruct((t, LANES), f32),
                   jax.ShapeDtypeStruct((n_tiles, N_EXPERTS, LANES), i32)],
        compiler_params=_cparams("parallel"),
        name="outproj_router",
    )(fox2, diff2, x2, mod, g2, w_o, wr_hi, wr_lo, b_r, triu)


def _dispatch_kernel(dest_ref, h_ref, xs_in_ref, xs_ref, sem, *, tm):
    del xs_in_ref
    i = pl.program_id(0)
    base = i * (2 * tm)

    def row_copy(r, k):
        d = dest_ref[base + 2 * r + k]
        src = h_ref.at[pl.ds(pl.multiple_of(ROW_CHUNKS * r, ROW_CHUNKS), ROW_CHUNKS), :]
        return pltpu.make_async_copy(src, xs_ref.at[d], sem)

    def issue(r, _):
        row_copy(r, 0).start()
        row_copy(r, 1).start()
        return 0

    lax.fori_loop(0, tm, issue, 0, unroll=8)

    def drain(r, _):
        row_copy(r, 0).wait()
        row_copy(r, 1).wait()
        return 0

    lax.fori_loop(0, tm, drain, 0, unroll=8)


def _dispatch(dest, h_packed, n_slots):
    t = h_packed.shape[0] // ROW_CHUNKS
    tm = GATHER_TILE
    xs0 = jnp.zeros((n_slots, ROW_CHUNKS, LANES), i32)
    grid_spec = pltpu.PrefetchScalarGridSpec(
        num_scalar_prefetch=1,
        grid=(t // tm,),
        in_specs=[pl.BlockSpec((ROW_CHUNKS * tm, LANES), lambda i, d: (i, 0)),
                  pl.BlockSpec(memory_space=pl.ANY)],
        out_specs=pl.BlockSpec(memory_space=pl.ANY),
        scratch_shapes=[pltpu.SemaphoreType.DMA(())],
    )
    return pl.pallas_call(
        functools.partial(_dispatch_kernel, tm=tm),
        grid_spec=grid_spec,
        out_shape=jax.ShapeDtypeStruct((n_slots, ROW_CHUNKS, LANES), i32),
        input_output_aliases={2: 0},
        compiler_params=_cparams("arbitrary"),
        name="moe_dispatch",
    )(dest, h_packed, xs0)


def _experts_kernel(be_ref, nu_ref, xs_ref, wg_ref, wu_ref, wd_ref, y_ref, wg_s, wu_s, wd_s):
    i = pl.program_id(0)
    prev = be_ref[jnp.maximum(i - 1, 0)]
    fresh = jnp.logical_or(i == 0, be_ref[i] != prev)

    @pl.when(jnp.logical_and(fresh, i < nu_ref[0]))
    def _():
        wg_s[...] = wg_ref[0].astype(bf16)
        wu_s[...] = wu_ref[0].astype(bf16)
        wd_s[...] = wd_ref[0].astype(bf16)

    @pl.when(i < nu_ref[0])
    def _():
        lo, hi = _unpack_rows(_load_slabs(xs_ref, MOE_BLOCK))
        xb = jnp.concatenate([lo.astype(bf16), hi.astype(bf16)], axis=-1)
        g = jnp.dot(xb, wg_s[...], preferred_element_type=f32)
        u = jnp.dot(xb, wu_s[...], preferred_element_type=f32)
        hid = (g * (1.0 / (1.0 + jnp.exp(-g)))) * u
        y = jnp.dot(hid.astype(bf16), wd_s[...], preferred_element_type=f32)
        _store_slabs(y_ref, _pack_rows(y))

    @pl.when(i >= nu_ref[0])
    def _():
        y_ref[...] = jnp.zeros_like(y_ref)


def _experts(block_e, n_used, xs, w_gate, w_up, w_down):
    n_slots = xs.shape[0]
    bk = MOE_BLOCK
    grid_spec = pltpu.PrefetchScalarGridSpec(
        num_scalar_prefetch=2,
        grid=(n_slots // bk,),
        in_specs=[pl.BlockSpec((ROW_CHUNKS * bk, LANES), lambda i, be, nu: (i, 0)),
                  pl.BlockSpec((1, D_MODEL, D_EXPERT), lambda i, be, nu: (be[i], 0, 0)),
                  pl.BlockSpec((1, D_MODEL, D_EXPERT), lambda i, be, nu: (be[i], 0, 0)),
                  pl.BlockSpec((1, D_EXPERT, D_MODEL), lambda i, be, nu: (be[i], 0, 0))],
        out_specs=pl.BlockSpec((ROW_CHUNKS * bk, LANES), lambda i, be, nu: (i, 0)),
        scratch_shapes=[pltpu.VMEM((D_MODEL, D_EXPERT), bf16),
                        pltpu.VMEM((D_MODEL, D_EXPERT), bf16),
                        pltpu.VMEM((D_EXPERT, D_MODEL), bf16)],
    )
    return pl.pallas_call(
        _experts_kernel,
        grid_spec=grid_spec,
        out_shape=jax.ShapeDtypeStruct((ROW_CHUNKS * n_slots, LANES), i32),
        compiler_params=_cparams("arbitrary"),
        name="moe_experts",
    )(block_e, n_used, xs.reshape(-1, LANES), w_gate, w_up, w_down)


def _combine_kernel(dest_ref, y_ref, x1_ref, wcol_ref, mod_ref, g_ref, o_ref, buf, sems, *, tm):
    i = pl.program_id(0)
    n = pl.num_programs(0)

    def row_copy(tile, slot, r, k):
        d = dest_ref[tile * (2 * tm) + 2 * r + k]
        dst = buf.at[slot, k, pl.ds(pl.multiple_of(ROW_CHUNKS * r, ROW_CHUNKS), ROW_CHUNKS), :]
        return pltpu.make_async_copy(y_ref.at[d], dst, sems.at[slot])

    def issue_tile(tile, slot):
        def body(r, _):
            row_copy(tile, slot, r, 0).start()
            row_copy(tile, slot, r, 1).start()
            return 0
        lax.fori_loop(0, tm, body, 0, unroll=8)

    @pl.when(i == 0)
    def _():
        issue_tile(0, 0)

    @pl.when(i + 1 < n)
    def _():
        issue_tile(i + 1, (i + 1) % 2)

    slot = i % 2

    def drain(r, _):
        row_copy(i, slot, r, 0).wait()
        row_copy(i, slot, r, 1).wait()
        return 0

    lax.fori_loop(0, tm, drain, 0, unroll=8)

    wc = wcol_ref[...]
    w0 = wc[:, 0:1]
    w1 = wc[:, 1:2]
    lo0, hi0 = _unpack_rows(_load_slabs(buf.at[slot, 0], tm))
    lo1, hi1 = _unpack_rows(_load_slabs(buf.at[slot, 1], tm))
    moe = jnp.concatenate([w0 * lo0 + w1 * lo1, w0 * hi0 + w1 * hi1], axis=-1)
    x = x1_ref[...] + mod_ref[0, 5:6, :] * moe
    ms = jnp.mean(x * x, axis=-1, keepdims=True)
    o_ref[...] = x * lax.rsqrt(ms + NORM_EPS) * g_ref[...]


def _combine(dest, y_packed, x1, wcol, mod, g_f, seq):
    t, d = x1.shape
    tm = GATHER_TILE
    tiles_per_batch = seq // tm
    grid_spec = pltpu.PrefetchScalarGridSpec(
        num_scalar_prefetch=1,
        grid=(t // tm,),
        in_specs=[pl.BlockSpec(memory_space=pl.ANY),
                  pl.BlockSpec((tm, d), lambda i, ds: (i, 0)),
                  pl.BlockSpec((tm, LANES), lambda i, ds: (i, 0)),
                  pl.BlockSpec((1, 6, d), lambda i, ds: (i // tiles_per_batch, 0, 0)),
                  pl.BlockSpec((1, d), lambda i, ds: (0, 0))],
        out_specs=pl.BlockSpec((tm, d), lambda i, ds: (i, 0)),
        scratch_shapes=[pltpu.VMEM((2, 2, ROW_CHUNKS * tm, LANES), i32),
                        pltpu.SemaphoreType.DMA((2,))],
    )
    return pl.pallas_call(
        functools.partial(_combine_kernel, tm=tm),
        grid_spec=grid_spec,
        out_shape=jax.ShapeDtypeStruct((t, d), f32),
        compiler_params=_cparams("arbitrary"),
        name="moe_combine",
    )(dest, y_packed.reshape(-1, ROW_CHUNKS, LANES), x1, wcol, mod, g_f)


def _route_tables(route, cnt, tm):
    n_tiles = cnt.shape[0]
    t = route.shape[1]
    bk = MOE_BLOCK
    n_blocks = (2 * t) // bk + N_EXPERTS
    ti = jnp.arange(n_tiles)
    ei = jnp.arange(N_EXPERTS)
    tile_base = jnp.sum(jnp.where((ti[:, None] > ti[None, :])[:, :, None], cnt[None], 0), axis=1)
    total = jnp.sum(cnt, axis=0)
    padded = (total + bk - 1) // bk * bk
    pends = jnp.sum(jnp.where(ei[None, :] <= ei[:, None], padded[None, :], 0), axis=1)
    base = (pends - padded)[None, :] + tile_base
    base_tok = jnp.broadcast_to(base[:, None, :], (n_tiles, tm, N_EXPERTS)).reshape(t, N_EXPERTS)

    def slot_base(e):
        return jnp.sum(jnp.where(ei[None, :] == e[:, None], base_tok, 0), axis=1)

    d0 = slot_base(route[0]) + route[2]
    d1 = slot_base(route[1]) + route[3]
    dest = jnp.stack([d0, d1], axis=1).reshape(-1).astype(i32)
    block_start = jnp.arange(n_blocks, dtype=i32) * bk
    block_e = jnp.clip(jnp.sum(pends[None, :] <= block_start[:, None], axis=1),
                       0, N_EXPERTS - 1).astype(i32)
    n_used = (pends[-1] // bk).astype(i32).reshape(1)
    return dest, block_e, n_used, n_blocks * bk


def _layer(x2, c, seq, l, ada_w, ada_b, norm1_g, w_in, b_f, lam_q1, lam_k1, lam_q2, lam_k2,
           subln_g, w_o, norm2_g, w_rg, b_rg, w_re, b_re, w_gate, w_up, w_down):
    t, d = x2.shape
    bsz = t // seq
    lam_init = 0.8 - 0.6 * math.exp(-0.3 * l)
    mod = _ada(c, ada_w, ada_b).reshape(bsz, 6, d)

    z0 = 3 * FOX_WIDTH
    w_main = jnp.concatenate([w_in[:, :z0], w_in[:, z0 + FOX_HEADS:]], axis=1).astype(bf16)
    w_fz = jnp.pad(w_in[:, z0:z0 + FOX_HEADS], ((0, 0), (0, LANES - FOX_HEADS))).astype(bf16)
    b_fz = jnp.pad(b_f, (0, LANES - FOX_HEADS)).reshape(1, LANES)
    proj, cum = _inproj(x2, mod, norm1_g.reshape(1, d), w_main, w_fz, b_fz, seq)
    proj3 = proj.reshape(bsz, seq, -1)

    fox = _fox_attention(proj3, cum.reshape(bsz, seq, LANES))
    lam_vecs = jnp.pad(jnp.stack([lam_q1, lam_k1, lam_q2, lam_k2]).astype(f32),
                       ((0, 4), (0, LANES - HEAD_DIM)))
    diff = _diff_attention(proj3, lam_vecs, subln_g.reshape(1, LANES), lam_init)

    w_r = jnp.concatenate([w_rg.T, jnp.zeros((8 - N_GROUPS, d), f32), w_re.T], axis=0)
    wr_hi = w_r.astype(bf16)
    wr_lo = (w_r - wr_hi.astype(f32)).astype(bf16)
    b_r = jnp.concatenate([b_rg, jnp.zeros((8 - N_GROUPS,), f32), b_re]).reshape(ROUTER_ROWS, 1)
    x1, h_packed, route, wcol, cnt = _outproj(
        fox.reshape(t, FOX_WIDTH), diff.reshape(t, DIFF_WIDTH), x2, mod, norm2_g.reshape(1, d),
        w_o.astype(bf16), wr_hi, wr_lo, b_r, seq)

    dest, block_e, n_used, n_slots = _route_tables(route, cnt[:, :, 0], ROW_TILE)
    xs = _dispatch(dest, h_packed, n_slots)
    y_packed = _experts(block_e, n_used, xs, w_gate, w_up, w_down)
    return dest, y_packed, x1, wcol, mod


def kernel(x, c, ada_w, ada_b, norm1_g, w_in, b_f, lam_q1, lam_k1, lam_q2, lam_k2, subln_g, w_o,
           norm2_g, w_rg, b_rg, w_re, b_re, w_gate, w_up, w_down, norm_f_g):
    bsz, seq, d = x.shape
    depth = ada_w.shape[0]
    assert depth == 1 and d == D_MODEL and seq % ROW_TILE == 0 and seq % ATT_WIDE == 0
    x2 = x.reshape(bsz * seq, d)
    dest, y_packed, x1, wcol, mod = _layer(
        x2, c, seq, 0, ada_w[0], ada_b[0], norm1_g[0], w_in[0], b_f[0], lam_q1[0], lam_k1[0],
        lam_q2[0], lam_k2[0], subln_g[0], w_o[0], norm2_g[0], w_rg[0], b_rg[0], w_re[0], b_re[0],
        w_gate[0], w_up[0], w_down[0])
    out = _combine(dest, y_packed, x1, wcol, mod, norm_f_g.reshape(1, d), seq)
    return out.reshape(bsz, seq, d)
```

```python
import functools
import math

import jax
import jax.numpy as jnp
from jax import lax
from jax.experimental import pallas as pl
from jax.experimental.pallas import tpu as pltpu

f32 = jnp.float32
bf16 = jnp.bfloat16
i32 = jnp.int32

D_MODEL = 1024
HEAD_DIM = 64
FOX_HEADS = 8
FOX_WIDTH = FOX_HEADS * HEAD_DIM
DIFF_HEADS = 4
DIFF_QK_WIDTH = DIFF_HEADS * 2 * HEAD_DIM
DIFF_WIDTH = DIFF_HEADS * 2 * HEAD_DIM
CHUNK = 64
N_GROUPS = 4
EXPERTS_PER_GROUP = 8
N_EXPERTS = N_GROUPS * EXPERTS_PER_GROUP
D_EXPERT = 512
NORM_EPS = 1e-6
SUBLN_EPS = 1e-5

LANES = 128
LOG2E = 1.4426950408889634
Q_SCALE = HEAD_DIM ** -0.5 * LOG2E
NEG = -1e30
HALF = D_MODEL // 2
ROW_CHUNKS = HALF // LANES

ROW_TILE = 512
ATT_TQ = 1024
ATT_WIDE = 1024
ATT_COLS = 256
ATT_KEYS = 256
N_PAIRS = EXPERTS_PER_GROUP * (EXPERTS_PER_GROUP - 1) // 2
N_BINS = 128
MOE_BLOCK = 128
X_CHUNKS = 8
GATHER_TILE = 256
VMEM_LIMIT = 48 * 1024 * 1024


def _cparams(*sem):
    return pltpu.CompilerParams(dimension_semantics=sem, vmem_limit_bytes=VMEM_LIMIT)


def _split3(c):
    hi = c.astype(bf16).astype(f32)
    r = c - hi
    mid = r.astype(bf16).astype(f32)
    lo = r - mid
    return hi, mid, lo


def _pack_rows(y):
    a = pltpu.bitcast(y[:, :HALF].astype(bf16).astype(f32), i32)
    b = pltpu.bitcast(y[:, HALF:].astype(bf16).astype(f32), i32)
    return lax.shift_right_logical(a, 16) | (b & jnp.int32(-65536))


def _store_slabs(ref, packed, chunks=ROW_CHUNKS):
    n = packed.shape[0]
    for j in range(ROW_CHUNKS):
        ref[pl.ds(j, n, stride=chunks), :] = packed[:, j * LANES:(j + 1) * LANES]


def _load_slabs(ref, n, chunks=ROW_CHUNKS):
    return jnp.concatenate([ref[pl.ds(j, n, stride=chunks), :] for j in range(ROW_CHUNKS)], axis=-1)


def _unpack_rows(w):
    lo = pltpu.bitcast(lax.shift_left(w, 16), f32)
    hi = pltpu.bitcast(w & jnp.int32(-65536), f32)
    return lo, hi


def _ada_kernel(c_ref, w_ref, b_ref, o_ref):
    c = c_ref[...]
    w = w_ref[...]
    c_hi = c.astype(bf16)
    c_lo = (c - c_hi.astype(f32)).astype(bf16)
    w_hi = w.astype(bf16)
    w_lo = (w - w_hi.astype(f32)).astype(bf16)
    acc = jnp.dot(c_hi, w_hi, preferred_element_type=f32)
    acc += jnp.dot(c_hi, w_lo, preferred_element_type=f32)
    acc += jnp.dot(c_lo, w_hi, preferred_element_type=f32)
    o_ref[...] = acc + b_ref[...]


def _ada(c, w, b):
    bsz, d = c.shape
    n = w.shape[1]
    tn = 1024
    return pl.pallas_call(
        _ada_kernel,
        grid=(n // tn,),
        in_specs=[pl.BlockSpec((bsz, d), lambda j: (0, 0)),
                  pl.BlockSpec((d, tn), lambda j: (0, j)),
                  pl.BlockSpec((1, tn), lambda j: (0, j))],
        out_specs=pl.BlockSpec((bsz, tn), lambda j: (0, j)),
        out_shape=jax.ShapeDtypeStruct((bsz, n), f32),
        compiler_params=_cparams("parallel"),
        name="ada_mod",
    )(c, w, b.reshape(1, n))


def _inproj_kernel(x_ref, mod_ref, g_ref, wm_ref, wz_ref, bf_ref, tril_ref,
                   proj_ref, cum_ref, carry_ref, *, tiles_per_batch):
    i = pl.program_id(0)
    x = x_ref[...]
    ms = jnp.mean(x * x, axis=-1, keepdims=True)
    y = x * lax.rsqrt(ms + NORM_EPS) * g_ref[...]
    h = (y * (1.0 + mod_ref[0, 1:2, :]) + mod_ref[0, 0:1, :]).astype(bf16)
    n_chunks = proj_ref.shape[1] // 512
    for j in range(n_chunks):
        acc = jnp.dot(h, wm_ref[:, j * 512:(j + 1) * 512], preferred_element_type=f32)
        if j in (0, 3):
            acc = acc * Q_SCALE
        proj_ref[:, j * 512:(j + 1) * 512] = acc.astype(bf16)
    fz = jnp.dot(h, wz_ref[...], preferred_element_type=f32) + bf_ref[...]
    ls = (jnp.minimum(fz, 0.0) - jnp.log(1.0 + jnp.exp(-jnp.abs(fz)))) * LOG2E
    hi, mid, lo = _split3(ls)
    tril = tril_ref[...]
    local = jnp.dot(tril, hi.astype(bf16), preferred_element_type=f32)
    local += jnp.dot(tril, mid.astype(bf16), preferred_element_type=f32)
    local += jnp.dot(tril, lo.astype(bf16), preferred_element_type=f32)

    @pl.when(i % tiles_per_batch == 0)
    def _():
        carry_ref[...] = jnp.zeros_like(carry_ref)

    cum = local + carry_ref[0:1, :]
    cum_ref[...] = cum
    tm = x.shape[0]
    carry_ref[0:1, :] = cum[tm - 1:tm, :]


def _inproj(x2, mod, g1, w_main, w_fz, b_fz, seq):
    t, d = x2.shape
    tm = ROW_TILE
    tiles_per_batch = seq // tm
    n_main = w_main.shape[1]
    tril = (jnp.arange(tm)[:, None] >= jnp.arange(tm)[None, :]).astype(bf16)
    return pl.pallas_call(
        functools.partial(_inproj_kernel, tiles_per_batch=tiles_per_batch),
        grid=(t // tm,),
        in_specs=[pl.BlockSpec((tm, d), lambda i: (i, 0)),
                  pl.BlockSpec((1, 6, d), lambda i: (i // tiles_per_batch, 0, 0)),
                  pl.BlockSpec((1, d), lambda i: (0, 0)),
                  pl.BlockSpec((d, n_main), lambda i: (0, 0)),
                  pl.BlockSpec((d, LANES), lambda i: (0, 0)),
                  pl.BlockSpec((1, LANES), lambda i: (0, 0)),
                  pl.BlockSpec((tm, tm), lambda i: (0, 0))],
        out_specs=[pl.BlockSpec((tm, n_main), lambda i: (i, 0)),
                   pl.BlockSpec((tm, LANES), lambda i: (i, 0))],
        out_shape=[jax.ShapeDtypeStruct((t, n_main), bf16),
                   jax.ShapeDtypeStruct((t, LANES), f32)],
        scratch_shapes=[pltpu.VMEM((8, LANES), f32)],
        compiler_params=_cparams("arbitrary"),
        name="inproj",
    )(x2, mod, g1, w_main, w_fz, b_fz, tril)


def _aug(data, lane, low_map, first, last):
    base = 64 if low_map else 0
    out = jnp.zeros_like(data)
    for n, val in enumerate(tuple(first) + tuple(last)):
        out = jnp.where(lane == base + n, val, out)
    keep = (lane < 64) if low_map else (lane >= 64)
    return jnp.where(keep, data, out)


_ONES3 = (1.0, 1.0, 1.0)


def _q_aug(q, lane, low_map, c):
    return _aug(q, lane, low_map, _split3(c), _ONES3).astype(bf16)


def _k_aug(k, lane, low_map, c):
    hi, mid, lo = _split3(c)
    return _aug(k, lane, low_map, _ONES3, (-hi, -mid, -lo)).astype(bf16)


ONES_ROWS = 16


def _vt_aug(vt):
    row = lax.broadcasted_iota(i32, (ONES_ROWS, vt.shape[1]), 0)
    extra = jnp.where(row == 0, 1.0, 0.0).astype(vt.dtype)
    return jnp.concatenate([vt, extra], axis=0).astype(bf16)


def _flash_scratch(tq, acc_rows):
    return ([pltpu.VMEM((ATT_WIDE, tq), f32)] * 2
            + [pltpu.VMEM((acc_rows, tq), f32)] * 2 + [pltpu.VMEM((1, tq), f32)] * 4)


def _flash_sweep(make_q, n_q, ka_s, kb_s, vta_s, vtb_s, mask_ref, tq, finish, scratch):
    s_a, s_b, acc_a, acc_b, m_a, m_b, tmax_a, tmax_b = scratch
    nt = (((1,), (1,)), ((), ()))
    wide = ATT_WIDE
    assert tq == wide
    s_refs, tmax_refs = (s_a, s_b), (tmax_a, tmax_b)
    k_refs, vt_refs, m_refs, acc_refs = (ka_s, kb_s), (vta_s, vtb_s), (m_a, m_b), (acc_a, acc_b)
    items = [(x, c * ATT_COLS) for x in range(2) for c in range(tq // ATT_COLS)]
    steps = [(qi, j) for qi in range(n_q) for j in range(qi + 1)]
    q_cache = {}

    def q_of(qi):
        if qi not in q_cache:
            q_cache[qi] = make_q(qi)
        return q_cache[qi]

    def key_chunks(width):
        return [(k0, ATT_KEYS) for k0 in range(0, width, ATT_KEYS)]

    def scores_to(step, item):
        (qi, j), (x, c0) = step, item
        cols = slice(c0, c0 + ATT_COLS)
        q_cols = q_of(qi)[x][cols, :]
        tile_max = None
        for k0, kn in key_chunks(wide):
            s = lax.dot_general(k_refs[x][j * wide + k0:j * wide + k0 + kn, :], q_cols, nt,
                                preferred_element_type=f32)
            s_refs[x][k0:k0 + kn, cols] = s
            c_max = jnp.max(s, axis=0, keepdims=True)
            tile_max = c_max if tile_max is None else jnp.maximum(tile_max, c_max)
        tmax_refs[x][:, cols] = tile_max

    def update(step, item):
        (qi, j), (x, c0) = step, item
        cols = slice(c0, c0 + ATT_COLS)
        diagonal = j == qi
        width = c0 + ATT_COLS if diagonal else wide

        def chunk(k0, kn):
            s = s_refs[x][k0:k0 + kn, cols]
            return s + mask_ref[k0:k0 + kn, cols] if diagonal else s

        if diagonal:
            tile_max = None
            for k0, kn in key_chunks(width):
                c_max = jnp.max(chunk(k0, kn), axis=0, keepdims=True)
                tile_max = c_max if tile_max is None else jnp.maximum(tile_max, c_max)
        else:
            tile_max = tmax_refs[x][:, cols]
        m = m_refs[x][:, cols]
        m_new = jnp.maximum(m, tile_max)
        pv = None
        for k0, kn in key_chunks(width):
            p = jnp.exp2(chunk(k0, kn) - m_new).astype(bf16)
            part = jnp.dot(vt_refs[x][:, j * wide + k0:j * wide + k0 + kn], p,
                           preferred_element_type=f32)
            pv = part if pv is None else pv + part
        acc_refs[x][:, cols] = jnp.exp2(m - m_new) * acc_refs[x][:, cols] + pv
        m_refs[x][:, cols] = m_new

    def reset():
        for x in range(2):
            m_refs[x][...] = jnp.full(m_refs[x].shape, NEG, f32)
            acc_refs[x][...] = jnp.zeros(acc_refs[x].shape, f32)

    reset()
    for item in items:
        scores_to(steps[0], item)
    for n, step in enumerate(steps):
        nxt = steps[n + 1] if n + 1 < len(steps) else None
        for item in items:
            update(step, item)
            if nxt is not None:
                scores_to(nxt, item)
        if step[1] == step[0]:
            finish(step[0], acc_a[...], acc_b[...])
            if nxt is not None:
                reset()


def _fox_kernel(q_ref, k_ref, v_ref, cum_ref, mask_ref, o_ref, ka_s, kb_s, vta_s, vtb_s,
                *flash_scratch, seq, tq):
    hp = pl.program_id(1)
    lane = lax.broadcasted_iota(i32, (1, LANES), 1)

    def head_cums(cm):
        c_a = jnp.sum(jnp.where(lane == 2 * hp, cm, 0.0), axis=-1, keepdims=True)
        c_b = jnp.sum(jnp.where(lane == 2 * hp + 1, cm, 0.0), axis=-1, keepdims=True)
        return c_a, c_b

    for c in range(seq // 512):
        rows = pl.ds(c * 512, 512)
        kk = k_ref[0, rows, :].astype(f32)
        c_a, c_b = head_cums(cum_ref[0, rows, :])
        ka_s[rows, :] = _k_aug(kk, lane, True, c_a)
        kb_s[rows, :] = _k_aug(kk, lane, False, c_b)
        vt = v_ref[0, rows, :].astype(f32).T
        vta_s[:, rows] = _vt_aug(vt[:HEAD_DIM])
        vtb_s[:, rows] = _vt_aug(vt[HEAD_DIM:])

    def make_q(qi):
        rows = pl.ds(qi * tq, tq)
        q = q_ref[0, rows, :].astype(f32)
        c_a, c_b = head_cums(cum_ref[0, rows, :])
        return _q_aug(q, lane, True, c_a), _q_aug(q, lane, False, c_b)

    def finish(qi, acc_a, acc_b):
        o_a = acc_a[:HEAD_DIM] / acc_a[HEAD_DIM:HEAD_DIM + 1]
        o_b = acc_b[:HEAD_DIM] / acc_b[HEAD_DIM:HEAD_DIM + 1]
        o_ref[0, pl.ds(qi * tq, tq), :] = jnp.concatenate([o_a, o_b], axis=0).T.astype(bf16)

    _flash_sweep(make_q, seq // tq, ka_s, kb_s, vta_s, vtb_s, mask_ref, tq, finish, flash_scratch)


def _tail_mask(diag):
    diag_t = jnp.swapaxes(diag, -1, -2)
    pad = [(0, 0)] * (diag.ndim - 2) + [(ATT_WIDE - diag.shape[-1], 0), (0, 0)]
    return jnp.pad(diag_t, pad)


def _fox_attention(proj3, cum3):
    bsz, seq, _ = proj3.shape
    tq, tk = ATT_TQ, ATT_WIDE
    n_pairs = FOX_HEADS // 2
    r = jnp.arange(tq)
    mask = _tail_mask(jnp.where(r[:, None] >= r[None, :], 0.0, NEG).astype(f32))
    return pl.pallas_call(
        functools.partial(_fox_kernel, seq=seq, tq=tq),
        grid=(bsz, n_pairs),
        in_specs=[pl.BlockSpec((1, seq, LANES), lambda b, h: (b, 0, h)),
                  pl.BlockSpec((1, seq, LANES), lambda b, h: (b, 0, 4 + h)),
                  pl.BlockSpec((1, seq, LANES), lambda b, h: (b, 0, 8 + h)),
                  pl.BlockSpec((1, seq, LANES), lambda b, h: (b, 0, 0)),
                  pl.BlockSpec((tk, tq), lambda b, h: (0, 0))],
        out_specs=pl.BlockSpec((1, seq, LANES), lambda b, h: (b, 0, h)),
        out_shape=jax.ShapeDtypeStruct((bsz, seq, FOX_WIDTH), bf16),
        scratch_shapes=[pltpu.VMEM((seq, LANES), bf16), pltpu.VMEM((seq, LANES), bf16),
                        pltpu.VMEM((HEAD_DIM + ONES_ROWS, seq), bf16),
                        pltpu.VMEM((HEAD_DIM + ONES_ROWS, seq), bf16)]
        + _flash_scratch(tq, HEAD_DIM + ONES_ROWS),
        compiler_params=_cparams("parallel", "parallel"),
        name="fox_attn",
    )(proj3, proj3, proj3, cum3, mask)


def _diff_kernel(slope_ref, q_ref, k_ref, v_ref, mask_ref, lamv_ref, g_ref, o_ref,
                 ka_s, kb_s, vt_s, *flash_scratch, seq, tq, lam_init):
    h = pl.program_id(1)
    lane = lax.broadcasted_iota(i32, (1, LANES), 1)
    slope = slope_ref[h]

    def pos_bias(start, n):
        pos = (start + lax.broadcasted_iota(i32, (n, 1), 0)).astype(f32)
        return -(slope * pos)

    for c in range(seq // 512):
        rows = pl.ds(c * 512, 512)
        kk = k_ref[0, rows, :].astype(f32)
        cb = pos_bias(c * 512, 512)
        ka_s[rows, :] = _k_aug(kk, lane, True, cb)
        kb_s[rows, :] = _k_aug(kk, lane, False, cb)
        vt_s[:, rows] = _vt_aug(v_ref[0, rows, :].astype(f32).T)

    def make_q(qi):
        q = q_ref[0, pl.ds(qi * tq, tq), :].astype(f32)
        cq = pos_bias(qi * tq, tq)
        return _q_aug(q, lane, True, cq), _q_aug(q, lane, False, cq)

    def finish(qi, acc_a, acc_b):
        lv = lamv_ref[...]
        s1 = jnp.sum(lv[0:1, :] * lv[1:2, :], axis=-1, keepdims=True)
        s2 = jnp.sum(lv[2:3, :] * lv[3:4, :], axis=-1, keepdims=True)
        lam = jnp.exp(s1) - jnp.exp(s2) + lam_init
        o_a = acc_a[:LANES] / acc_a[LANES:LANES + 1]
        o_b = acc_b[:LANES] / acc_b[LANES:LANES + 1]
        d = (o_a - lam * o_b).T
        y = d * lax.rsqrt(jnp.mean(d * d, axis=-1, keepdims=True) + SUBLN_EPS) * g_ref[...]
        o_ref[0, pl.ds(qi * tq, tq), :] = (y * (1.0 - lam_init)).astype(bf16)

    _flash_sweep(make_q, seq // tq, ka_s, kb_s, vt_s, vt_s, mask_ref.at[0], tq, finish, flash_scratch)


def _diff_attention(proj3, lam_vecs, subln_g, lam_init):
    bsz, seq, _ = proj3.shape
    tq, tk = ATT_TQ, ATT_WIDE
    slopes = jnp.asarray([2.0 ** (-8.0 * (i + 1) / DIFF_HEADS) for i in range(DIFF_HEADS)], f32) * LOG2E
    r = jnp.arange(tq)
    tq_i, tk_i = r[:, None], r[None, :]
    chunk_ok = (tq_i // CHUNK) >= (tk_i // CHUNK)
    ahead = jnp.maximum(tk_i - tq_i, 0).astype(f32)
    mask = _tail_mask(jnp.where(chunk_ok[None], -2.0 * slopes[:, None, None] * ahead[None], NEG).astype(f32))
    grid_spec = pltpu.PrefetchScalarGridSpec(
        num_scalar_prefetch=1,
        grid=(bsz, DIFF_HEADS),
        in_specs=[pl.BlockSpec((1, seq, LANES), lambda b, h, s: (b, 0, 12 + h)),
                  pl.BlockSpec((1, seq, LANES), lambda b, h, s: (b, 0, 16 + h)),
                  pl.BlockSpec((1, seq, LANES), lambda b, h, s: (b, 0, 20 + h)),
                  pl.BlockSpec((1, tk, tq), lambda b, h, s: (h, 0, 0)),
                  pl.BlockSpec((8, LANES), lambda b, h, s: (0, 0)),
                  pl.BlockSpec((1, LANES), lambda b, h, s: (0, 0))],
        out_specs=pl.BlockSpec((1, seq, LANES), lambda b, h, s: (b, 0, h)),
        scratch_shapes=[pltpu.VMEM((seq, LANES), bf16), pltpu.VMEM((seq, LANES), bf16),
                        pltpu.VMEM((LANES + ONES_ROWS, seq), bf16)]
        + _flash_scratch(tq, LANES + ONES_ROWS),
    )
    return pl.pallas_call(
        functools.partial(_diff_kernel, seq=seq, tq=tq, lam_init=lam_init),
        grid_spec=grid_spec,
        out_shape=jax.ShapeDtypeStruct((bsz, seq, DIFF_WIDTH), bf16),
        compiler_params=_cparams("parallel", "parallel"),
        name="diff_attn",
    )(slopes, proj3, proj3, proj3, mask, lam_vecs, subln_g)


ROUTER_ROWS = 8 + N_EXPERTS


def _outproj_kernel(fox_ref, diff_ref, x_ref, mod_ref, g_ref, wo_ref, wr_hi_ref, wr_lo_ref,
                    br_ref, triu_ref, x1_ref, hp_ref, route_ref, cnt_ref):
    tm = x_ref.shape[0]
    y = jnp.dot(fox_ref[...], wo_ref[0:FOX_WIDTH, :], preferred_element_type=f32)
    y += jnp.dot(diff_ref[...], wo_ref[FOX_WIDTH:, :], preferred_element_type=f32)
    x1 = x_ref[...] + mod_ref[0, 2:3, :] * y
    x1_ref[...] = x1
    ms = jnp.mean(x1 * x1, axis=-1, keepdims=True)
    h2 = (x1 * lax.rsqrt(ms + NORM_EPS) * g_ref[...]) * (1.0 + mod_ref[0, 4:5, :]) + mod_ref[0, 3:4, :]
    _store_slabs(hp_ref, _pack_rows(h2), X_CHUNKS)

    nt = (((1,), (1,)), ((), ()))
    h_hi = h2.astype(bf16)
    h_lo = (h2 - h_hi.astype(f32)).astype(bf16)
    logits = lax.dot_general(wr_hi_ref[...], h_hi, nt, preferred_element_type=f32)
    logits += lax.dot_general(wr_lo_ref[...], h_hi, nt, preferred_element_type=f32)
    logits += lax.dot_general(wr_hi_ref[...], h_lo, nt, preferred_element_type=f32)
    logits = logits + br_ref[...]

    row8 = lax.broadcasted_iota(i32, (8, tm), 0)
    gl = jnp.where(row8 < N_GROUPS, logits[0:8, :], NEG)
    gmax = jnp.max(gl, axis=0, keepdims=True)
    grp = jnp.min(jnp.where(gl == gmax, row8, 8), axis=0, keepdims=True)
    p_g = 1.0 / jnp.sum(jnp.exp(gl - gmax), axis=0, keepdims=True)
    sel = logits[8:16, :]
    for g in range(1, N_GROUPS):
        sel = jnp.where(grp == g, logits[8 + 8 * g:16 + 8 * g, :], sel)
    v1 = jnp.max(sel, axis=0, keepdims=True)
    i1 = jnp.min(jnp.where(sel == v1, row8, 8), axis=0, keepdims=True)
    sel2 = jnp.where(row8 == i1, -jnp.inf, sel)
    v2 = jnp.max(sel2, axis=0, keepdims=True)
    i2 = jnp.min(jnp.where(sel2 == v2, row8, 8), axis=0, keepdims=True)
    e21 = jnp.exp(v2 - v1)
    w1 = p_g / (1.0 + e21)
    w2 = w1 * e21
    lo = jnp.minimum(i1, i2)
    hi = jnp.maximum(i1, i2)
    w_lo = jnp.where(i1 < i2, w1, w2)
    w_hi = jnp.where(i1 < i2, w2, w1)
    pair = lax.shift_right_logical(lo * (2 * EXPERTS_PER_GROUP - 1 - lo), 1) + hi - lo - 1
    bins = grp * N_PAIRS + pair

    row_b = lax.broadcasted_iota(i32, (N_BINS, tm), 0)
    hit = row_b == bins
    onehot = jnp.where(hit, 1.0, 0.0)
    prefix = jnp.dot(onehot.astype(bf16), triu_ref[...], preferred_element_type=f32)
    rank = jnp.sum(jnp.where(hit, prefix, 0.0), axis=0, keepdims=True).astype(i32)
    route_ref[...] = jnp.where(row8 == 0, bins, jnp.where(row8 == 1, rank, 0))
    cnt = jnp.sum(onehot, axis=1, keepdims=True)
    cnt_ref[0] = jnp.broadcast_to(cnt, (N_BINS, LANES)).astype(i32)
    w_rows = jnp.where(row_b == 0, w_lo, jnp.where(row_b == 1, w_hi, 0.0))
    hp_ref[pl.ds(ROW_CHUNKS, tm, stride=X_CHUNKS), :] = pltpu.bitcast(w_rows.T, i32)
    for j in range(ROW_CHUNKS + 1, X_CHUNKS):
        hp_ref[pl.ds(j, tm, stride=X_CHUNKS), :] = jnp.zeros((tm, LANES), i32)


def _outproj(fox2, diff2, x2, mod, g2, w_o, wr_hi, wr_lo, b_r, seq):
    t, d = x2.shape
    tm = ROW_TILE
    tiles_per_batch = seq // tm
    n_tiles = t // tm
    triu = (jnp.arange(tm)[:, None] < jnp.arange(tm)[None, :]).astype(bf16)
    row = lambda i: (i, 0)
    const = lambda i: (0, 0)
    return pl.pallas_call(
        _outproj_kernel,
        grid=(n_tiles,),
        in_specs=[pl.BlockSpec((tm, FOX_WIDTH), row),
                  pl.BlockSpec((tm, DIFF_WIDTH), row),
                  pl.BlockSpec((tm, d), row),
                  pl.BlockSpec((1, 6, d), lambda i: (i // tiles_per_batch, 0, 0)),
                  pl.BlockSpec((1, d), const),
                  pl.BlockSpec((d, d), const),
                  pl.BlockSpec((ROUTER_ROWS, d), const),
                  pl.BlockSpec((ROUTER_ROWS, d), const),
                  pl.BlockSpec((ROUTER_ROWS, 1), const),
                  pl.BlockSpec((tm, tm), const)],
        out_specs=[pl.BlockSpec((tm, d), row),
                   pl.BlockSpec((X_CHUNKS * tm, LANES), row),
                   pl.BlockSpec((8, tm), lambda i: (0, i)),
                   pl.BlockSpec((1, N_BINS, LANES), lambda i: (i, 0, 0))],
        out_shape=[jax.ShapeDtypeStruct((t, d), f32),
                   jax.ShapeDtypeStruct((X_CHUNKS * t, LANES), i32),
                   jax.ShapeDtypeStruct((8, t), i32),
                   jax.ShapeDtypeStruct((n_tiles, N_BINS, LANES), i32)],
        compiler_params=_cparams("parallel"),
        name="outproj_router",
    )(fox2, diff2, x2, mod, g2, w_o, wr_hi, wr_lo, b_r, triu)


def _dispatch_kernel(dest_ref, h_ref, xs_in_ref, xs_ref, sem, *, tm):
    del xs_in_ref
    base = pl.program_id(0) * tm

    def row_copy(r):
        src = h_ref.at[pl.ds(pl.multiple_of(X_CHUNKS * r, X_CHUNKS), X_CHUNKS), :]
        return pltpu.make_async_copy(src, xs_ref.at[dest_ref[base + r]], sem)

    def issue(r, _):
        row_copy(r).start()
        return 0

    lax.fori_loop(0, tm, issue, 0, unroll=8)

    def drain(r, _):
        row_copy(r).wait()
        return 0

    lax.fori_loop(0, tm, drain, 0, unroll=8)


def _dispatch(dest, h_rows, n_slots):
    t = h_rows.shape[0] // X_CHUNKS
    tm = GATHER_TILE
    xs0 = jnp.zeros((n_slots, X_CHUNKS, LANES), i32)
    grid_spec = pltpu.PrefetchScalarGridSpec(
        num_scalar_prefetch=1,
        grid=(t // tm,),
        in_specs=[pl.BlockSpec((X_CHUNKS * tm, LANES), lambda i, d: (i, 0)),
                  pl.BlockSpec(memory_space=pl.ANY)],
        out_specs=pl.BlockSpec(memory_space=pl.ANY),
        scratch_shapes=[pltpu.SemaphoreType.DMA(())],
    )
    return pl.pallas_call(
        functools.partial(_dispatch_kernel, tm=tm),
        grid_spec=grid_spec,
        out_shape=jax.ShapeDtypeStruct((n_slots, X_CHUNKS, LANES), i32),
        input_output_aliases={2: 0},
        compiler_params=_cparams("arbitrary"),
        name="moe_dispatch",
    )(dest, h_rows, xs0)


def _experts_kernel(lo_ref, hi_ref, nu_ref, xs_ref, wgl_ref, wul_ref, wdl_ref, wgh_ref, wuh_ref,
                    wdh_ref, y_ref, wgl_s, wul_s, wdl_s, wgh_s, wuh_s, wdh_s):
    bk = MOE_BLOCK
    i = pl.program_id(0)
    live = i < nu_ref[0]
    before = jnp.maximum(i - 1, 0)
    fresh_lo = jnp.logical_or(i == 0, lo_ref[i] != lo_ref[before])
    fresh_hi = jnp.logical_or(i == 0, hi_ref[i] != hi_ref[before])

    @pl.when(jnp.logical_and(fresh_lo, live))
    def _():
        wgl_s[...] = wgl_ref[0].astype(bf16)
        wul_s[...] = wul_ref[0].astype(bf16)
        wdl_s[...] = wdl_ref[0].astype(bf16)

    @pl.when(jnp.logical_and(fresh_hi, live))
    def _():
        wgh_s[...] = wgh_ref[0].astype(bf16)
        wuh_s[...] = wuh_ref[0].astype(bf16)
        wdh_s[...] = wdh_ref[0].astype(bf16)

    @pl.when(live)
    def _():
        lo, hi = _unpack_rows(_load_slabs(xs_ref, bk, X_CHUNKS))
        xb = jnp.concatenate([lo.astype(bf16), hi.astype(bf16)], axis=-1)
        wv = pltpu.bitcast(xs_ref[pl.ds(ROW_CHUNKS, bk, stride=X_CHUNKS), :], f32)

        def mlp(wg_s, wu_s, wd_s):
            g = jnp.dot(xb, wg_s[...], preferred_element_type=f32)
            u = jnp.dot(xb, wu_s[...], preferred_element_type=f32)
            hid = (g * (1.0 / (1.0 + jnp.exp(-g)))) * u
            return jnp.dot(hid.astype(bf16), wd_s[...], preferred_element_type=f32)

        y = wv[:, 0:1] * mlp(wgl_s, wul_s, wdl_s) + wv[:, 1:2] * mlp(wgh_s, wuh_s, wdh_s)
        _store_slabs(y_ref, _pack_rows(y))

    @pl.when(jnp.logical_not(live))
    def _():
        y_ref[...] = jnp.zeros_like(y_ref)


def _experts(block_lo, block_hi, n_used, xs, w_gate, w_up, w_down):
    n_slots = xs.shape[0]
    bk = MOE_BLOCK
    up_spec = lambda which: pl.BlockSpec((1, D_MODEL, D_EXPERT),
                                         lambda i, lo, hi, nu: ((lo, hi)[which][i], 0, 0))
    down_spec = lambda which: pl.BlockSpec((1, D_EXPERT, D_MODEL),
                                           lambda i, lo, hi, nu: ((lo, hi)[which][i], 0, 0))
    grid_spec = pltpu.PrefetchScalarGridSpec(
        num_scalar_prefetch=3,
        grid=(n_slots // bk,),
        in_specs=[pl.BlockSpec((X_CHUNKS * bk, LANES), lambda i, lo, hi, nu: (i, 0)),
                  up_spec(0), up_spec(0), down_spec(0), up_spec(1), up_spec(1), down_spec(1)],
        out_specs=pl.BlockSpec((ROW_CHUNKS * bk, LANES), lambda i, lo, hi, nu: (i, 0)),
        scratch_shapes=[pltpu.VMEM((D_MODEL, D_EXPERT), bf16),
                        pltpu.VMEM((D_MODEL, D_EXPERT), bf16),
                        pltpu.VMEM((D_EXPERT, D_MODEL), bf16)] * 2,
    )
    return pl.pallas_call(
        _experts_kernel,
        grid_spec=grid_spec,
        out_shape=jax.ShapeDtypeStruct((ROW_CHUNKS * n_slots, LANES), i32),
        compiler_params=_cparams("arbitrary"),
        name="moe_experts",
    )(block_lo, block_hi, n_used, xs.reshape(-1, LANES), w_gate, w_up, w_down, w_gate, w_up, w_down)


def _combine_kernel(dest_ref, y_ref, x1_ref, mod_ref, g_ref, o_ref, buf, sems, *, tm):
    i = pl.program_id(0)
    n = pl.num_programs(0)

    def row_copy(tile, slot, r):
        dst = buf.at[slot, pl.ds(pl.multiple_of(ROW_CHUNKS * r, ROW_CHUNKS), ROW_CHUNKS), :]
        return pltpu.make_async_copy(y_ref.at[dest_ref[tile * tm + r]], dst, sems.at[slot])

    def issue_tile(tile, slot):
        def body(r, _):
            row_copy(tile, slot, r).start()
            return 0
        lax.fori_loop(0, tm, body, 0, unroll=8)

    @pl.when(i == 0)
    def _():
        issue_tile(0, 0)

    @pl.when(i + 1 < n)
    def _():
        issue_tile(i + 1, (i + 1) % 2)

    slot = i % 2

    def drain(r, _):
        row_copy(i, slot, r).wait()
        return 0

    lax.fori_loop(0, tm, drain, 0, unroll=8)

    lo, hi = _unpack_rows(_load_slabs(buf.at[slot], tm))
    x = x1_ref[...] + mod_ref[0, 5:6, :] * jnp.concatenate([lo, hi], axis=-1)
    ms = jnp.mean(x * x, axis=-1, keepdims=True)
    o_ref[...] = x * lax.rsqrt(ms + NORM_EPS) * g_ref[...]


def _combine(dest, y_packed, x1, mod, g_f, seq):
    t, d = x1.shape
    tm = GATHER_TILE
    tiles_per_batch = seq // tm
    grid_spec = pltpu.PrefetchScalarGridSpec(
        num_scalar_prefetch=1,
        grid=(t // tm,),
        in_specs=[pl.BlockSpec(memory_space=pl.ANY),
                  pl.BlockSpec((tm, d), lambda i, ds: (i, 0)),
                  pl.BlockSpec((1, 6, d), lambda i, ds: (i // tiles_per_batch, 0, 0)),
                  pl.BlockSpec((1, d), lambda i, ds: (0, 0))],
        out_specs=pl.BlockSpec((tm, d), lambda i, ds: (i, 0)),
        scratch_shapes=[pltpu.VMEM((2, ROW_CHUNKS * tm, LANES), i32),
                        pltpu.SemaphoreType.DMA((2,))],
    )
    return pl.pallas_call(
        functools.partial(_combine_kernel, tm=tm),
        grid_spec=grid_spec,
        out_shape=jax.ShapeDtypeStruct((t, d), f32),
        compiler_params=_cparams("arbitrary"),
        name="moe_combine",
    )(dest, y_packed.reshape(-1, ROW_CHUNKS, LANES), x1, mod, g_f)


def _route_tables(route, cnt, tm):
    n_tiles = cnt.shape[0]
    t = route.shape[1]
    bk = MOE_BLOCK
    n_blocks = t // bk + N_BINS
    ti = jnp.arange(n_tiles)
    bi = jnp.arange(N_BINS)
    tile_base = jnp.sum(jnp.where((ti[:, None] > ti[None, :])[:, :, None], cnt[None], 0), axis=1)
    total = jnp.sum(cnt, axis=0)
    padded = (total + bk - 1) // bk * bk
    pends = jnp.sum(jnp.where(bi[None, :] <= bi[:, None], padded[None, :], 0), axis=1)
    base = (pends - padded)[None, :] + tile_base
    base_tok = jnp.broadcast_to(base[:, None, :], (n_tiles, tm, N_BINS)).reshape(t, N_BINS)
    dest = (jnp.sum(jnp.where(bi[None, :] == route[0][:, None], base_tok, 0), axis=1) + route[1]).astype(i32)
    block_start = jnp.arange(n_blocks, dtype=i32) * bk
    block_bin = jnp.clip(jnp.sum(pends[None, :] <= block_start[:, None], axis=1), 0, N_BINS - 1)
    pairs = [(lo, hi) for lo in range(EXPERTS_PER_GROUP) for hi in range(lo + 1, EXPERTS_PER_GROUP)]
    lo_of = [min(b // N_PAIRS, N_GROUPS - 1) * EXPERTS_PER_GROUP + pairs[b % N_PAIRS][0] for b in range(N_BINS)]
    hi_of = [min(b // N_PAIRS, N_GROUPS - 1) * EXPERTS_PER_GROUP + pairs[b % N_PAIRS][1] for b in range(N_BINS)]
    pick = bi[None, :] == block_bin[:, None]
    block_lo = jnp.sum(jnp.where(pick, jnp.asarray(lo_of, i32)[None, :], 0), axis=1).astype(i32)
    block_hi = jnp.sum(jnp.where(pick, jnp.asarray(hi_of, i32)[None, :], 0), axis=1).astype(i32)
    n_used = (pends[-1] // bk).astype(i32).reshape(1)
    return dest, block_lo, block_hi, n_used, n_blocks * bk


def _layer(x2, c, seq, l, ada_w, ada_b, norm1_g, w_in, b_f, lam_q1, lam_k1, lam_q2, lam_k2,
           subln_g, w_o, norm2_g, w_rg, b_rg, w_re, b_re, w_gate, w_up, w_down):
    t, d = x2.shape
    bsz = t // seq
    lam_init = 0.8 - 0.6 * math.exp(-0.3 * l)
    mod = _ada(c, ada_w, ada_b).reshape(bsz, 6, d)

    z0 = 3 * FOX_WIDTH
    w_main = jnp.concatenate([w_in[:, :z0], w_in[:, z0 + FOX_HEADS:]], axis=1).astype(bf16)
    w_fz = jnp.pad(w_in[:, z0:z0 + FOX_HEADS], ((0, 0), (0, LANES - FOX_HEADS))).astype(bf16)
    b_fz = jnp.pad(b_f, (0, LANES - FOX_HEADS)).reshape(1, LANES)
    proj, cum = _inproj(x2, mod, norm1_g.reshape(1, d), w_main, w_fz, b_fz, seq)
    proj3 = proj.reshape(bsz, seq, -1)

    fox = _fox_attention(proj3, cum.reshape(bsz, seq, LANES))
    lam_vecs = jnp.pad(jnp.stack([lam_q1, lam_k1, lam_q2, lam_k2]).astype(f32),
                       ((0, 4), (0, LANES - HEAD_DIM)))
    diff = _diff_attention(proj3, lam_vecs, subln_g.reshape(1, LANES), lam_init)

    w_r = jnp.concatenate([w_rg.T, jnp.zeros((8 - N_GROUPS, d), f32), w_re.T], axis=0)
    wr_hi = w_r.astype(bf16)
    wr_lo = (w_r - wr_hi.astype(f32)).astype(bf16)
    b_r = jnp.concatenate([b_rg, jnp.zeros((8 - N_GROUPS,), f32), b_re]).reshape(ROUTER_ROWS, 1)
    x1, h_rows, route, cnt = _outproj(
        fox.reshape(t, FOX_WIDTH), diff.reshape(t, DIFF_WIDTH), x2, mod, norm2_g.reshape(1, d),
        w_o.astype(bf16), wr_hi, wr_lo, b_r, seq)

    dest, block_lo, block_hi, n_used, n_slots = _route_tables(route, cnt[:, :, 0], ROW_TILE)
    xs = _dispatch(dest, h_rows, n_slots)
    y_packed = _experts(block_lo, block_hi, n_used, xs, w_gate, w_up, w_down)
    return dest, y_packed, x1, mod


def kernel(x, c, ada_w, ada_b, norm1_g, w_in, b_f, lam_q1, lam_k1, lam_q2, lam_k2, subln_g, w_o,
           norm2_g, w_rg, b_rg, w_re, b_re, w_gate, w_up, w_down, norm_f_g):
    bsz, seq, d = x.shape
    depth = ada_w.shape[0]
    assert depth == 1 and d == D_MODEL and seq % ROW_TILE == 0 and seq % ATT_WIDE == 0
    x2 = x.reshape(bsz * seq, d)
    dest, y_packed, x1, mod = _layer(
        x2, c, seq, 0, ada_w[0], ada_b[0], norm1_g[0], w_in[0], b_f[0], lam_q1[0], lam_k1[0],
        lam_q2[0], lam_k2[0], subln_g[0], w_o[0], norm2_g[0], w_rg[0], b_rg[0], w_re[0], b_re[0],
        w_gate[0], w_up[0], w_down[0])
    out = _combine(dest, y_packed, x1, mod, norm_f_g.reshape(1, d), seq)
    return out.reshape(bsz, seq, d)
```

```python
import functools
import math

import jax
import jax.numpy as jnp
from jax import lax
from jax.experimental import pallas as pl
from jax.experimental.pallas import tpu as pltpu

f32 = jnp.float32
bf16 = jnp.bfloat16
i32 = jnp.int32

D_MODEL = 1024
HEAD_DIM = 64
FOX_HEADS = 8
FOX_WIDTH = FOX_HEADS * HEAD_DIM
DIFF_HEADS = 4
DIFF_QK_WIDTH = DIFF_HEADS * 2 * HEAD_DIM
DIFF_WIDTH = DIFF_HEADS * 2 * HEAD_DIM
CHUNK = 64
N_GROUPS = 4
EXPERTS_PER_GROUP = 8
N_EXPERTS = N_GROUPS * EXPERTS_PER_GROUP
D_EXPERT = 512
NORM_EPS = 1e-6
SUBLN_EPS = 1e-5

LANES = 128
LOG2E = 1.4426950408889634
Q_SCALE = HEAD_DIM ** -0.5 * LOG2E
NEG = -1e30
HALF = D_MODEL // 2
ROW_CHUNKS = HALF // LANES

ROW_TILE = 512
ATT_TQ = 1024
ATT_WIDE = 1024
ATT_COLS = 256
ATT_KEYS = 256
N_PAIRS = EXPERTS_PER_GROUP * (EXPERTS_PER_GROUP - 1) // 2
N_BINS = 128
MOE_BLOCK = 128
X_CHUNKS = 8
GATHER_TILE = 256
VMEM_LIMIT = 48 * 1024 * 1024


def _cparams(*sem):
    return pltpu.CompilerParams(dimension_semantics=sem, vmem_limit_bytes=VMEM_LIMIT)


def _split3(c):
    hi = c.astype(bf16).astype(f32)
    r = c - hi
    mid = r.astype(bf16).astype(f32)
    lo = r - mid
    return hi, mid, lo


def _pack_rows(y):
    a = pltpu.bitcast(y[:, :HALF].astype(bf16).astype(f32), i32)
    b = pltpu.bitcast(y[:, HALF:].astype(bf16).astype(f32), i32)
    return lax.shift_right_logical(a, 16) | (b & jnp.int32(-65536))


def _store_slabs(ref, packed, chunks=ROW_CHUNKS):
    n = packed.shape[0]
    for j in range(ROW_CHUNKS):
        ref[pl.ds(j, n, stride=chunks), :] = packed[:, j * LANES:(j + 1) * LANES]


def _load_slabs(ref, n, chunks=ROW_CHUNKS):
    return jnp.concatenate([ref[pl.ds(j, n, stride=chunks), :] for j in range(ROW_CHUNKS)], axis=-1)


def _unpack_rows(w):
    lo = pltpu.bitcast(lax.shift_left(w, 16), f32)
    hi = pltpu.bitcast(w & jnp.int32(-65536), f32)
    return lo, hi


def _ada_kernel(c_ref, w_ref, b_ref, o_ref):
    c = c_ref[...]
    w = w_ref[...]
    c_hi = c.astype(bf16)
    c_lo = (c - c_hi.astype(f32)).astype(bf16)
    w_hi = w.astype(bf16)
    w_lo = (w - w_hi.astype(f32)).astype(bf16)
    acc = jnp.dot(c_hi, w_hi, preferred_element_type=f32)
    acc += jnp.dot(c_hi, w_lo, preferred_element_type=f32)
    acc += jnp.dot(c_lo, w_hi, preferred_element_type=f32)
    o_ref[...] = acc + b_ref[...]


def _ada(c, w, b):
    bsz, d = c.shape
    n = w.shape[1]
    tn = 1024
    return pl.pallas_call(
        _ada_kernel,
        grid=(n // tn,),
        in_specs=[pl.BlockSpec((bsz, d), lambda j: (0, 0)),
                  pl.BlockSpec((d, tn), lambda j: (0, j)),
                  pl.BlockSpec((1, tn), lambda j: (0, j))],
        out_specs=pl.BlockSpec((bsz, tn), lambda j: (0, j)),
        out_shape=jax.ShapeDtypeStruct((bsz, n), f32),
        compiler_params=_cparams("parallel"),
        name="ada_mod",
    )(c, w, b.reshape(1, n))


def _inproj_kernel(x_ref, mod_ref, g_ref, wm_ref, wz_ref, bf_ref, tril_ref,
                   proj_ref, cum_ref, carry_ref, *, tiles_per_batch):
    i = pl.program_id(0)
    x = x_ref[...]
    ms = jnp.mean(x * x, axis=-1, keepdims=True)
    y = x * lax.rsqrt(ms + NORM_EPS) * g_ref[...]
    h = (y * (1.0 + mod_ref[0, 1:2, :]) + mod_ref[0, 0:1, :]).astype(bf16)
    n_chunks = proj_ref.shape[1] // 512
    for j in range(n_chunks):
        acc = jnp.dot(h, wm_ref[:, j * 512:(j + 1) * 512], preferred_element_type=f32)
        if j in (0, 3):
            acc = acc * Q_SCALE
        proj_ref[:, j * 512:(j + 1) * 512] = acc.astype(bf16)
    fz = jnp.dot(h, wz_ref[...], preferred_element_type=f32) + bf_ref[...]
    ls = (jnp.minimum(fz, 0.0) - jnp.log(1.0 + jnp.exp(-jnp.abs(fz)))) * LOG2E
    hi, mid, lo = _split3(ls)
    tril = tril_ref[...]
    local = jnp.dot(tril, hi.astype(bf16), preferred_element_type=f32)
    local += jnp.dot(tril, mid.astype(bf16), preferred_element_type=f32)
    local += jnp.dot(tril, lo.astype(bf16), preferred_element_type=f32)

    @pl.when(i % tiles_per_batch == 0)
    def _():
        carry_ref[...] = jnp.zeros_like(carry_ref)

    cum = local + carry_ref[0:1, :]
    cum_ref[...] = cum
    tm = x.shape[0]
    carry_ref[0:1, :] = cum[tm - 1:tm, :]


def _inproj(x2, mod, g1, w_main, w_fz, b_fz, seq):
    t, d = x2.shape
    tm = ROW_TILE
    tiles_per_batch = seq // tm
    n_main = w_main.shape[1]
    tril = (jnp.arange(tm)[:, None] >= jnp.arange(tm)[None, :]).astype(bf16)
    return pl.pallas_call(
        functools.partial(_inproj_kernel, tiles_per_batch=tiles_per_batch),
        grid=(t // tm,),
        in_specs=[pl.BlockSpec((tm, d), lambda i: (i, 0)),
                  pl.BlockSpec((1, 6, d), lambda i: (i // tiles_per_batch, 0, 0)),
                  pl.BlockSpec((1, d), lambda i: (0, 0)),
                  pl.BlockSpec((d, n_main), lambda i: (0, 0)),
                  pl.BlockSpec((d, LANES), lambda i: (0, 0)),
                  pl.BlockSpec((1, LANES), lambda i: (0, 0)),
                  pl.BlockSpec((tm, tm), lambda i: (0, 0))],
        out_specs=[pl.BlockSpec((tm, n_main), lambda i: (i, 0)),
                   pl.BlockSpec((tm, LANES), lambda i: (i, 0))],
        out_shape=[jax.ShapeDtypeStruct((t, n_main), bf16),
                   jax.ShapeDtypeStruct((t, LANES), f32)],
        scratch_shapes=[pltpu.VMEM((8, LANES), f32)],
        compiler_params=_cparams("arbitrary"),
        name="inproj",
    )(x2, mod, g1, w_main, w_fz, b_fz, tril)


def _aug(data, lane, low_map, first, last):
    base = 64 if low_map else 0
    out = jnp.zeros_like(data)
    for n, val in enumerate(tuple(first) + tuple(last)):
        out = jnp.where(lane == base + n, val, out)
    keep = (lane < 64) if low_map else (lane >= 64)
    return jnp.where(keep, data, out)


_ONES3 = (1.0, 1.0, 1.0)


def _q_aug(q, lane, low_map, c):
    return _aug(q, lane, low_map, _split3(c), _ONES3).astype(bf16)


def _k_aug(k, lane, low_map, c):
    hi, mid, lo = _split3(c)
    return _aug(k, lane, low_map, _ONES3, (-hi, -mid, -lo)).astype(bf16)


ONES_ROWS = 16


def _vt_aug(vt):
    row = lax.broadcasted_iota(i32, (ONES_ROWS, vt.shape[1]), 0)
    extra = jnp.where(row == 0, 1.0, 0.0).astype(vt.dtype)
    return jnp.concatenate([vt, extra], axis=0).astype(bf16)


def _flash_scratch(tq, acc_rows):
    return ([pltpu.VMEM((ATT_WIDE, tq), f32)] * 2
            + [pltpu.VMEM((acc_rows, tq), f32)] * 2 + [pltpu.VMEM((1, tq), f32)] * 4)


def _flash_sweep(make_q, n_q, ka_s, kb_s, vta_s, vtb_s, mask_ref, tq, finish, scratch):
    s_a, s_b, acc_a, acc_b, m_a, m_b, tmax_a, tmax_b = scratch
    nt = (((1,), (1,)), ((), ()))
    wide = ATT_WIDE
    assert tq == wide
    s_refs, tmax_refs = (s_a, s_b), (tmax_a, tmax_b)
    k_refs, vt_refs, m_refs, acc_refs = (ka_s, kb_s), (vta_s, vtb_s), (m_a, m_b), (acc_a, acc_b)
    items = [(x, c * ATT_COLS) for x in range(2) for c in range(tq // ATT_COLS)]
    steps = [(qi, j) for qi in range(n_q) for j in range(qi + 1)]
    q_cache = {}

    def q_of(qi):
        if qi not in q_cache:
            q_cache[qi] = make_q(qi)
        return q_cache[qi]

    def key_chunks(width):
        return [(k0, ATT_KEYS) for k0 in range(0, width, ATT_KEYS)]

    def scores_to(step, item):
        (qi, j), (x, c0) = step, item
        cols = slice(c0, c0 + ATT_COLS)
        q_cols = q_of(qi)[x][cols, :]
        tile_max = None
        for k0, kn in key_chunks(wide):
            s = lax.dot_general(k_refs[x][j * wide + k0:j * wide + k0 + kn, :], q_cols, nt,
                                preferred_element_type=f32)
            s_refs[x][k0:k0 + kn, cols] = s
            c_max = jnp.max(s, axis=0, keepdims=True)
            tile_max = c_max if tile_max is None else jnp.maximum(tile_max, c_max)
        tmax_refs[x][:, cols] = tile_max

    def update(step, item):
        (qi, j), (x, c0) = step, item
        cols = slice(c0, c0 + ATT_COLS)
        diagonal = j == qi
        width = c0 + ATT_COLS if diagonal else wide

        def chunk(k0, kn):
            s = s_refs[x][k0:k0 + kn, cols]
            return s + mask_ref[k0:k0 + kn, cols] if diagonal else s

        if diagonal:
            tile_max = None
            for k0, kn in key_chunks(width):
                c_max = jnp.max(chunk(k0, kn), axis=0, keepdims=True)
                tile_max = c_max if tile_max is None else jnp.maximum(tile_max, c_max)
        else:
            tile_max = tmax_refs[x][:, cols]
        m = m_refs[x][:, cols]
        m_new = jnp.maximum(m, tile_max)
        pv = None
        for k0, kn in key_chunks(width):
            p = jnp.exp2(chunk(k0, kn) - m_new).astype(bf16)
            part = jnp.dot(vt_refs[x][:, j * wide + k0:j * wide + k0 + kn], p,
                           preferred_element_type=f32)
            pv = part if pv is None else pv + part
        acc_refs[x][:, cols] = jnp.exp2(m - m_new) * acc_refs[x][:, cols] + pv
        m_refs[x][:, cols] = m_new

    def reset():
        for x in range(2):
            m_refs[x][...] = jnp.full(m_refs[x].shape, NEG, f32)
            acc_refs[x][...] = jnp.zeros(acc_refs[x].shape, f32)

    reset()
    for item in items:
        scores_to(steps[0], item)
    for n, step in enumerate(steps):
        nxt = steps[n + 1] if n + 1 < len(steps) else None
        for item in items:
            update(step, item)
            if nxt is not None:
                scores_to(nxt, item)
        if step[1] == step[0]:
            finish(step[0], acc_a[...], acc_b[...])
            if nxt is not None:
                reset()


def _fox_kernel(q_ref, k_ref, v_ref, cum_ref, mask_ref, o_ref, ka_s, kb_s, vta_s, vtb_s,
                *flash_scratch, seq, tq):
    hp = pl.program_id(1)
    lane = lax.broadcasted_iota(i32, (1, LANES), 1)

    def head_cums(cm):
        c_a = jnp.sum(jnp.where(lane == 2 * hp, cm, 0.0), axis=-1, keepdims=True)
        c_b = jnp.sum(jnp.where(lane == 2 * hp + 1, cm, 0.0), axis=-1, keepdims=True)
        return c_a, c_b

    for c in range(seq // 512):
        rows = pl.ds(c * 512, 512)
        kk = k_ref[0, rows, :].astype(f32)
        c_a, c_b = head_cums(cum_ref[0, rows, :])
        ka_s[rows, :] = _k_aug(kk, lane, True, c_a)
        kb_s[rows, :] = _k_aug(kk, lane, False, c_b)
        vt = v_ref[0, rows, :].astype(f32).T
        vta_s[:, rows] = _vt_aug(vt[:HEAD_DIM])
        vtb_s[:, rows] = _vt_aug(vt[HEAD_DIM:])

    def make_q(qi):
        rows = pl.ds(qi * tq, tq)
        q = q_ref[0, rows, :].astype(f32)
        c_a, c_b = head_cums(cum_ref[0, rows, :])
        return _q_aug(q, lane, True, c_a), _q_aug(q, lane, False, c_b)

    def finish(qi, acc_a, acc_b):
        o_a = acc_a[:HEAD_DIM] / acc_a[HEAD_DIM:HEAD_DIM + 1]
        o_b = acc_b[:HEAD_DIM] / acc_b[HEAD_DIM:HEAD_DIM + 1]
        o_ref[0, pl.ds(qi * tq, tq), :] = jnp.concatenate([o_a, o_b], axis=0).T.astype(bf16)

    _flash_sweep(make_q, seq // tq, ka_s, kb_s, vta_s, vtb_s, mask_ref, tq, finish, flash_scratch)


def _tail_mask(diag):
    diag_t = jnp.swapaxes(diag, -1, -2)
    pad = [(0, 0)] * (diag.ndim - 2) + [(ATT_WIDE - diag.shape[-1], 0), (0, 0)]
    return jnp.pad(diag_t, pad)


def _fox_attention(proj3, cum3):
    bsz, seq, _ = proj3.shape
    tq, tk = ATT_TQ, ATT_WIDE
    n_pairs = FOX_HEADS // 2
    r = jnp.arange(tq)
    mask = _tail_mask(jnp.where(r[:, None] >= r[None, :], 0.0, NEG).astype(f32))
    return pl.pallas_call(
        functools.partial(_fox_kernel, seq=seq, tq=tq),
        grid=(bsz, n_pairs),
        in_specs=[pl.BlockSpec((1, seq, LANES), lambda b, h: (b, 0, h)),
                  pl.BlockSpec((1, seq, LANES), lambda b, h: (b, 0, 4 + h)),
                  pl.BlockSpec((1, seq, LANES), lambda b, h: (b, 0, 8 + h)),
                  pl.BlockSpec((1, seq, LANES), lambda b, h: (b, 0, 0)),
                  pl.BlockSpec((tk, tq), lambda b, h: (0, 0))],
        out_specs=pl.BlockSpec((1, seq, LANES), lambda b, h: (b, 0, h)),
        out_shape=jax.ShapeDtypeStruct((bsz, seq, FOX_WIDTH), bf16),
        scratch_shapes=[pltpu.VMEM((seq, LANES), bf16), pltpu.VMEM((seq, LANES), bf16),
                        pltpu.VMEM((HEAD_DIM + ONES_ROWS, seq), bf16),
                        pltpu.VMEM((HEAD_DIM + ONES_ROWS, seq), bf16)]
        + _flash_scratch(tq, HEAD_DIM + ONES_ROWS),
        compiler_params=_cparams("parallel", "parallel"),
        name="fox_attn",
    )(proj3, proj3, proj3, cum3, mask)


def _diff_kernel(slope_ref, q_ref, k_ref, v_ref, mask_ref, lamv_ref, g_ref, o_ref,
                 ka_s, kb_s, vt_s, *flash_scratch, seq, tq, lam_init):
    h = pl.program_id(1)
    lane = lax.broadcasted_iota(i32, (1, LANES), 1)
    slope = slope_ref[h]

    def pos_bias(start, n):
        pos = (start + lax.broadcasted_iota(i32, (n, 1), 0)).astype(f32)
        return -(slope * pos)

    for c in range(seq // 512):
        rows = pl.ds(c * 512, 512)
        kk = k_ref[0, rows, :].astype(f32)
        cb = pos_bias(c * 512, 512)
        ka_s[rows, :] = _k_aug(kk, lane, True, cb)
        kb_s[rows, :] = _k_aug(kk, lane, False, cb)
        vt_s[:, rows] = _vt_aug(v_ref[0, rows, :].astype(f32).T)

    def make_q(qi):
        q = q_ref[0, pl.ds(qi * tq, tq), :].astype(f32)
        cq = pos_bias(qi * tq, tq)
        return _q_aug(q, lane, True, cq), _q_aug(q, lane, False, cq)

    def finish(qi, acc_a, acc_b):
        lv = lamv_ref[...]
        s1 = jnp.sum(lv[0:1, :] * lv[1:2, :], axis=-1, keepdims=True)
        s2 = jnp.sum(lv[2:3, :] * lv[3:4, :], axis=-1, keepdims=True)
        lam = jnp.exp(s1) - jnp.exp(s2) + lam_init
        o_a = acc_a[:LANES] / acc_a[LANES:LANES + 1]
        o_b = acc_b[:LANES] / acc_b[LANES:LANES + 1]
        d = (o_a - lam * o_b).T
        y = d * lax.rsqrt(jnp.mean(d * d, axis=-1, keepdims=True) + SUBLN_EPS) * g_ref[...]
        o_ref[0, pl.ds(qi * tq, tq), :] = (y * (1.0 - lam_init)).astype(bf16)

    _flash_sweep(make_q, seq // tq, ka_s, kb_s, vt_s, vt_s, mask_ref.at[0], tq, finish, flash_scratch)


def _diff_attention(proj3, lam_vecs, subln_g, lam_init):
    bsz, seq, _ = proj3.shape
    tq, tk = ATT_TQ, ATT_WIDE
    slopes = jnp.asarray([2.0 ** (-8.0 * (i + 1) / DIFF_HEADS) for i in range(DIFF_HEADS)], f32) * LOG2E
    r = jnp.arange(tq)
    tq_i, tk_i = r[:, None], r[None, :]
    chunk_ok = (tq_i // CHUNK) >= (tk_i // CHUNK)
    ahead = jnp.maximum(tk_i - tq_i, 0).astype(f32)
    mask = _tail_mask(jnp.where(chunk_ok[None], -2.0 * slopes[:, None, None] * ahead[None], NEG).astype(f32))
    grid_spec = pltpu.PrefetchScalarGridSpec(
        num_scalar_prefetch=1,
        grid=(bsz, DIFF_HEADS),
        in_specs=[pl.BlockSpec((1, seq, LANES), lambda b, h, s: (b, 0, 12 + h)),
                  pl.BlockSpec((1, seq, LANES), lambda b, h, s: (b, 0, 16 + h)),
                  pl.BlockSpec((1, seq, LANES), lambda b, h, s: (b, 0, 20 + h)),
                  pl.BlockSpec((1, tk, tq), lambda b, h, s: (h, 0, 0)),
                  pl.BlockSpec((8, LANES), lambda b, h, s: (0, 0)),
                  pl.BlockSpec((1, LANES), lambda b, h, s: (0, 0))],
        out_specs=pl.BlockSpec((1, seq, LANES), lambda b, h, s: (b, 0, h)),
        scratch_shapes=[pltpu.VMEM((seq, LANES), bf16), pltpu.VMEM((seq, LANES), bf16),
                        pltpu.VMEM((LANES + ONES_ROWS, seq), bf16)]
        + _flash_scratch(tq, LANES + ONES_ROWS),
    )
    return pl.pallas_call(
        functools.partial(_diff_kernel, seq=seq, tq=tq, lam_init=lam_init),
        grid_spec=grid_spec,
        out_shape=jax.ShapeDtypeStruct((bsz, seq, DIFF_WIDTH), bf16),
        compiler_params=_cparams("parallel", "parallel"),
        name="diff_attn",
    )(slopes, proj3, proj3, proj3, mask, lam_vecs, subln_g)


ROUTER_ROWS = 8 + N_EXPERTS


def _outproj_kernel(fox_ref, diff_ref, x_ref, mod_ref, g_ref, wo_ref, wr_hi_ref, wr_lo_ref,
                    br_ref, triu_ref, x1_ref, hp_ref, route_ref, cnt_ref):
    tm = x_ref.shape[0]
    y = jnp.dot(fox_ref[...], wo_ref[0:FOX_WIDTH, :], preferred_element_type=f32)
    y += jnp.dot(diff_ref[...], wo_ref[FOX_WIDTH:, :], preferred_element_type=f32)
    x1 = x_ref[...] + mod_ref[0, 2:3, :] * y
    x1_ref[...] = x1
    ms = jnp.mean(x1 * x1, axis=-1, keepdims=True)
    h2 = (x1 * lax.rsqrt(ms + NORM_EPS) * g_ref[...]) * (1.0 + mod_ref[0, 4:5, :]) + mod_ref[0, 3:4, :]
    _store_slabs(hp_ref, _pack_rows(h2), X_CHUNKS)

    nt = (((1,), (1,)), ((), ()))
    h_hi = h2.astype(bf16)
    h_lo = (h2 - h_hi.astype(f32)).astype(bf16)
    logits = lax.dot_general(wr_hi_ref[...], h_hi, nt, preferred_element_type=f32)
    logits += lax.dot_general(wr_lo_ref[...], h_hi, nt, preferred_element_type=f32)
    logits += lax.dot_general(wr_hi_ref[...], h_lo, nt, preferred_element_type=f32)
    logits = logits + br_ref[...]

    row8 = lax.broadcasted_iota(i32, (8, tm), 0)
    gl = jnp.where(row8 < N_GROUPS, logits[0:8, :], NEG)
    gmax = jnp.max(gl, axis=0, keepdims=True)
    grp = jnp.min(jnp.where(gl == gmax, row8, 8), axis=0, keepdims=True)
    p_g = 1.0 / jnp.sum(jnp.exp(gl - gmax), axis=0, keepdims=True)
    sel = logits[8:16, :]
    for g in range(1, N_GROUPS):
        sel = jnp.where(grp == g, logits[8 + 8 * g:16 + 8 * g, :], sel)
    v1 = jnp.max(sel, axis=0, keepdims=True)
    i1 = jnp.min(jnp.where(sel == v1, row8, 8), axis=0, keepdims=True)
    sel2 = jnp.where(row8 == i1, -jnp.inf, sel)
    v2 = jnp.max(sel2, axis=0, keepdims=True)
    i2 = jnp.min(jnp.where(sel2 == v2, row8, 8), axis=0, keepdims=True)
    e21 = jnp.exp(v2 - v1)
    w1 = p_g / (1.0 + e21)
    w2 = w1 * e21
    lo = jnp.minimum(i1, i2)
    hi = jnp.maximum(i1, i2)
    w_lo = jnp.where(i1 < i2, w1, w2)
    w_hi = jnp.where(i1 < i2, w2, w1)
    pair = lax.shift_right_logical(lo * (2 * EXPERTS_PER_GROUP - 1 - lo), 1) + hi - lo - 1
    bins = grp * N_PAIRS + pair

    row_b = lax.broadcasted_iota(i32, (N_BINS, tm), 0)
    hit = row_b == bins
    onehot = jnp.where(hit, 1.0, 0.0)
    prefix = jnp.dot(onehot.astype(bf16), triu_ref[...], preferred_element_type=f32)
    rank = jnp.sum(jnp.where(hit, prefix, 0.0), axis=0, keepdims=True).astype(i32)
    route_ref[...] = jnp.where(row8 == 0, bins, jnp.where(row8 == 1, rank, 0))
    cnt = jnp.sum(onehot, axis=1, keepdims=True)
    cnt_ref[0] = jnp.broadcast_to(cnt, (N_BINS, LANES)).astype(i32)
    w_rows = jnp.where(row_b == 0, w_lo, jnp.where(row_b == 1, w_hi, 0.0))
    hp_ref[pl.ds(ROW_CHUNKS, tm, stride=X_CHUNKS), :] = pltpu.bitcast(w_rows.T, i32)
    for j in range(ROW_CHUNKS + 1, X_CHUNKS):
        hp_ref[pl.ds(j, tm, stride=X_CHUNKS), :] = jnp.zeros((tm, LANES), i32)


def _outproj(fox2, diff2, x2, mod, g2, w_o, wr_hi, wr_lo, b_r, seq):
    t, d = x2.shape
    tm = ROW_TILE
    tiles_per_batch = seq // tm
    n_tiles = t // tm
    triu = (jnp.arange(tm)[:, None] < jnp.arange(tm)[None, :]).astype(bf16)
    row = lambda i: (i, 0)
    const = lambda i: (0, 0)
    return pl.pallas_call(
        _outproj_kernel,
        grid=(n_tiles,),
        in_specs=[pl.BlockSpec((tm, FOX_WIDTH), row),
                  pl.BlockSpec((tm, DIFF_WIDTH), row),
                  pl.BlockSpec((tm, d), row),
                  pl.BlockSpec((1, 6, d), lambda i: (i // tiles_per_batch, 0, 0)),
                  pl.BlockSpec((1, d), const),
                  pl.BlockSpec((d, d), const),
                  pl.BlockSpec((ROUTER_ROWS, d), const),
                  pl.BlockSpec((ROUTER_ROWS, d), const),
                  pl.BlockSpec((ROUTER_ROWS, 1), const),
                  pl.BlockSpec((tm, tm), const)],
        out_specs=[pl.BlockSpec((tm, d), row),
                   pl.BlockSpec((X_CHUNKS * tm, LANES), row),
                   pl.BlockSpec((8, tm), lambda i: (0, i)),
                   pl.BlockSpec((1, N_BINS, LANES), lambda i: (i, 0, 0))],
        out_shape=[jax.ShapeDtypeStruct((t, d), f32),
                   jax.ShapeDtypeStruct((X_CHUNKS * t, LANES), i32),
                   jax.ShapeDtypeStruct((8, t), i32),
                   jax.ShapeDtypeStruct((n_tiles, N_BINS, LANES), i32)],
        compiler_params=_cparams("parallel"),
        name="outproj_router",
    )(fox2, diff2, x2, mod, g2, w_o, wr_hi, wr_lo, b_r, triu)


def _dispatch_kernel(dest_ref, h_ref, xs_in_ref, xs_ref, sem, *, tm):
    del xs_in_ref
    base = pl.program_id(0) * tm

    def row_copy(r):
        src = h_ref.at[pl.ds(pl.multiple_of(X_CHUNKS * r, X_CHUNKS), X_CHUNKS), :]
        return pltpu.make_async_copy(src, xs_ref.at[dest_ref[base + r]], sem)

    def issue(r, _):
        row_copy(r).start()
        return 0

    lax.fori_loop(0, tm, issue, 0, unroll=8)

    def drain(r, _):
        row_copy(r).wait()
        return 0

    lax.fori_loop(0, tm, drain, 0, unroll=8)


def _dispatch(dest, h_rows, n_slots):
    t = h_rows.shape[0] // X_CHUNKS
    tm = GATHER_TILE
    xs0 = jnp.zeros((n_slots, X_CHUNKS, LANES), i32)
    grid_spec = pltpu.PrefetchScalarGridSpec(
        num_scalar_prefetch=1,
        grid=(t // tm,),
        in_specs=[pl.BlockSpec((X_CHUNKS * tm, LANES), lambda i, d: (i, 0)),
                  pl.BlockSpec(memory_space=pl.ANY)],
        out_specs=pl.BlockSpec(memory_space=pl.ANY),
        scratch_shapes=[pltpu.SemaphoreType.DMA(())],
    )
    return pl.pallas_call(
        functools.partial(_dispatch_kernel, tm=tm),
        grid_spec=grid_spec,
        out_shape=jax.ShapeDtypeStruct((n_slots, X_CHUNKS, LANES), i32),
        input_output_aliases={2: 0},
        compiler_params=_cparams("arbitrary"),
        name="moe_dispatch",
    )(dest, h_rows, xs0)


def _experts_kernel(lo_ref, hi_ref, nu_ref, xs_ref, wgl_ref, wul_ref, wdl_ref, wgh_ref, wuh_ref,
                    wdh_ref, y_ref):
    del lo_ref, hi_ref
    bk = MOE_BLOCK
    live = pl.program_id(0) < nu_ref[0]

    @pl.when(live)
    def _():
        lo, hi = _unpack_rows(_load_slabs(xs_ref, bk, X_CHUNKS))
        xb = jnp.concatenate([lo.astype(bf16), hi.astype(bf16)], axis=-1)
        wv = pltpu.bitcast(xs_ref[pl.ds(ROW_CHUNKS, bk, stride=X_CHUNKS), :], f32)

        def mlp(wg_ref, wu_ref, wd_ref):
            g = jnp.dot(xb, wg_ref[0], preferred_element_type=f32)
            u = jnp.dot(xb, wu_ref[0], preferred_element_type=f32)
            hid = (g * (1.0 / (1.0 + jnp.exp(-g)))) * u
            return jnp.dot(hid.astype(bf16), wd_ref[0], preferred_element_type=f32)

        y = wv[:, 0:1] * mlp(wgl_ref, wul_ref, wdl_ref) + wv[:, 1:2] * mlp(wgh_ref, wuh_ref, wdh_ref)
        _store_slabs(y_ref, _pack_rows(y))

    @pl.when(jnp.logical_not(live))
    def _():
        y_ref[...] = jnp.zeros_like(y_ref)


def _experts(block_lo, block_hi, n_used, xs, w_gate, w_up, w_down):
    n_slots = xs.shape[0]
    bk = MOE_BLOCK
    up_spec = lambda which: pl.BlockSpec((1, D_MODEL, D_EXPERT),
                                         lambda i, lo, hi, nu: ((lo, hi)[which][i], 0, 0))
    down_spec = lambda which: pl.BlockSpec((1, D_EXPERT, D_MODEL),
                                           lambda i, lo, hi, nu: ((lo, hi)[which][i], 0, 0))
    grid_spec = pltpu.PrefetchScalarGridSpec(
        num_scalar_prefetch=3,
        grid=(n_slots // bk,),
        in_specs=[pl.BlockSpec((X_CHUNKS * bk, LANES), lambda i, lo, hi, nu: (i, 0)),
                  up_spec(0), up_spec(0), down_spec(0), up_spec(1), up_spec(1), down_spec(1)],
        out_specs=pl.BlockSpec((ROW_CHUNKS * bk, LANES), lambda i, lo, hi, nu: (i, 0)),
    )
    return pl.pallas_call(
        _experts_kernel,
        grid_spec=grid_spec,
        out_shape=jax.ShapeDtypeStruct((ROW_CHUNKS * n_slots, LANES), i32),
        compiler_params=_cparams("arbitrary"),
        name="moe_experts",
    )(block_lo, block_hi, n_used, xs.reshape(-1, LANES), w_gate, w_up, w_down, w_gate, w_up, w_down)


def _combine_kernel(dest_ref, y_ref, x1_ref, mod_ref, g_ref, o_ref, buf, sems, *, tm):
    i = pl.program_id(0)
    n = pl.num_programs(0)

    def row_copy(tile, slot, r):
        dst = buf.at[slot, pl.ds(pl.multiple_of(ROW_CHUNKS * r, ROW_CHUNKS), ROW_CHUNKS), :]
        return pltpu.make_async_copy(y_ref.at[dest_ref[tile * tm + r]], dst, sems.at[slot])

    def issue_tile(tile, slot):
        def body(r, _):
            row_copy(tile, slot, r).start()
            return 0
        lax.fori_loop(0, tm, body, 0, unroll=8)

    @pl.when(i == 0)
    def _():
        issue_tile(0, 0)

    @pl.when(i + 1 < n)
    def _():
        issue_tile(i + 1, (i + 1) % 2)

    slot = i % 2

    def drain(r, _):
        row_copy(i, slot, r).wait()
        return 0

    lax.fori_loop(0, tm, drain, 0, unroll=8)

    lo, hi = _unpack_rows(_load_slabs(buf.at[slot], tm))
    x = x1_ref[...] + mod_ref[0, 5:6, :] * jnp.concatenate([lo, hi], axis=-1)
    ms = jnp.mean(x * x, axis=-1, keepdims=True)
    o_ref[...] = x * lax.rsqrt(ms + NORM_EPS) * g_ref[...]


def _combine(dest, y_packed, x1, mod, g_f, seq):
    t, d = x1.shape
    tm = GATHER_TILE
    tiles_per_batch = seq // tm
    grid_spec = pltpu.PrefetchScalarGridSpec(
        num_scalar_prefetch=1,
        grid=(t // tm,),
        in_specs=[pl.BlockSpec(memory_space=pl.ANY),
                  pl.BlockSpec((tm, d), lambda i, ds: (i, 0)),
                  pl.BlockSpec((1, 6, d), lambda i, ds: (i // tiles_per_batch, 0, 0)),
                  pl.BlockSpec((1, d), lambda i, ds: (0, 0))],
        out_specs=pl.BlockSpec((tm, d), lambda i, ds: (i, 0)),
        scratch_shapes=[pltpu.VMEM((2, ROW_CHUNKS * tm, LANES), i32),
                        pltpu.SemaphoreType.DMA((2,))],
    )
    return pl.pallas_call(
        functools.partial(_combine_kernel, tm=tm),
        grid_spec=grid_spec,
        out_shape=jax.ShapeDtypeStruct((t, d), f32),
        compiler_params=_cparams("arbitrary"),
        name="moe_combine",
    )(dest, y_packed.reshape(-1, ROW_CHUNKS, LANES), x1, mod, g_f)


def _route_tables(route, cnt, tm):
    n_tiles = cnt.shape[0]
    t = route.shape[1]
    bk = MOE_BLOCK
    n_blocks = t // bk + N_BINS
    ti = jnp.arange(n_tiles)
    bi = jnp.arange(N_BINS)
    tile_base = jnp.sum(jnp.where((ti[:, None] > ti[None, :])[:, :, None], cnt[None], 0), axis=1)
    total = jnp.sum(cnt, axis=0)
    padded = (total + bk - 1) // bk * bk
    pends = jnp.sum(jnp.where(bi[None, :] <= bi[:, None], padded[None, :], 0), axis=1)
    base = (pends - padded)[None, :] + tile_base
    base_tok = jnp.broadcast_to(base[:, None, :], (n_tiles, tm, N_BINS)).reshape(t, N_BINS)
    dest = (jnp.sum(jnp.where(bi[None, :] == route[0][:, None], base_tok, 0), axis=1) + route[1]).astype(i32)
    block_start = jnp.arange(n_blocks, dtype=i32) * bk
    block_bin = jnp.clip(jnp.sum(pends[None, :] <= block_start[:, None], axis=1), 0, N_BINS - 1)
    pairs = [(lo, hi) for lo in range(EXPERTS_PER_GROUP) for hi in range(lo + 1, EXPERTS_PER_GROUP)]
    lo_of = [min(b // N_PAIRS, N_GROUPS - 1) * EXPERTS_PER_GROUP + pairs[b % N_PAIRS][0] for b in range(N_BINS)]
    hi_of = [min(b // N_PAIRS, N_GROUPS - 1) * EXPERTS_PER_GROUP + pairs[b % N_PAIRS][1] for b in range(N_BINS)]
    pick = bi[None, :] == block_bin[:, None]
    block_lo = jnp.sum(jnp.where(pick, jnp.asarray(lo_of, i32)[None, :], 0), axis=1).astype(i32)
    block_hi = jnp.sum(jnp.where(pick, jnp.asarray(hi_of, i32)[None, :], 0), axis=1).astype(i32)
    n_used = (pends[-1] // bk).astype(i32).reshape(1)
    return dest, block_lo, block_hi, n_used, n_blocks * bk


def _layer(x2, c, seq, l, ada_w, ada_b, norm1_g, w_in, b_f, lam_q1, lam_k1, lam_q2, lam_k2,
           subln_g, w_o, norm2_g, w_rg, b_rg, w_re, b_re, w_gate, w_up, w_down):
    t, d = x2.shape
    bsz = t // seq
    lam_init = 0.8 - 0.6 * math.exp(-0.3 * l)
    mod = _ada(c, ada_w, ada_b).reshape(bsz, 6, d)

    z0 = 3 * FOX_WIDTH
    w_main = jnp.concatenate([w_in[:, :z0], w_in[:, z0 + FOX_HEADS:]], axis=1).astype(bf16)
    w_fz = jnp.pad(w_in[:, z0:z0 + FOX_HEADS], ((0, 0), (0, LANES - FOX_HEADS))).astype(bf16)
    b_fz = jnp.pad(b_f, (0, LANES - FOX_HEADS)).reshape(1, LANES)
    proj, cum = _inproj(x2, mod, norm1_g.reshape(1, d), w_main, w_fz, b_fz, seq)
    proj3 = proj.reshape(bsz, seq, -1)

    fox = _fox_attention(proj3, cum.reshape(bsz, seq, LANES))
    lam_vecs = jnp.pad(jnp.stack([lam_q1, lam_k1, lam_q2, lam_k2]).astype(f32),
                       ((0, 4), (0, LANES - HEAD_DIM)))
    diff = _diff_attention(proj3, lam_vecs, subln_g.reshape(1, LANES), lam_init)

    w_r = jnp.concatenate([w_rg.T, jnp.zeros((8 - N_GROUPS, d), f32), w_re.T], axis=0)
    wr_hi = w_r.astype(bf16)
    wr_lo = (w_r - wr_hi.astype(f32)).astype(bf16)
    b_r = jnp.concatenate([b_rg, jnp.zeros((8 - N_GROUPS,), f32), b_re]).reshape(ROUTER_ROWS, 1)
    x1, h_rows, route, cnt = _outproj(
        fox.reshape(t, FOX_WIDTH), diff.reshape(t, DIFF_WIDTH), x2, mod, norm2_g.reshape(1, d),
        w_o.astype(bf16), wr_hi, wr_lo, b_r, seq)

    dest, block_lo, block_hi, n_used, n_slots = _route_tables(route, cnt[:, :, 0], ROW_TILE)
    xs = _dispatch(dest, h_rows, n_slots)
    y_packed = _experts(block_lo, block_hi, n_used, xs,
                        w_gate.astype(bf16), w_up.astype(bf16), w_down.astype(bf16))
    return dest, y_packed, x1, mod


def kernel(x, c, ada_w, ada_b, norm1_g, w_in, b_f, lam_q1, lam_k1, lam_q2, lam_k2, subln_g, w_o,
           norm2_g, w_rg, b_rg, w_re, b_re, w_gate, w_up, w_down, norm_f_g):
    bsz, seq, d = x.shape
    depth = ada_w.shape[0]
    assert depth == 1 and d == D_MODEL and seq % ROW_TILE == 0 and seq % ATT_WIDE == 0
    x2 = x.reshape(bsz * seq, d)
    dest, y_packed, x1, mod = _layer(
        x2, c, seq, 0, ada_w[0], ada_b[0], norm1_g[0], w_in[0], b_f[0], lam_q1[0], lam_k1[0],
        lam_q2[0], lam_k2[0], subln_g[0], w_o[0], norm2_g[0], w_rg[0], b_rg[0], w_re[0], b_re[0],
        w_gate[0], w_up[0], w_down[0])
    out = _combine(dest, y_packed, x1, mod, norm_f_g.reshape(1, d), seq)
    return out.reshape(bsz, seq, d)
```

```python
import functools
import math

import jax
import jax.numpy as jnp
from jax import lax
from jax.experimental import pallas as pl
from jax.experimental.pallas import tpu as pltpu

f32 = jnp.float32
bf16 = jnp.bfloat16
i32 = jnp.int32

D_MODEL = 1024
HEAD_DIM = 64
FOX_HEADS = 8
FOX_WIDTH = FOX_HEADS * HEAD_DIM
DIFF_HEADS = 4
DIFF_QK_WIDTH = DIFF_HEADS * 2 * HEAD_DIM
DIFF_WIDTH = DIFF_HEADS * 2 * HEAD_DIM
CHUNK = 64
N_GROUPS = 4
EXPERTS_PER_GROUP = 8
N_EXPERTS = N_GROUPS * EXPERTS_PER_GROUP
D_EXPERT = 512
NORM_EPS = 1e-6
SUBLN_EPS = 1e-5

LANES = 128
LOG2E = 1.4426950408889634
Q_SCALE = HEAD_DIM ** -0.5 * LOG2E
NEG = -1e30
HALF = D_MODEL // 2
ROW_CHUNKS = HALF // LANES

ROW_TILE = 512
ATT_TQ = 1024
ATT_WIDE = 1024
ATT_COLS = 256
ATT_KEYS = 256
MOE_BLOCK = 256
GATHER_TILE = 512
VMEM_LIMIT = 48 * 1024 * 1024


def _cparams(*sem):
    return pltpu.CompilerParams(dimension_semantics=sem, vmem_limit_bytes=VMEM_LIMIT)


def _split3(c):
    hi = c.astype(bf16).astype(f32)
    r = c - hi
    mid = r.astype(bf16).astype(f32)
    lo = r - mid
    return hi, mid, lo


def _pack_rows(y):
    a = pltpu.bitcast(y[:, :HALF].astype(bf16).astype(f32), i32)
    b = pltpu.bitcast(y[:, HALF:].astype(bf16).astype(f32), i32)
    return lax.shift_right_logical(a, 16) | (b & jnp.int32(-65536))


def _store_slabs(ref, packed):
    n = packed.shape[0]
    for j in range(ROW_CHUNKS):
        ref[pl.ds(j, n, stride=ROW_CHUNKS), :] = packed[:, j * LANES:(j + 1) * LANES]


def _load_slabs(ref, n):
    return jnp.concatenate([ref[pl.ds(j, n, stride=ROW_CHUNKS), :] for j in range(ROW_CHUNKS)], axis=-1)


def _unpack_rows(w):
    lo = pltpu.bitcast(lax.shift_left(w, 16), f32)
    hi = pltpu.bitcast(w & jnp.int32(-65536), f32)
    return lo, hi


def _ada_kernel(c_ref, w_ref, b_ref, o_ref):
    c = c_ref[...]
    w = w_ref[...]
    c_hi = c.astype(bf16)
    c_lo = (c - c_hi.astype(f32)).astype(bf16)
    w_hi = w.astype(bf16)
    w_lo = (w - w_hi.astype(f32)).astype(bf16)
    acc = jnp.dot(c_hi, w_hi, preferred_element_type=f32)
    acc += jnp.dot(c_hi, w_lo, preferred_element_type=f32)
    acc += jnp.dot(c_lo, w_hi, preferred_element_type=f32)
    o_ref[...] = acc + b_ref[...]


def _ada(c, w, b):
    bsz, d = c.shape
    n = w.shape[1]
    tn = 1024
    return pl.pallas_call(
        _ada_kernel,
        grid=(n // tn,),
        in_specs=[pl.BlockSpec((bsz, d), lambda j: (0, 0)),
                  pl.BlockSpec((d, tn), lambda j: (0, j)),
                  pl.BlockSpec((1, tn), lambda j: (0, j))],
        out_specs=pl.BlockSpec((bsz, tn), lambda j: (0, j)),
        out_shape=jax.ShapeDtypeStruct((bsz, n), f32),
        compiler_params=_cparams("parallel"),
        name="ada_mod",
    )(c, w, b.reshape(1, n))


def _inproj_kernel(x_ref, mod_ref, g_ref, wm_ref, wz_ref, bf_ref, tril_ref,
                   proj_ref, cum_ref, carry_ref, *, tiles_per_batch):
    i = pl.program_id(0)
    x = x_ref[...]
    ms = jnp.mean(x * x, axis=-1, keepdims=True)
    y = x * lax.rsqrt(ms + NORM_EPS) * g_ref[...]
    h = (y * (1.0 + mod_ref[0, 1:2, :]) + mod_ref[0, 0:1, :]).astype(bf16)
    n_chunks = proj_ref.shape[1] // 512
    for j in range(n_chunks):
        acc = jnp.dot(h, wm_ref[:, j * 512:(j + 1) * 512], preferred_element_type=f32)
        if j in (0, 3):
            acc = acc * Q_SCALE
        proj_ref[:, j * 512:(j + 1) * 512] = acc.astype(bf16)
    fz = jnp.dot(h, wz_ref[...], preferred_element_type=f32) + bf_ref[...]
    ls = (jnp.minimum(fz, 0.0) - jnp.log(1.0 + jnp.exp(-jnp.abs(fz)))) * LOG2E
    lane = lax.broadcasted_iota(i32, (1, LANES), 1)
    hi, mid, lo = _split3(jnp.where(lane < FOX_HEADS, ls, 0.0))
    parts = hi + pltpu.roll(mid, FOX_HEADS, 1) + pltpu.roll(lo, 2 * FOX_HEADS, 1)
    sums = jnp.dot(tril_ref[...], parts.astype(bf16), preferred_element_type=f32)
    local = sums + pltpu.roll(sums, LANES - FOX_HEADS, 1) + pltpu.roll(sums, LANES - 2 * FOX_HEADS, 1)

    @pl.when(i % tiles_per_batch == 0)
    def _():
        carry_ref[...] = jnp.zeros_like(carry_ref)

    cum = local + carry_ref[0:1, :]
    cum_ref[...] = cum
    tm = x.shape[0]
    carry_ref[0:1, :] = cum[tm - 1:tm, :]


def _inproj(x2, mod, g1, w_main, w_fz, b_fz, seq):
    t, d = x2.shape
    tm = ROW_TILE
    tiles_per_batch = seq // tm
    n_main = w_main.shape[1]
    tril = (jnp.arange(tm)[:, None] >= jnp.arange(tm)[None, :]).astype(bf16)
    return pl.pallas_call(
        functools.partial(_inproj_kernel, tiles_per_batch=tiles_per_batch),
        grid=(t // tm,),
        in_specs=[pl.BlockSpec((tm, d), lambda i: (i, 0)),
                  pl.BlockSpec((1, 6, d), lambda i: (i // tiles_per_batch, 0, 0)),
                  pl.BlockSpec((1, d), lambda i: (0, 0)),
                  pl.BlockSpec((d, n_main), lambda i: (0, 0)),
                  pl.BlockSpec((d, LANES), lambda i: (0, 0)),
                  pl.BlockSpec((1, LANES), lambda i: (0, 0)),
                  pl.BlockSpec((tm, tm), lambda i: (0, 0))],
        out_specs=[pl.BlockSpec((tm, n_main), lambda i: (i, 0)),
                   pl.BlockSpec((tm, LANES), lambda i: (i, 0))],
        out_shape=[jax.ShapeDtypeStruct((t, n_main), bf16),
                   jax.ShapeDtypeStruct((t, LANES), f32)],
        scratch_shapes=[pltpu.VMEM((8, LANES), f32)],
        compiler_params=_cparams("arbitrary"),
        name="inproj",
    )(x2, mod, g1, w_main, w_fz, b_fz, tril)


def _aug(data, lane, low_map, first, last):
    base = 64 if low_map else 0
    out = jnp.zeros_like(data)
    for n, val in enumerate(tuple(first) + tuple(last)):
        out = jnp.where(lane == base + n, val, out)
    keep = (lane < 64) if low_map else (lane >= 64)
    return jnp.where(keep, data, out)


_ONES3 = (1.0, 1.0, 1.0)


def _q_aug(q, lane, low_map, c):
    return _aug(q, lane, low_map, _split3(c), _ONES3).astype(bf16)


def _k_aug(k, lane, low_map, c):
    hi, mid, lo = _split3(c)
    return _aug(k, lane, low_map, _ONES3, (-hi, -mid, -lo)).astype(bf16)


ONES_ROWS = 16


def _vt_aug(vt):
    row = lax.broadcasted_iota(i32, (ONES_ROWS, vt.shape[1]), 0)
    extra = jnp.where(row == 0, 1.0, 0.0).astype(vt.dtype)
    return jnp.concatenate([vt, extra], axis=0).astype(bf16)


def _flash_scratch(tq, acc_rows):
    return ([pltpu.VMEM((ATT_WIDE, tq), f32)] * 2
            + [pltpu.VMEM((acc_rows, tq), f32)] * 2 + [pltpu.VMEM((1, tq), f32)] * 4)


def _flash_sweep(make_q, n_q, ka_s, kb_s, vta_s, vtb_s, mask_ref, tq, finish, scratch):
    s_a, s_b, acc_a, acc_b, m_a, m_b, tmax_a, tmax_b = scratch
    nt = (((1,), (1,)), ((), ()))
    wide = ATT_WIDE
    assert tq == wide
    s_refs, tmax_refs = (s_a, s_b), (tmax_a, tmax_b)
    k_refs, vt_refs, m_refs, acc_refs = (ka_s, kb_s), (vta_s, vtb_s), (m_a, m_b), (acc_a, acc_b)
    items = [(x, c * ATT_COLS) for x in range(2) for c in range(tq // ATT_COLS)]
    steps = [(qi, j) for qi in range(n_q) for j in range(qi + 1)]
    q_cache = {}

    def q_of(qi):
        if qi not in q_cache:
            q_cache[qi] = make_q(qi)
        return q_cache[qi]

    def key_chunks(width):
        return [(k0, ATT_KEYS) for k0 in range(0, width, ATT_KEYS)]

    def scores_to(step, item):
        (qi, j), (x, c0) = step, item
        cols = slice(c0, c0 + ATT_COLS)
        q_cols = q_of(qi)[x][cols, :]
        tile_max = None
        for k0, kn in key_chunks(wide):
            s = lax.dot_general(k_refs[x][j * wide + k0:j * wide + k0 + kn, :], q_cols, nt,
                                preferred_element_type=f32)
            s_refs[x][k0:k0 + kn, cols] = s
            c_max = jnp.max(s, axis=0, keepdims=True)
            tile_max = c_max if tile_max is None else jnp.maximum(tile_max, c_max)
        tmax_refs[x][:, cols] = tile_max

    def update(step, item):
        (qi, j), (x, c0) = step, item
        cols = slice(c0, c0 + ATT_COLS)
        diagonal = j == qi
        width = c0 + ATT_COLS if diagonal else wide

        def chunk(k0, kn):
            s = s_refs[x][k0:k0 + kn, cols]
            return s + mask_ref[k0:k0 + kn, cols] if diagonal else s

        if diagonal:
            tile_max = None
            for k0, kn in key_chunks(width):
                c_max = jnp.max(chunk(k0, kn), axis=0, keepdims=True)
                tile_max = c_max if tile_max is None else jnp.maximum(tile_max, c_max)
        else:
            tile_max = tmax_refs[x][:, cols]
        m = m_refs[x][:, cols]
        m_new = jnp.maximum(m, tile_max)
        pv = None
        for k0, kn in key_chunks(width):
            p = jnp.exp2(chunk(k0, kn) - m_new).astype(bf16)
            part = jnp.dot(vt_refs[x][:, j * wide + k0:j * wide + k0 + kn], p,
                           preferred_element_type=f32)
            pv = part if pv is None else pv + part
        acc_refs[x][:, cols] = jnp.exp2(m - m_new) * acc_refs[x][:, cols] + pv
        m_refs[x][:, cols] = m_new

    def reset():
        for x in range(2):
            m_refs[x][...] = jnp.full(m_refs[x].shape, NEG, f32)
            acc_refs[x][...] = jnp.zeros(acc_refs[x].shape, f32)

    reset()
    for item in items:
        scores_to(steps[0], item)
    for n, step in enumerate(steps):
        nxt = steps[n + 1] if n + 1 < len(steps) else None
        for item in items:
            update(step, item)
            if nxt is not None:
                scores_to(nxt, item)
        if step[1] == step[0]:
            finish(step[0], acc_a[...], acc_b[...])
            if nxt is not None:
                reset()


def _fox_kernel(q_ref, k_ref, v_ref, cum_ref, mask_ref, o_ref, ka_s, kb_s, vta_s, vtb_s,
                *flash_scratch, seq, tq):
    hp = pl.program_id(1)
    lane = lax.broadcasted_iota(i32, (1, LANES), 1)

    def head_cums(cm):
        c_a = jnp.sum(jnp.where(lane == 2 * hp, cm, 0.0), axis=-1, keepdims=True)
        c_b = jnp.sum(jnp.where(lane == 2 * hp + 1, cm, 0.0), axis=-1, keepdims=True)
        return c_a, c_b

    for c in range(seq // 512):
        rows = pl.ds(c * 512, 512)
        kk = k_ref[0, rows, :].astype(f32)
        c_a, c_b = head_cums(cum_ref[0, rows, :])
        ka_s[rows, :] = _k_aug(kk, lane, True, c_a)
        kb_s[rows, :] = _k_aug(kk, lane, False, c_b)
        vt = v_ref[0, rows, :].astype(f32).T
        vta_s[:, rows] = _vt_aug(vt[:HEAD_DIM])
        vtb_s[:, rows] = _vt_aug(vt[HEAD_DIM:])

    def make_q(qi):
        rows = pl.ds(qi * tq, tq)
        q = q_ref[0, rows, :].astype(f32)
        c_a, c_b = head_cums(cum_ref[0, rows, :])
        return _q_aug(q, lane, True, c_a), _q_aug(q, lane, False, c_b)

    def finish(qi, acc_a, acc_b):
        o_a = acc_a[:HEAD_DIM] / acc_a[HEAD_DIM:HEAD_DIM + 1]
        o_b = acc_b[:HEAD_DIM] / acc_b[HEAD_DIM:HEAD_DIM + 1]
        o_ref[0, pl.ds(qi * tq, tq), :] = jnp.concatenate([o_a, o_b], axis=0).T.astype(bf16)

    _flash_sweep(make_q, seq // tq, ka_s, kb_s, vta_s, vtb_s, mask_ref, tq, finish, flash_scratch)


def _tail_mask(diag):
    diag_t = jnp.swapaxes(diag, -1, -2)
    pad = [(0, 0)] * (diag.ndim - 2) + [(ATT_WIDE - diag.shape[-1], 0), (0, 0)]
    return jnp.pad(diag_t, pad)


def _fox_attention(proj3, cum3):
    bsz, seq, _ = proj3.shape
    tq, tk = ATT_TQ, ATT_WIDE
    n_pairs = FOX_HEADS // 2
    r = jnp.arange(tq)
    mask = _tail_mask(jnp.where(r[:, None] >= r[None, :], 0.0, NEG).astype(f32))
    return pl.pallas_call(
        functools.partial(_fox_kernel, seq=seq, tq=tq),
        grid=(bsz, n_pairs),
        in_specs=[pl.BlockSpec((1, seq, LANES), lambda b, h: (b, 0, h)),
                  pl.BlockSpec((1, seq, LANES), lambda b, h: (b, 0, 4 + h)),
                  pl.BlockSpec((1, seq, LANES), lambda b, h: (b, 0, 8 + h)),
                  pl.BlockSpec((1, seq, LANES), lambda b, h: (b, 0, 0)),
                  pl.BlockSpec((tk, tq), lambda b, h: (0, 0))],
        out_specs=pl.BlockSpec((1, seq, LANES), lambda b, h: (b, 0, h)),
        out_shape=jax.ShapeDtypeStruct((bsz, seq, FOX_WIDTH), bf16),
        scratch_shapes=[pltpu.VMEM((seq, LANES), bf16), pltpu.VMEM((seq, LANES), bf16),
                        pltpu.VMEM((HEAD_DIM + ONES_ROWS, seq), bf16),
                        pltpu.VMEM((HEAD_DIM + ONES_ROWS, seq), bf16)]
        + _flash_scratch(tq, HEAD_DIM + ONES_ROWS),
        compiler_params=_cparams("parallel", "parallel"),
        name="fox_attn",
    )(proj3, proj3, proj3, cum3, mask)


def _diff_kernel(slope_ref, q_ref, k_ref, v_ref, mask_ref, lamv_ref, g_ref, o_ref,
                 ka_s, kb_s, vt_s, *flash_scratch, seq, tq, lam_init):
    h = pl.program_id(1)
    lane = lax.broadcasted_iota(i32, (1, LANES), 1)
    slope = slope_ref[h]

    def pos_bias(start, n):
        pos = (start + lax.broadcasted_iota(i32, (n, 1), 0)).astype(f32)
        return -(slope * pos)

    for c in range(seq // 512):
        rows = pl.ds(c * 512, 512)
        kk = k_ref[0, rows, :].astype(f32)
        cb = pos_bias(c * 512, 512)
        ka_s[rows, :] = _k_aug(kk, lane, True, cb)
        kb_s[rows, :] = _k_aug(kk, lane, False, cb)
        vt_s[:, rows] = _vt_aug(v_ref[0, rows, :].astype(f32).T)

    def make_q(qi):
        q = q_ref[0, pl.ds(qi * tq, tq), :].astype(f32)
        cq = pos_bias(qi * tq, tq)
        return _q_aug(q, lane, True, cq), _q_aug(q, lane, False, cq)

    def finish(qi, acc_a, acc_b):
        lv = lamv_ref[...]
        s1 = jnp.sum(lv[0:1, :] * lv[1:2, :], axis=-1, keepdims=True)
        s2 = jnp.sum(lv[2:3, :] * lv[3:4, :], axis=-1, keepdims=True)
        lam = jnp.exp(s1) - jnp.exp(s2) + lam_init
        o_a = acc_a[:LANES] / acc_a[LANES:LANES + 1]
        o_b = acc_b[:LANES] / acc_b[LANES:LANES + 1]
        d = (o_a - lam * o_b).T
        y = d * lax.rsqrt(jnp.mean(d * d, axis=-1, keepdims=True) + SUBLN_EPS) * g_ref[...]
        o_ref[0, pl.ds(qi * tq, tq), :] = (y * (1.0 - lam_init)).astype(bf16)

    _flash_sweep(make_q, seq // tq, ka_s, kb_s, vt_s, vt_s, mask_ref.at[0], tq, finish, flash_scratch)


def _diff_attention(proj3, lam_vecs, subln_g, lam_init):
    bsz, seq, _ = proj3.shape
    tq, tk = ATT_TQ, ATT_WIDE
    slopes = jnp.asarray([2.0 ** (-8.0 * (i + 1) / DIFF_HEADS) for i in range(DIFF_HEADS)], f32) * LOG2E
    r = jnp.arange(tq)
    tq_i, tk_i = r[:, None], r[None, :]
    chunk_ok = (tq_i // CHUNK) >= (tk_i // CHUNK)
    ahead = jnp.maximum(tk_i - tq_i, 0).astype(f32)
    mask = _tail_mask(jnp.where(chunk_ok[None], -2.0 * slopes[:, None, None] * ahead[None], NEG).astype(f32))
    grid_spec = pltpu.PrefetchScalarGridSpec(
        num_scalar_prefetch=1,
        grid=(bsz, DIFF_HEADS),
        in_specs=[pl.BlockSpec((1, seq, LANES), lambda b, h, s: (b, 0, 12 + h)),
                  pl.BlockSpec((1, seq, LANES), lambda b, h, s: (b, 0, 16 + h)),
                  pl.BlockSpec((1, seq, LANES), lambda b, h, s: (b, 0, 20 + h)),
                  pl.BlockSpec((1, tk, tq), lambda b, h, s: (h, 0, 0)),
                  pl.BlockSpec((8, LANES), lambda b, h, s: (0, 0)),
                  pl.BlockSpec((1, LANES), lambda b, h, s: (0, 0))],
        out_specs=pl.BlockSpec((1, seq, LANES), lambda b, h, s: (b, 0, h)),
        scratch_shapes=[pltpu.VMEM((seq, LANES), bf16), pltpu.VMEM((seq, LANES), bf16),
                        pltpu.VMEM((LANES + ONES_ROWS, seq), bf16)]
        + _flash_scratch(tq, LANES + ONES_ROWS),
    )
    return pl.pallas_call(
        functools.partial(_diff_kernel, seq=seq, tq=tq, lam_init=lam_init),
        grid_spec=grid_spec,
        out_shape=jax.ShapeDtypeStruct((bsz, seq, DIFF_WIDTH), bf16),
        compiler_params=_cparams("parallel", "parallel"),
        name="diff_attn",
    )(slopes, proj3, proj3, proj3, mask, lam_vecs, subln_g)


ROUTER_ROWS = 8 + N_EXPERTS


def _outproj_kernel(fox_ref, diff_ref, x_ref, mod_ref, g_ref, wo_ref, wr_ref,
                    br_ref, triu_ref, x1_ref, hp_ref, route_ref, wcol_ref, cnt_ref):
    tm = x_ref.shape[0]
    y = jnp.dot(fox_ref[...], wo_ref[0:FOX_WIDTH, :], preferred_element_type=f32)
    y += jnp.dot(diff_ref[...], wo_ref[FOX_WIDTH:, :], preferred_element_type=f32)
    x1 = x_ref[...] + mod_ref[0, 2:3, :] * y
    x1_ref[...] = x1
    ms = jnp.mean(x1 * x1, axis=-1, keepdims=True)
    h2 = (x1 * lax.rsqrt(ms + NORM_EPS) * g_ref[...]) * (1.0 + mod_ref[0, 4:5, :]) + mod_ref[0, 3:4, :]
    _store_slabs(hp_ref, _pack_rows(h2))

    nt = (((1,), (1,)), ((), ()))
    h_hi = h2.astype(bf16)
    h_lo = (h2 - h_hi.astype(f32)).astype(bf16)
    stacked = lax.dot_general(wr_ref[...], h_hi, nt, preferred_element_type=f32)
    logits = stacked[:ROUTER_ROWS] + stacked[ROUTER_ROWS:]
    logits += lax.dot_general(wr_ref[:ROUTER_ROWS, :], h_lo, nt, preferred_element_type=f32)
    logits = logits + br_ref[...]

    row8 = lax.broadcasted_iota(i32, (8, tm), 0)
    gl = jnp.where(row8 < N_GROUPS, logits[0:8, :], NEG)
    gmax = jnp.max(gl, axis=0, keepdims=True)
    grp = jnp.min(jnp.where(gl == gmax, row8, 8), axis=0, keepdims=True)
    p_g = 1.0 / jnp.sum(jnp.exp(gl - gmax), axis=0, keepdims=True)
    sel = logits[8:16, :]
    for g in range(1, N_GROUPS):
        sel = jnp.where(grp == g, logits[8 + 8 * g:16 + 8 * g, :], sel)
    v1 = jnp.max(sel, axis=0, keepdims=True)
    i1 = jnp.min(jnp.where(sel == v1, row8, 8), axis=0, keepdims=True)
    sel2 = jnp.where(row8 == i1, -jnp.inf, sel)
    v2 = jnp.max(sel2, axis=0, keepdims=True)
    i2 = jnp.min(jnp.where(sel2 == v2, row8, 8), axis=0, keepdims=True)
    e21 = jnp.exp(v2 - v1)
    w1 = p_g / (1.0 + e21)
    w2 = w1 * e21
    e1 = grp * EXPERTS_PER_GROUP + i1
    e2 = grp * EXPERTS_PER_GROUP + i2

    row32 = lax.broadcasted_iota(i32, (N_EXPERTS, tm), 0)
    oh1 = row32 == e1
    oh2 = row32 == e2
    both = jnp.where(oh1 | oh2, 1.0, 0.0)
    prefix = jnp.dot(both.astype(bf16), triu_ref[...], preferred_element_type=f32)
    r1 = jnp.sum(jnp.where(oh1, prefix, 0.0), axis=0, keepdims=True).astype(i32)
    r2 = jnp.sum(jnp.where(oh2, prefix, 0.0), axis=0, keepdims=True).astype(i32)
    route = jnp.where(row8 == 0, e1, jnp.where(row8 == 1, e2,
                      jnp.where(row8 == 2, r1, jnp.where(row8 == 3, r2, 0))))
    route_ref[...] = route
    cnt = jnp.sum(both, axis=1, keepdims=True)
    cnt_ref[0] = jnp.broadcast_to(cnt, (N_EXPERTS, LANES)).astype(i32)
    row128 = lax.broadcasted_iota(i32, (LANES, tm), 0)
    w_rows = jnp.where(row128 == 0, w1, jnp.where(row128 == 1, w2, 0.0))
    wcol_ref[...] = w_rows.T


def _outproj(fox2, diff2, x2, mod, g2, w_o, wr_parts, b_r, seq):
    t, d = x2.shape
    tm = ROW_TILE
    tiles_per_batch = seq // tm
    n_tiles = t // tm
    triu = (jnp.arange(tm)[:, None] < jnp.arange(tm)[None, :]).astype(bf16)
    row = lambda i: (i, 0)
    const = lambda i: (0, 0)
    return pl.pallas_call(
        _outproj_kernel,
        grid=(n_tiles,),
        in_specs=[pl.BlockSpec((tm, FOX_WIDTH), row),
                  pl.BlockSpec((tm, DIFF_WIDTH), row),
                  pl.BlockSpec((tm, d), row),
                  pl.BlockSpec((1, 6, d), lambda i: (i // tiles_per_batch, 0, 0)),
                  pl.BlockSpec((1, d), const),
                  pl.BlockSpec((d, d), const),
                  pl.BlockSpec((2 * ROUTER_ROWS, d), const),
                  pl.BlockSpec((ROUTER_ROWS, 1), const),
                  pl.BlockSpec((tm, tm), const)],
        out_specs=[pl.BlockSpec((tm, d), row),
                   pl.BlockSpec((ROW_CHUNKS * tm, LANES), row),
                   pl.BlockSpec((8, tm), lambda i: (0, i)),
                   pl.BlockSpec((tm, LANES), row),
                   pl.BlockSpec((1, N_EXPERTS, LANES), lambda i: (i, 0, 0))],
        out_shape=[jax.ShapeDtypeStruct((t, d), f32),
                   jax.ShapeDtypeStruct((ROW_CHUNKS * t, LANES), i32),
                   jax.ShapeDtypeStruct((8, t), i32),
                   jax.ShapeDtypeStruct((t, LANES), f32),
                   jax.ShapeDtypeStruct((n_tiles, N_EXPERTS, LANES), i32)],
        compiler_params=_cparams("parallel"),
        name="outproj_router",
    )(fox2, diff2, x2, mod, g2, w_o, wr_parts, b_r, triu)


def _dispatch_kernel(dest_ref, h_ref, xs_in_ref, xs_ref, sem, *, tm):
    del xs_in_ref
    i = pl.program_id(0)
    base = i * (2 * tm)

    def row_copy(r, k):
        d = dest_ref[base + 2 * r + k]
        src = h_ref.at[pl.ds(pl.multiple_of(ROW_CHUNKS * r, ROW_CHUNKS), ROW_CHUNKS), :]
        return pltpu.make_async_copy(src, xs_ref.at[d], sem)

    def issue(r, _):
        row_copy(r, 0).start()
        row_copy(r, 1).start()
        return 0

    lax.fori_loop(0, tm, issue, 0, unroll=8)

    def drain(r, _):
        row_copy(r, 0).wait()
        row_copy(r, 1).wait()
        return 0

    lax.fori_loop(0, tm, drain, 0, unroll=8)


def _dispatch(dest, h_packed, n_slots):
    t = h_packed.shape[0] // ROW_CHUNKS
    tm = GATHER_TILE
    xs0 = jnp.zeros((n_slots, ROW_CHUNKS, LANES), i32)
    grid_spec = pltpu.PrefetchScalarGridSpec(
        num_scalar_prefetch=1,
        grid=(t // tm,),
        in_specs=[pl.BlockSpec((ROW_CHUNKS * tm, LANES), lambda i, d: (i, 0)),
                  pl.BlockSpec(memory_space=pl.ANY)],
        out_specs=pl.BlockSpec(memory_space=pl.ANY),
        scratch_shapes=[pltpu.SemaphoreType.DMA(())],
    )
    return pl.pallas_call(
        functools.partial(_dispatch_kernel, tm=tm),
        grid_spec=grid_spec,
        out_shape=jax.ShapeDtypeStruct((n_slots, ROW_CHUNKS, LANES), i32),
        input_output_aliases={2: 0},
        compiler_params=_cparams("arbitrary"),
        name="moe_dispatch",
    )(dest, h_packed, xs0)


def _experts_kernel(be_ref, nu_ref, xs_ref, wg_ref, wu_ref, wd_ref, y_ref, wg_s, wu_s, wd_s):
    i = pl.program_id(0)
    prev = be_ref[jnp.maximum(i - 1, 0)]
    fresh = jnp.logical_or(i == 0, be_ref[i] != prev)

    @pl.when(jnp.logical_and(fresh, i < nu_ref[0]))
    def _():
        wg_s[...] = wg_ref[0].astype(bf16)
        wu_s[...] = wu_ref[0].astype(bf16)
        wd_s[...] = wd_ref[0].astype(bf16)

    @pl.when(i < nu_ref[0])
    def _():
        lo, hi = _unpack_rows(_load_slabs(xs_ref, MOE_BLOCK))
        xb = jnp.concatenate([lo.astype(bf16), hi.astype(bf16)], axis=-1)
        g = jnp.dot(xb, wg_s[...], preferred_element_type=f32)
        u = jnp.dot(xb, wu_s[...], preferred_element_type=f32)
        hid = (g * (1.0 / (1.0 + jnp.exp(-g)))) * u
        y = jnp.dot(hid.astype(bf16), wd_s[...], preferred_element_type=f32)
        _store_slabs(y_ref, _pack_rows(y))

    @pl.when(i >= nu_ref[0])
    def _():
        y_ref[...] = jnp.zeros_like(y_ref)


def _experts(block_e, n_used, xs, w_gate, w_up, w_down):
    n_slots = xs.shape[0]
    bk = MOE_BLOCK
    grid_spec = pltpu.PrefetchScalarGridSpec(
        num_scalar_prefetch=2,
        grid=(n_slots // bk,),
        in_specs=[pl.BlockSpec((ROW_CHUNKS * bk, LANES), lambda i, be, nu: (i, 0)),
                  pl.BlockSpec((1, D_MODEL, D_EXPERT), lambda i, be, nu: (be[i], 0, 0)),
                  pl.BlockSpec((1, D_MODEL, D_EXPERT), lambda i, be, nu: (be[i], 0, 0)),
                  pl.BlockSpec((1, D_EXPERT, D_MODEL), lambda i, be, nu: (be[i], 0, 0))],
        out_specs=pl.BlockSpec((ROW_CHUNKS * bk, LANES), lambda i, be, nu: (i, 0)),
        scratch_shapes=[pltpu.VMEM((D_MODEL, D_EXPERT), bf16),
                        pltpu.VMEM((D_MODEL, D_EXPERT), bf16),
                        pltpu.VMEM((D_EXPERT, D_MODEL), bf16)],
    )
    return pl.pallas_call(
        _experts_kernel,
        grid_spec=grid_spec,
        out_shape=jax.ShapeDtypeStruct((ROW_CHUNKS * n_slots, LANES), i32),
        compiler_params=_cparams("arbitrary"),
        name="moe_experts",
    )(block_e, n_used, xs.reshape(-1, LANES), w_gate, w_up, w_down)


def _combine_kernel(dest_ref, y_ref, x1_ref, wcol_ref, mod_ref, g_ref, o_ref, buf, sems, *, tm):
    i = pl.program_id(0)
    n = pl.num_programs(0)

    def row_copy(tile, slot, r, k):
        d = dest_ref[tile * (2 * tm) + 2 * r + k]
        dst = buf.at[slot, k, pl.ds(pl.multiple_of(ROW_CHUNKS * r, ROW_CHUNKS), ROW_CHUNKS), :]
        return pltpu.make_async_copy(y_ref.at[d], dst, sems.at[slot])

    def issue_tile(tile, slot):
        def body(r, _):
            row_copy(tile, slot, r, 0).start()
            row_copy(tile, slot, r, 1).start()
            return 0
        lax.fori_loop(0, tm, body, 0, unroll=8)

    @pl.when(i == 0)
    def _():
        issue_tile(0, 0)

    @pl.when(i + 1 < n)
    def _():
        issue_tile(i + 1, (i + 1) % 2)

    slot = i % 2

    def drain(r, _):
        row_copy(i, slot, r, 0).wait()
        row_copy(i, slot, r, 1).wait()
        return 0

    lax.fori_loop(0, tm, drain, 0, unroll=8)

    wc = wcol_ref[...]
    w0 = wc[:, 0:1]
    w1 = wc[:, 1:2]
    lo0, hi0 = _unpack_rows(_load_slabs(buf.at[slot, 0], tm))
    lo1, hi1 = _unpack_rows(_load_slabs(buf.at[slot, 1], tm))
    moe = jnp.concatenate([w0 * lo0 + w1 * lo1, w0 * hi0 + w1 * hi1], axis=-1)
    x = x1_ref[...] + mod_ref[0, 5:6, :] * moe
    ms = jnp.mean(x * x, axis=-1, keepdims=True)
    o_ref[...] = x * lax.rsqrt(ms + NORM_EPS) * g_ref[...]


def _combine(dest, y_packed, x1, wcol, mod, g_f, seq):
    t, d = x1.shape
    tm = GATHER_TILE
    tiles_per_batch = seq // tm
    grid_spec = pltpu.PrefetchScalarGridSpec(
        num_scalar_prefetch=1,
        grid=(t // tm,),
        in_specs=[pl.BlockSpec(memory_space=pl.ANY),
                  pl.BlockSpec((tm, d), lambda i, ds: (i, 0)),
                  pl.BlockSpec((tm, LANES), lambda i, ds: (i, 0)),
                  pl.BlockSpec((1, 6, d), lambda i, ds: (i // tiles_per_batch, 0, 0)),
                  pl.BlockSpec((1, d), lambda i, ds: (0, 0))],
        out_specs=pl.BlockSpec((tm, d), lambda i, ds: (i, 0)),
        scratch_shapes=[pltpu.VMEM((2, 2, ROW_CHUNKS * tm, LANES), i32),
                        pltpu.SemaphoreType.DMA((2,))],
    )
    return pl.pallas_call(
        functools.partial(_combine_kernel, tm=tm),
        grid_spec=grid_spec,
        out_shape=jax.ShapeDtypeStruct((t, d), f32),
        compiler_params=_cparams("arbitrary"),
        name="moe_combine",
    )(dest, y_packed.reshape(-1, ROW_CHUNKS, LANES), x1, wcol, mod, g_f)


def _route_tables(route, cnt, tm):
    n_tiles = cnt.shape[0]
    t = route.shape[1]
    bk = MOE_BLOCK
    n_blocks = (2 * t) // bk + N_EXPERTS
    ti = jnp.arange(n_tiles)
    ei = jnp.arange(N_EXPERTS)
    tile_base = jnp.sum(jnp.where((ti[:, None] > ti[None, :])[:, :, None], cnt[None], 0), axis=1)
    total = jnp.sum(cnt, axis=0)
    padded = (total + bk - 1) // bk * bk
    pends = jnp.sum(jnp.where(ei[None, :] <= ei[:, None], padded[None, :], 0), axis=1)
    base = (pends - padded)[None, :] + tile_base
    base_tok = jnp.broadcast_to(base[:, None, :], (n_tiles, tm, N_EXPERTS)).reshape(t, N_EXPERTS)

    def slot_base(e):
        return jnp.sum(jnp.where(ei[None, :] == e[:, None], base_tok, 0), axis=1)

    d0 = slot_base(route[0]) + route[2]
    d1 = slot_base(route[1]) + route[3]
    dest = jnp.stack([d0, d1], axis=1).reshape(-1).astype(i32)
    block_start = jnp.arange(n_blocks, dtype=i32) * bk
    block_e = jnp.clip(jnp.sum(pends[None, :] <= block_start[:, None], axis=1),
                       0, N_EXPERTS - 1).astype(i32)
    n_used = (pends[-1] // bk).astype(i32).reshape(1)
    return dest, block_e, n_used, n_blocks * bk


def _layer(x2, c, seq, l, ada_w, ada_b, norm1_g, w_in, b_f, lam_q1, lam_k1, lam_q2, lam_k2,
           subln_g, w_o, norm2_g, w_rg, b_rg, w_re, b_re, w_gate, w_up, w_down):
    t, d = x2.shape
    bsz = t // seq
    lam_init = 0.8 - 0.6 * math.exp(-0.3 * l)
    mod = _ada(c, ada_w, ada_b).reshape(bsz, 6, d)

    z0 = 3 * FOX_WIDTH
    w_main = jnp.concatenate([w_in[:, :z0], w_in[:, z0 + FOX_HEADS:]], axis=1).astype(bf16)
    w_fz = jnp.pad(w_in[:, z0:z0 + FOX_HEADS], ((0, 0), (0, LANES - FOX_HEADS))).astype(bf16)
    b_fz = jnp.pad(b_f, (0, LANES - FOX_HEADS)).reshape(1, LANES)
    proj, cum = _inproj(x2, mod, norm1_g.reshape(1, d), w_main, w_fz, b_fz, seq)
    proj3 = proj.reshape(bsz, seq, -1)

    fox = _fox_attention(proj3, cum.reshape(bsz, seq, LANES))
    lam_vecs = jnp.pad(jnp.stack([lam_q1, lam_k1, lam_q2, lam_k2]).astype(f32),
                       ((0, 4), (0, LANES - HEAD_DIM)))
    diff = _diff_attention(proj3, lam_vecs, subln_g.reshape(1, LANES), lam_init)

    w_r = jnp.concatenate([w_rg.T, jnp.zeros((8 - N_GROUPS, d), f32), w_re.T], axis=0)
    wr_hi = w_r.astype(bf16)
    wr_lo = (w_r - wr_hi.astype(f32)).astype(bf16)
    b_r = jnp.concatenate([b_rg, jnp.zeros((8 - N_GROUPS,), f32), b_re]).reshape(ROUTER_ROWS, 1)
    x1, h_packed, route, wcol, cnt = _outproj(
        fox.reshape(t, FOX_WIDTH), diff.reshape(t, DIFF_WIDTH), x2, mod, norm2_g.reshape(1, d),
        w_o.astype(bf16), jnp.concatenate([wr_hi, wr_lo], axis=0), b_r, seq)

    dest, block_e, n_used, n_slots = _route_tables(route, cnt[:, :, 0], ROW_TILE)
    xs = _dispatch(dest, h_packed, n_slots)
    y_packed = _experts(block_e, n_used, xs, w_gate, w_up, w_down)
    return dest, y_packed, x1, wcol, mod


def kernel(x, c, ada_w, ada_b, norm1_g, w_in, b_f, lam_q1, lam_k1, lam_q2, lam_k2, subln_g, w_o,
           norm2_g, w_rg, b_rg, w_re, b_re, w_gate, w_up, w_down, norm_f_g):
    bsz, seq, d = x.shape
    depth = ada_w.shape[0]
    assert depth == 1 and d == D_MODEL and seq % ROW_TILE == 0 and seq % ATT_WIDE == 0
    x2 = x.reshape(bsz * seq, d)
    dest, y_packed, x1, wcol, mod = _layer(
        x2, c, seq, 0, ada_w[0], ada_b[0], norm1_g[0], w_in[0], b_f[0], lam_q1[0], lam_k1[0],
        lam_q2[0], lam_k2[0], subln_g[0], w_o[0], norm2_g[0], w_rg[0], b_rg[0], w_re[0], b_re[0],
        w_gate[0], w_up[0], w_down[0])
    out = _combine(dest, y_packed, x1, wcol, mod, norm_f_g.reshape(1, d), seq)
    return out.reshape(bsz, seq, d)
```

```python
import functools
import math

import jax
import jax.numpy as jnp
from jax import lax
from jax.experimental import pallas as pl
from jax.experimental.pallas import tpu as pltpu

f32 = jnp.float32
bf16 = jnp.bfloat16
i32 = jnp.int32

D_MODEL = 1024
HEAD_DIM = 64
FOX_HEADS = 8
FOX_WIDTH = FOX_HEADS * HEAD_DIM
DIFF_HEADS = 4
DIFF_QK_WIDTH = DIFF_HEADS * 2 * HEAD_DIM
DIFF_WIDTH = DIFF_HEADS * 2 * HEAD_DIM
CHUNK = 64
N_GROUPS = 4
EXPERTS_PER_GROUP = 8
N_EXPERTS = N_GROUPS * EXPERTS_PER_GROUP
D_EXPERT = 512
NORM_EPS = 1e-6
SUBLN_EPS = 1e-5

LANES = 128
LOG2E = 1.4426950408889634
Q_SCALE = HEAD_DIM ** -0.5 * LOG2E
NEG = -1e30
HALF = D_MODEL // 2
ROW_CHUNKS = HALF // LANES

ROW_TILE = 512
ATT_TQ = 1024
ATT_WIDE = 1024
ATT_COLS = 256
ATT_KEYS = 512
MOE_BLOCK = 256
DISPATCH_TILE = 2048
GATHER_TILE = 512
VMEM_LIMIT = 48 * 1024 * 1024


def _cparams(*sem):
    return pltpu.CompilerParams(dimension_semantics=sem, vmem_limit_bytes=VMEM_LIMIT)


def _split3(c):
    hi = c.astype(bf16).astype(f32)
    r = c - hi
    mid = r.astype(bf16).astype(f32)
    lo = r - mid
    return hi, mid, lo


def _pack_rows(y):
    a = pltpu.bitcast(y[:, :HALF].astype(bf16).astype(f32), i32)
    b = pltpu.bitcast(y[:, HALF:].astype(bf16).astype(f32), i32)
    return lax.shift_right_logical(a, 16) | (b & jnp.int32(-65536))


def _store_slabs(ref, packed):
    n = packed.shape[0]
    for j in range(ROW_CHUNKS):
        ref[pl.ds(j, n, stride=ROW_CHUNKS), :] = packed[:, j * LANES:(j + 1) * LANES]


def _load_slabs(ref, n):
    return jnp.concatenate([ref[pl.ds(j, n, stride=ROW_CHUNKS), :] for j in range(ROW_CHUNKS)], axis=-1)


def _unpack_rows(w):
    lo = pltpu.bitcast(lax.shift_left(w, 16), f32)
    hi = pltpu.bitcast(w & jnp.int32(-65536), f32)
    return lo, hi


def _ada_kernel(c_ref, w_ref, b_ref, o_ref):
    c = c_ref[...]
    w = w_ref[...]
    c_hi = c.astype(bf16)
    c_lo = (c - c_hi.astype(f32)).astype(bf16)
    w_hi = w.astype(bf16)
    w_lo = (w - w_hi.astype(f32)).astype(bf16)
    acc = jnp.dot(c_hi, w_hi, preferred_element_type=f32)
    acc += jnp.dot(c_hi, w_lo, preferred_element_type=f32)
    acc += jnp.dot(c_lo, w_hi, preferred_element_type=f32)
    o_ref[...] = acc + b_ref[...]


def _ada(c, w, b):
    bsz, d = c.shape
    n = w.shape[1]
    tn = 1024
    return pl.pallas_call(
        _ada_kernel,
        grid=(n // tn,),
        in_specs=[pl.BlockSpec((bsz, d), lambda j: (0, 0)),
                  pl.BlockSpec((d, tn), lambda j: (0, j)),
                  pl.BlockSpec((1, tn), lambda j: (0, j))],
        out_specs=pl.BlockSpec((bsz, tn), lambda j: (0, j)),
        out_shape=jax.ShapeDtypeStruct((bsz, n), f32),
        compiler_params=_cparams("parallel"),
        name="ada_mod",
    )(c, w, b.reshape(1, n))


def _inproj_kernel(x_ref, mod_ref, g_ref, wm_ref, wz_ref, bf_ref, tril_ref,
                   proj_ref, cum_ref, carry_ref, *, tiles_per_batch):
    i = pl.program_id(0)
    x = x_ref[...]
    ms = jnp.mean(x * x, axis=-1, keepdims=True)
    y = x * lax.rsqrt(ms + NORM_EPS) * g_ref[...]
    h = (y * (1.0 + mod_ref[0, 1:2, :]) + mod_ref[0, 0:1, :]).astype(bf16)
    n_chunks = proj_ref.shape[1] // 512
    for j in range(n_chunks):
        acc = jnp.dot(h, wm_ref[:, j * 512:(j + 1) * 512], preferred_element_type=f32)
        if j in (0, 3):
            acc = acc * Q_SCALE
        proj_ref[:, j * 512:(j + 1) * 512] = acc.astype(bf16)
    fz = jnp.dot(h, wz_ref[...], preferred_element_type=f32) + bf_ref[...]
    ls = (jnp.minimum(fz, 0.0) - jnp.log(1.0 + jnp.exp(-jnp.abs(fz)))) * LOG2E
    lane = lax.broadcasted_iota(i32, (1, LANES), 1)
    hi, mid, lo = _split3(jnp.where(lane < FOX_HEADS, ls, 0.0))
    parts = hi + pltpu.roll(mid, FOX_HEADS, 1) + pltpu.roll(lo, 2 * FOX_HEADS, 1)
    sums = jnp.dot(tril_ref[...], parts.astype(bf16), preferred_element_type=f32)
    local = sums + pltpu.roll(sums, LANES - FOX_HEADS, 1) + pltpu.roll(sums, LANES - 2 * FOX_HEADS, 1)

    @pl.when(i % tiles_per_batch == 0)
    def _():
        carry_ref[...] = jnp.zeros_like(carry_ref)

    cum = local + carry_ref[0:1, :]
    cum_ref[...] = cum
    tm = x.shape[0]
    carry_ref[0:1, :] = cum[tm - 1:tm, :]


def _inproj(x2, mod, g1, w_main, w_fz, b_fz, seq):
    t, d = x2.shape
    tm = ROW_TILE
    tiles_per_batch = seq // tm
    n_main = w_main.shape[1]
    tril = (jnp.arange(tm)[:, None] >= jnp.arange(tm)[None, :]).astype(bf16)
    return pl.pallas_call(
        functools.partial(_inproj_kernel, tiles_per_batch=tiles_per_batch),
        grid=(t // tm,),
        in_specs=[pl.BlockSpec((tm, d), lambda i: (i, 0)),
                  pl.BlockSpec((1, 6, d), lambda i: (i // tiles_per_batch, 0, 0)),
                  pl.BlockSpec((1, d), lambda i: (0, 0)),
                  pl.BlockSpec((d, n_main), lambda i: (0, 0)),
                  pl.BlockSpec((d, LANES), lambda i: (0, 0)),
                  pl.BlockSpec((1, LANES), lambda i: (0, 0)),
                  pl.BlockSpec((tm, tm), lambda i: (0, 0))],
        out_specs=[pl.BlockSpec((tm, n_main), lambda i: (i, 0)),
                   pl.BlockSpec((tm, LANES), lambda i: (i, 0))],
        out_shape=[jax.ShapeDtypeStruct((t, n_main), bf16),
                   jax.ShapeDtypeStruct((t, LANES), f32)],
        scratch_shapes=[pltpu.VMEM((8, LANES), f32)],
        compiler_params=_cparams("arbitrary"),
        name="inproj",
    )(x2, mod, g1, w_main, w_fz, b_fz, tril)


def _aug(data, lane, low_map, first, last):
    base = 64 if low_map else 0
    out = jnp.zeros_like(data)
    for n, val in enumerate(tuple(first) + tuple(last)):
        out = jnp.where(lane == base + n, val, out)
    keep = (lane < 64) if low_map else (lane >= 64)
    return jnp.where(keep, data, out)


_ONES3 = (1.0, 1.0, 1.0)


def _q_aug(q, lane, low_map, c):
    return _aug(q, lane, low_map, _split3(c), _ONES3).astype(bf16)


def _k_aug(k, lane, low_map, c):
    hi, mid, lo = _split3(c)
    return _aug(k, lane, low_map, _ONES3, (-hi, -mid, -lo)).astype(bf16)


ONES_ROWS = 16


def _vt_aug(vt):
    row = lax.broadcasted_iota(i32, (ONES_ROWS, vt.shape[1]), 0)
    extra = jnp.where(row == 0, 1.0, 0.0).astype(vt.dtype)
    return jnp.concatenate([vt, extra], axis=0).astype(bf16)


def _flash_scratch(tq, acc_rows):
    return ([pltpu.VMEM((ATT_WIDE, tq), f32)] * 2
            + [pltpu.VMEM((acc_rows, tq), f32)] * 2 + [pltpu.VMEM((1, tq), f32)] * 4)


def _flash_sweep(make_q, n_q, ka_s, kb_s, vta_s, vtb_s, mask_ref, tq, finish, scratch):
    s_a, s_b, acc_a, acc_b, m_a, m_b, tmax_a, tmax_b = scratch
    nt = (((1,), (1,)), ((), ()))
    wide = ATT_WIDE
    assert tq == wide
    s_refs, tmax_refs = (s_a, s_b), (tmax_a, tmax_b)
    k_refs, vt_refs, m_refs, acc_refs = (ka_s, kb_s), (vta_s, vtb_s), (m_a, m_b), (acc_a, acc_b)
    items = [(x, c * ATT_COLS) for x in range(2) for c in range(tq // ATT_COLS)]
    steps = [(qi, j) for qi in range(n_q) for j in range(qi + 1)]
    q_cache = {}

    def q_of(qi):
        if qi not in q_cache:
            q_cache[qi] = make_q(qi)
        return q_cache[qi]

    def key_chunks(width):
        return [(k0, min(ATT_KEYS, width - k0)) for k0 in range(0, width, ATT_KEYS)]

    def scores_to(step, item):
        (qi, j), (x, c0) = step, item
        cols = slice(c0, c0 + ATT_COLS)
        q_cols = q_of(qi)[x][cols, :]
        tile_max = None
        for k0, kn in key_chunks(wide):
            s = lax.dot_general(k_refs[x][j * wide + k0:j * wide + k0 + kn, :], q_cols, nt,
                                preferred_element_type=f32)
            s_refs[x][k0:k0 + kn, cols] = s
            c_max = jnp.max(s, axis=0, keepdims=True)
            tile_max = c_max if tile_max is None else jnp.maximum(tile_max, c_max)
        tmax_refs[x][:, cols] = tile_max

    def update(step, item):
        (qi, j), (x, c0) = step, item
        cols = slice(c0, c0 + ATT_COLS)
        diagonal = j == qi
        width = c0 + ATT_COLS if diagonal else wide

        def chunk(k0, kn):
            s = s_refs[x][k0:k0 + kn, cols]
            return s + mask_ref[k0:k0 + kn, cols] if diagonal else s

        if diagonal:
            tile_max = None
            for k0, kn in key_chunks(width):
                c_max = jnp.max(chunk(k0, kn), axis=0, keepdims=True)
                tile_max = c_max if tile_max is None else jnp.maximum(tile_max, c_max)
        else:
            tile_max = tmax_refs[x][:, cols]
        m = m_refs[x][:, cols]
        m_new = jnp.maximum(m, tile_max)
        pv = None
        for k0, kn in key_chunks(width):
            p = jnp.exp2(chunk(k0, kn) - m_new).astype(bf16)
            part = jnp.dot(vt_refs[x][:, j * wide + k0:j * wide + k0 + kn], p,
                           preferred_element_type=f32)
            pv = part if pv is None else pv + part
        acc_refs[x][:, cols] = jnp.exp2(m - m_new) * acc_refs[x][:, cols] + pv
        m_refs[x][:, cols] = m_new

    def reset():
        for x in range(2):
            m_refs[x][...] = jnp.full(m_refs[x].shape, NEG, f32)
            acc_refs[x][...] = jnp.zeros(acc_refs[x].shape, f32)

    reset()
    for item in items:
        scores_to(steps[0], item)
    for n, step in enumerate(steps):
        nxt = steps[n + 1] if n + 1 < len(steps) else None
        for item in items:
            update(step, item)
            if nxt is not None:
                scores_to(nxt, item)
        if step[1] == step[0]:
            finish(step[0], acc_a[...], acc_b[...])
            if nxt is not None:
                reset()


def _fox_kernel(q_ref, k_ref, v_ref, cum_ref, mask_ref, o_ref, ka_s, kb_s, vta_s, vtb_s,
                *flash_scratch, seq, tq):
    hp = pl.program_id(1)
    lane = lax.broadcasted_iota(i32, (1, LANES), 1)

    def head_cums(cm):
        c_a = jnp.sum(jnp.where(lane == 2 * hp, cm, 0.0), axis=-1, keepdims=True)
        c_b = jnp.sum(jnp.where(lane == 2 * hp + 1, cm, 0.0), axis=-1, keepdims=True)
        return c_a, c_b

    for c in range(seq // 512):
        rows = pl.ds(c * 512, 512)
        kk = k_ref[0, rows, :].astype(f32)
        c_a, c_b = head_cums(cum_ref[0, rows, :])
        ka_s[rows, :] = _k_aug(kk, lane, True, c_a)
        kb_s[rows, :] = _k_aug(kk, lane, False, c_b)
        vt = v_ref[0, rows, :].astype(f32).T
        vta_s[:, rows] = _vt_aug(vt[:HEAD_DIM])
        vtb_s[:, rows] = _vt_aug(vt[HEAD_DIM:])

    def make_q(qi):
        rows = pl.ds(qi * tq, tq)
        q = q_ref[0, rows, :].astype(f32)
        c_a, c_b = head_cums(cum_ref[0, rows, :])
        return _q_aug(q, lane, True, c_a), _q_aug(q, lane, False, c_b)

    def finish(qi, acc_a, acc_b):
        o_a = acc_a[:HEAD_DIM] / acc_a[HEAD_DIM:HEAD_DIM + 1]
        o_b = acc_b[:HEAD_DIM] / acc_b[HEAD_DIM:HEAD_DIM + 1]
        o_ref[0, pl.ds(qi * tq, tq), :] = jnp.concatenate([o_a, o_b], axis=0).T.astype(bf16)

    _flash_sweep(make_q, seq // tq, ka_s, kb_s, vta_s, vtb_s, mask_ref, tq, finish, flash_scratch)


def _tail_mask(diag):
    diag_t = jnp.swapaxes(diag, -1, -2)
    pad = [(0, 0)] * (diag.ndim - 2) + [(ATT_WIDE - diag.shape[-1], 0), (0, 0)]
    return jnp.pad(diag_t, pad)


def _fox_attention(proj3, cum3):
    bsz, seq, _ = proj3.shape
    tq, tk = ATT_TQ, ATT_WIDE
    n_pairs = FOX_HEADS // 2
    r = jnp.arange(tq)
    mask = _tail_mask(jnp.where(r[:, None] >= r[None, :], 0.0, NEG).astype(f32))
    return pl.pallas_call(
        functools.partial(_fox_kernel, seq=seq, tq=tq),
        grid=(bsz, n_pairs),
        in_specs=[pl.BlockSpec((1, seq, LANES), lambda b, h: (b, 0, h)),
                  pl.BlockSpec((1, seq, LANES), lambda b, h: (b, 0, 4 + h)),
                  pl.BlockSpec((1, seq, LANES), lambda b, h: (b, 0, 8 + h)),
                  pl.BlockSpec((1, seq, LANES), lambda b, h: (b, 0, 0)),
                  pl.BlockSpec((tk, tq), lambda b, h: (0, 0))],
        out_specs=pl.BlockSpec((1, seq, LANES), lambda b, h: (b, 0, h)),
        out_shape=jax.ShapeDtypeStruct((bsz, seq, FOX_WIDTH), bf16),
        scratch_shapes=[pltpu.VMEM((seq, LANES), bf16), pltpu.VMEM((seq, LANES), bf16),
                        pltpu.VMEM((HEAD_DIM + ONES_ROWS, seq), bf16),
                        pltpu.VMEM((HEAD_DIM + ONES_ROWS, seq), bf16)]
        + _flash_scratch(tq, HEAD_DIM + ONES_ROWS),
        compiler_params=_cparams("parallel", "parallel"),
        name="fox_attn",
    )(proj3, proj3, proj3, cum3, mask)


def _diff_kernel(slope_ref, q_ref, k_ref, v_ref, mask_ref, lamv_ref, g_ref, o_ref,
                 ka_s, kb_s, vt_s, *flash_scratch, seq, tq, lam_init):
    h = pl.program_id(1)
    lane = lax.broadcasted_iota(i32, (1, LANES), 1)
    slope = slope_ref[h]

    def pos_bias(start, n):
        pos = (start + lax.broadcasted_iota(i32, (n, 1), 0)).astype(f32)
        return -(slope * pos)

    for c in range(seq // 512):
        rows = pl.ds(c * 512, 512)
        kk = k_ref[0, rows, :].astype(f32)
        cb = pos_bias(c * 512, 512)
        ka_s[rows, :] = _k_aug(kk, lane, True, cb)
        kb_s[rows, :] = _k_aug(kk, lane, False, cb)
        vt_s[:, rows] = _vt_aug(v_ref[0, rows, :].astype(f32).T)

    def make_q(qi):
        q = q_ref[0, pl.ds(qi * tq, tq), :].astype(f32)
        cq = pos_bias(qi * tq, tq)
        return _q_aug(q, lane, True, cq), _q_aug(q, lane, False, cq)

    def finish(qi, acc_a, acc_b):
        lv = lamv_ref[...]
        s1 = jnp.sum(lv[0:1, :] * lv[1:2, :], axis=-1, keepdims=True)
        s2 = jnp.sum(lv[2:3, :] * lv[3:4, :], axis=-1, keepdims=True)
        lam = jnp.exp(s1) - jnp.exp(s2) + lam_init
        o_a = acc_a[:LANES] / acc_a[LANES:LANES + 1]
        o_b = acc_b[:LANES] / acc_b[LANES:LANES + 1]
        d = (o_a - lam * o_b).T
        y = d * lax.rsqrt(jnp.mean(d * d, axis=-1, keepdims=True) + SUBLN_EPS) * g_ref[...]
        o_ref[0, pl.ds(qi * tq, tq), :] = (y * (1.0 - lam_init)).astype(bf16)

    _flash_sweep(make_q, seq // tq, ka_s, kb_s, vt_s, vt_s, mask_ref.at[0], tq, finish, flash_scratch)


def _diff_attention(proj3, lam_vecs, subln_g, lam_init):
    bsz, seq, _ = proj3.shape
    tq, tk = ATT_TQ, ATT_WIDE
    slopes = jnp.asarray([2.0 ** (-8.0 * (i + 1) / DIFF_HEADS) for i in range(DIFF_HEADS)], f32) * LOG2E
    r = jnp.arange(tq)
    tq_i, tk_i = r[:, None], r[None, :]
    chunk_ok = (tq_i // CHUNK) >= (tk_i // CHUNK)
    ahead = jnp.maximum(tk_i - tq_i, 0).astype(f32)
    mask = _tail_mask(jnp.where(chunk_ok[None], -2.0 * slopes[:, None, None] * ahead[None], NEG).astype(f32))
    grid_spec = pltpu.PrefetchScalarGridSpec(
        num_scalar_prefetch=1,
        grid=(bsz, DIFF_HEADS),
        in_specs=[pl.BlockSpec((1, seq, LANES), lambda b, h, s: (b, 0, 12 + h)),
                  pl.BlockSpec((1, seq, LANES), lambda b, h, s: (b, 0, 16 + h)),
                  pl.BlockSpec((1, seq, LANES), lambda b, h, s: (b, 0, 20 + h)),
                  pl.BlockSpec((1, tk, tq), lambda b, h, s: (h, 0, 0)),
                  pl.BlockSpec((8, LANES), lambda b, h, s: (0, 0)),
                  pl.BlockSpec((1, LANES), lambda b, h, s: (0, 0))],
        out_specs=pl.BlockSpec((1, seq, LANES), lambda b, h, s: (b, 0, h)),
        scratch_shapes=[pltpu.VMEM((seq, LANES), bf16), pltpu.VMEM((seq, LANES), bf16),
                        pltpu.VMEM((LANES + ONES_ROWS, seq), bf16)]
        + _flash_scratch(tq, LANES + ONES_ROWS),
    )
    return pl.pallas_call(
        functools.partial(_diff_kernel, seq=seq, tq=tq, lam_init=lam_init),
        grid_spec=grid_spec,
        out_shape=jax.ShapeDtypeStruct((bsz, seq, DIFF_WIDTH), bf16),
        compiler_params=_cparams("parallel", "parallel"),
        name="diff_attn",
    )(slopes, proj3, proj3, proj3, mask, lam_vecs, subln_g)


ROUTER_ROWS = 8 + N_EXPERTS


def _outproj_kernel(fox_ref, diff_ref, x_ref, mod_ref, g_ref, wo_ref, wr_ref,
                    br_ref, triu_ref, x1_ref, hp_ref, route_ref, wcol_ref, cnt_ref):
    tm = x_ref.shape[0]
    y = jnp.dot(fox_ref[...], wo_ref[0:FOX_WIDTH, :], preferred_element_type=f32)
    y += jnp.dot(diff_ref[...], wo_ref[FOX_WIDTH:, :], preferred_element_type=f32)
    x1 = x_ref[...] + mod_ref[0, 2:3, :] * y
    x1_ref[...] = x1
    ms = jnp.mean(x1 * x1, axis=-1, keepdims=True)
    h2 = (x1 * lax.rsqrt(ms + NORM_EPS) * g_ref[...]) * (1.0 + mod_ref[0, 4:5, :]) + mod_ref[0, 3:4, :]
    _store_slabs(hp_ref, _pack_rows(h2))

    nt = (((1,), (1,)), ((), ()))
    h_hi = h2.astype(bf16)
    h_lo = (h2 - h_hi.astype(f32)).astype(bf16)
    stacked = lax.dot_general(wr_ref[...], h_hi, nt, preferred_element_type=f32)
    logits = stacked[:ROUTER_ROWS] + stacked[ROUTER_ROWS:]
    logits += lax.dot_general(wr_ref[:ROUTER_ROWS, :], h_lo, nt, preferred_element_type=f32)
    logits = logits + br_ref[...]

    row8 = lax.broadcasted_iota(i32, (8, tm), 0)
    gl = jnp.where(row8 < N_GROUPS, logits[0:8, :], NEG)
    gmax = jnp.max(gl, axis=0, keepdims=True)
    grp = jnp.min(jnp.where(gl == gmax, row8, 8), axis=0, keepdims=True)
    p_g = 1.0 / jnp.sum(jnp.exp(gl - gmax), axis=0, keepdims=True)
    sel = logits[8:16, :]
    for g in range(1, N_GROUPS):
        sel = jnp.where(grp == g, logits[8 + 8 * g:16 + 8 * g, :], sel)
    v1 = jnp.max(sel, axis=0, keepdims=True)
    i1 = jnp.min(jnp.where(sel == v1, row8, 8), axis=0, keepdims=True)
    sel2 = jnp.where(row8 == i1, -jnp.inf, sel)
    v2 = jnp.max(sel2, axis=0, keepdims=True)
    i2 = jnp.min(jnp.where(sel2 == v2, row8, 8), axis=0, keepdims=True)
    e21 = jnp.exp(v2 - v1)
    w1 = p_g / (1.0 + e21)
    w2 = w1 * e21
    e1 = grp * EXPERTS_PER_GROUP + i1
    e2 = grp * EXPERTS_PER_GROUP + i2

    row32 = lax.broadcasted_iota(i32, (N_EXPERTS, tm), 0)
    oh1 = row32 == e1
    oh2 = row32 == e2
    both = jnp.where(oh1 | oh2, 1.0, 0.0)
    prefix = jnp.dot(both.astype(bf16), triu_ref[...], preferred_element_type=f32)
    r1 = jnp.sum(jnp.where(oh1, prefix, 0.0), axis=0, keepdims=True).astype(i32)
    r2 = jnp.sum(jnp.where(oh2, prefix, 0.0), axis=0, keepdims=True).astype(i32)
    route = jnp.where(row8 == 0, e1, jnp.where(row8 == 1, e2,
                      jnp.where(row8 == 2, r1, jnp.where(row8 == 3, r2, 0))))
    route_ref[...] = route
    cnt = jnp.sum(both, axis=1, keepdims=True)
    cnt_ref[0] = jnp.broadcast_to(cnt, (N_EXPERTS, LANES)).astype(i32)
    row128 = lax.broadcasted_iota(i32, (LANES, tm), 0)
    w_rows = jnp.where(row128 == 0, w1, jnp.where(row128 == 1, w2, 0.0))
    wcol_ref[...] = w_rows.T


def _outproj(fox2, diff2, x2, mod, g2, w_o, wr_parts, b_r, seq):
    t, d = x2.shape
    tm = ROW_TILE
    tiles_per_batch = seq // tm
    n_tiles = t // tm
    triu = (jnp.arange(tm)[:, None] < jnp.arange(tm)[None, :]).astype(bf16)
    row = lambda i: (i, 0)
    const = lambda i: (0, 0)
    return pl.pallas_call(
        _outproj_kernel,
        grid=(n_tiles,),
        in_specs=[pl.BlockSpec((tm, FOX_WIDTH), row),
                  pl.BlockSpec((tm, DIFF_WIDTH), row),
                  pl.BlockSpec((tm, d), row),
                  pl.BlockSpec((1, 6, d), lambda i: (i // tiles_per_batch, 0, 0)),
                  pl.BlockSpec((1, d), const),
                  pl.BlockSpec((d, d), const),
                  pl.BlockSpec((2 * ROUTER_ROWS, d), const),
                  pl.BlockSpec((ROUTER_ROWS, 1), const),
                  pl.BlockSpec((tm, tm), const)],
        out_specs=[pl.BlockSpec((tm, d), row),
                   pl.BlockSpec((ROW_CHUNKS * tm, LANES), row),
                   pl.BlockSpec((8, tm), lambda i: (0, i)),
                   pl.BlockSpec((tm, LANES), row),
                   pl.BlockSpec((1, N_EXPERTS, LANES), lambda i: (i, 0, 0))],
        out_shape=[jax.ShapeDtypeStruct((t, d), f32),
                   jax.ShapeDtypeStruct((ROW_CHUNKS * t, LANES), i32),
                   jax.ShapeDtypeStruct((8, t), i32),
                   jax.ShapeDtypeStruct((t, LANES), f32),
                   jax.ShapeDtypeStruct((n_tiles, N_EXPERTS, LANES), i32)],
        compiler_params=_cparams("parallel"),
        name="outproj_router",
    )(fox2, diff2, x2, mod, g2, w_o, wr_parts, b_r, triu)


def _dispatch_kernel(dest_ref, h_ref, xs_in_ref, xs_ref, sem, *, tm):
    del xs_in_ref
    i = pl.program_id(0)
    base = i * (2 * tm)

    def row_copy(r, k):
        d = dest_ref[base + 2 * r + k]
        src = h_ref.at[pl.ds(pl.multiple_of(ROW_CHUNKS * r, ROW_CHUNKS), ROW_CHUNKS), :]
        return pltpu.make_async_copy(src, xs_ref.at[d], sem)

    def issue(r, _):
        row_copy(r, 0).start()
        row_copy(r, 1).start()
        return 0

    lax.fori_loop(0, tm, issue, 0, unroll=8)

    def drain(r, _):
        row_copy(r, 0).wait()
        row_copy(r, 1).wait()
        return 0

    lax.fori_loop(0, tm, drain, 0, unroll=8)


def _dispatch(dest, h_packed, n_slots):
    t = h_packed.shape[0] // ROW_CHUNKS
    tm = DISPATCH_TILE
    xs0 = jnp.zeros((n_slots, ROW_CHUNKS, LANES), i32)
    grid_spec = pltpu.PrefetchScalarGridSpec(
        num_scalar_prefetch=1,
        grid=(t // tm,),
        in_specs=[pl.BlockSpec((ROW_CHUNKS * tm, LANES), lambda i, d: (i, 0)),
                  pl.BlockSpec(memory_space=pl.ANY)],
        out_specs=pl.BlockSpec(memory_space=pl.ANY),
        scratch_shapes=[pltpu.SemaphoreType.DMA(())],
    )
    return pl.pallas_call(
        functools.partial(_dispatch_kernel, tm=tm),
        grid_spec=grid_spec,
        out_shape=jax.ShapeDtypeStruct((n_slots, ROW_CHUNKS, LANES), i32),
        input_output_aliases={2: 0},
        compiler_params=_cparams("arbitrary"),
        name="moe_dispatch",
    )(dest, h_packed, xs0)


def _experts_kernel(be_ref, nu_ref, xs_ref, wg_ref, wu_ref, wd_ref, y_ref, wg_s, wu_s, wd_s):
    i = pl.program_id(0)
    prev = be_ref[jnp.maximum(i - 1, 0)]
    fresh = jnp.logical_or(i == 0, be_ref[i] != prev)

    @pl.when(jnp.logical_and(fresh, i < nu_ref[0]))
    def _():
        wg_s[...] = wg_ref[0].astype(bf16)
        wu_s[...] = wu_ref[0].astype(bf16)
        wd_s[...] = wd_ref[0].astype(bf16)

    @pl.when(i < nu_ref[0])
    def _():
        lo, hi = _unpack_rows(_load_slabs(xs_ref, MOE_BLOCK))
        xb = jnp.concatenate([lo.astype(bf16), hi.astype(bf16)], axis=-1)
        g = jnp.dot(xb, wg_s[...], preferred_element_type=f32)
        u = jnp.dot(xb, wu_s[...], preferred_element_type=f32)
        hid = (g * (1.0 / (1.0 + jnp.exp(-g)))) * u
        y = jnp.dot(hid.astype(bf16), wd_s[...], preferred_element_type=f32)
        _store_slabs(y_ref, _pack_rows(y))

    @pl.when(i >= nu_ref[0])
    def _():
        y_ref[...] = jnp.zeros_like(y_ref)


def _experts(block_e, n_used, xs, w_gate, w_up, w_down):
    n_slots = xs.shape[0]
    bk = MOE_BLOCK
    grid_spec = pltpu.PrefetchScalarGridSpec(
        num_scalar_prefetch=2,
        grid=(n_slots // bk,),
        in_specs=[pl.BlockSpec((ROW_CHUNKS * bk, LANES), lambda i, be, nu: (i, 0)),
                  pl.BlockSpec((1, D_MODEL, D_EXPERT), lambda i, be, nu: (be[i], 0, 0)),
                  pl.BlockSpec((1, D_MODEL, D_EXPERT), lambda i, be, nu: (be[i], 0, 0)),
                  pl.BlockSpec((1, D_EXPERT, D_MODEL), lambda i, be, nu: (be[i], 0, 0))],
        out_specs=pl.BlockSpec((ROW_CHUNKS * bk, LANES), lambda i, be, nu: (i, 0)),
        scratch_shapes=[pltpu.VMEM((D_MODEL, D_EXPERT), bf16),
                        pltpu.VMEM((D_MODEL, D_EXPERT), bf16),
                        pltpu.VMEM((D_EXPERT, D_MODEL), bf16)],
    )
    return pl.pallas_call(
        _experts_kernel,
        grid_spec=grid_spec,
        out_shape=jax.ShapeDtypeStruct((ROW_CHUNKS * n_slots, LANES), i32),
        compiler_params=_cparams("arbitrary"),
        name="moe_experts",
    )(block_e, n_used, xs.reshape(-1, LANES), w_gate, w_up, w_down)


def _combine_kernel(dest_ref, y_ref, x1_ref, wcol_ref, mod_ref, g_ref, o_ref, buf, sems, *, tm):
    i = pl.program_id(0)
    n = pl.num_programs(0)

    def row_copy(tile, slot, r, k):
        d = dest_ref[tile * (2 * tm) + 2 * r + k]
        dst = buf.at[slot, k, pl.ds(pl.multiple_of(ROW_CHUNKS * r, ROW_CHUNKS), ROW_CHUNKS), :]
        return pltpu.make_async_copy(y_ref.at[d], dst, sems.at[slot])

    def issue_tile(tile, slot):
        def body(r, _):
            row_copy(tile, slot, r, 0).start()
            row_copy(tile, slot, r, 1).start()
            return 0
        lax.fori_loop(0, tm, body, 0, unroll=8)

    @pl.when(i == 0)
    def _():
        issue_tile(0, 0)

    @pl.when(i + 1 < n)
    def _():
        issue_tile(i + 1, (i + 1) % 2)

    slot = i % 2

    def drain(r, _):
        row_copy(i, slot, r, 0).wait()
        row_copy(i, slot, r, 1).wait()
        return 0

    lax.fori_loop(0, tm, drain, 0, unroll=8)

    wc = wcol_ref[...]
    w0 = wc[:, 0:1]
    w1 = wc[:, 1:2]
    lo0, hi0 = _unpack_rows(_load_slabs(buf.at[slot, 0], tm))
    lo1, hi1 = _unpack_rows(_load_slabs(buf.at[slot, 1], tm))
    moe = jnp.concatenate([w0 * lo0 + w1 * lo1, w0 * hi0 + w1 * hi1], axis=-1)
    x = x1_ref[...] + mod_ref[0, 5:6, :] * moe
    ms = jnp.mean(x * x, axis=-1, keepdims=True)
    o_ref[...] = x * lax.rsqrt(ms + NORM_EPS) * g_ref[...]


def _combine(dest, y_packed, x1, wcol, mod, g_f, seq):
    t, d = x1.shape
    tm = GATHER_TILE
    tiles_per_batch = seq // tm
    grid_spec = pltpu.PrefetchScalarGridSpec(
        num_scalar_prefetch=1,
        grid=(t // tm,),
        in_specs=[pl.BlockSpec(memory_space=pl.ANY),
                  pl.BlockSpec((tm, d), lambda i, ds: (i, 0)),
                  pl.BlockSpec((tm, LANES), lambda i, ds: (i, 0)),
                  pl.BlockSpec((1, 6, d), lambda i, ds: (i // tiles_per_batch, 0, 0)),
                  pl.BlockSpec((1, d), lambda i, ds: (0, 0))],
        out_specs=pl.BlockSpec((tm, d), lambda i, ds: (i, 0)),
        scratch_shapes=[pltpu.VMEM((2, 2, ROW_CHUNKS * tm, LANES), i32),
                        pltpu.SemaphoreType.DMA((2,))],
    )
    return pl.pallas_call(
        functools.partial(_combine_kernel, tm=tm),
        grid_spec=grid_spec,
        out_shape=jax.ShapeDtypeStruct((t, d), f32),
        compiler_params=_cparams("arbitrary"),
        name="moe_combine",
    )(dest, y_packed.reshape(-1, ROW_CHUNKS, LANES), x1, wcol, mod, g_f)


def _route_tables(route, cnt, tm):
    n_tiles = cnt.shape[0]
    t = route.shape[1]
    bk = MOE_BLOCK
    n_blocks = (2 * t) // bk + N_EXPERTS
    ti = jnp.arange(n_tiles)
    ei = jnp.arange(N_EXPERTS)
    tile_base = jnp.sum(jnp.where((ti[:, None] > ti[None, :])[:, :, None], cnt[None], 0), axis=1)
    total = jnp.sum(cnt, axis=0)
    padded = (total + bk - 1) // bk * bk
    pends = jnp.sum(jnp.where(ei[None, :] <= ei[:, None], padded[None, :], 0), axis=1)
    base = (pends - padded)[None, :] + tile_base
    base_tok = jnp.broadcast_to(base[:, None, :], (n_tiles, tm, N_EXPERTS)).reshape(t, N_EXPERTS)

    def slot_base(e):
        return jnp.sum(jnp.where(ei[None, :] == e[:, None], base_tok, 0), axis=1)

    d0 = slot_base(route[0]) + route[2]
    d1 = slot_base(route[1]) + route[3]
    dest = jnp.stack([d0, d1], axis=1).reshape(-1).astype(i32)
    block_start = jnp.arange(n_blocks, dtype=i32) * bk
    block_e = jnp.clip(jnp.sum(pends[None, :] <= block_start[:, None], axis=1),
                       0, N_EXPERTS - 1).astype(i32)
    n_used = (pends[-1] // bk).astype(i32).reshape(1)
    return dest, block_e, n_used, n_blocks * bk


def _layer(x2, c, seq, l, ada_w, ada_b, norm1_g, w_in, b_f, lam_q1, lam_k1, lam_q2, lam_k2,
           subln_g, w_o, norm2_g, w_rg, b_rg, w_re, b_re, w_gate, w_up, w_down):
    t, d = x2.shape
    bsz = t // seq
    lam_init = 0.8 - 0.6 * math.exp(-0.3 * l)
    mod = _ada(c, ada_w, ada_b).reshape(bsz, 6, d)

    z0 = 3 * FOX_WIDTH
    w_main = jnp.concatenate([w_in[:, :z0], w_in[:, z0 + FOX_HEADS:]], axis=1).astype(bf16)
    w_fz = jnp.pad(w_in[:, z0:z0 + FOX_HEADS], ((0, 0), (0, LANES - FOX_HEADS))).astype(bf16)
    b_fz = jnp.pad(b_f, (0, LANES - FOX_HEADS)).reshape(1, LANES)
    proj, cum = _inproj(x2, mod, norm1_g.reshape(1, d), w_main, w_fz, b_fz, seq)
    proj3 = proj.reshape(bsz, seq, -1)

    fox = _fox_attention(proj3, cum.reshape(bsz, seq, LANES))
    lam_vecs = jnp.pad(jnp.stack([lam_q1, lam_k1, lam_q2, lam_k2]).astype(f32),
                       ((0, 4), (0, LANES - HEAD_DIM)))
    diff = _diff_attention(proj3, lam_vecs, subln_g.reshape(1, LANES), lam_init)

    w_r = jnp.concatenate([w_rg.T, jnp.zeros((8 - N_GROUPS, d), f32), w_re.T], axis=0)
    wr_hi = w_r.astype(bf16)
    wr_lo = (w_r - wr_hi.astype(f32)).astype(bf16)
    b_r = jnp.concatenate([b_rg, jnp.zeros((8 - N_GROUPS,), f32), b_re]).reshape(ROUTER_ROWS, 1)
    x1, h_packed, route, wcol, cnt = _outproj(
        fox.reshape(t, FOX_WIDTH), diff.reshape(t, DIFF_WIDTH), x2, mod, norm2_g.reshape(1, d),
        w_o.astype(bf16), jnp.concatenate([wr_hi, wr_lo], axis=0), b_r, seq)

    dest, block_e, n_used, n_slots = _route_tables(route, cnt[:, :, 0], ROW_TILE)
    xs = _dispatch(dest, h_packed, n_slots)
    y_packed = _experts(block_e, n_used, xs, w_gate, w_up, w_down)
    return dest, y_packed, x1, wcol, mod


def kernel(x, c, ada_w, ada_b, norm1_g, w_in, b_f, lam_q1, lam_k1, lam_q2, lam_k2, subln_g, w_o,
           norm2_g, w_rg, b_rg, w_re, b_re, w_gate, w_up, w_down, norm_f_g):
    bsz, seq, d = x.shape
    depth = ada_w.shape[0]
    assert depth == 1 and d == D_MODEL and seq % ROW_TILE == 0 and seq % ATT_WIDE == 0
    x2 = x.reshape(bsz * seq, d)
    dest, y_packed, x1, wcol, mod = _layer(
        x2, c, seq, 0, ada_w[0], ada_b[0], norm1_g[0], w_in[0], b_f[0], lam_q1[0], lam_k1[0],
        lam_q2[0], lam_k2[0], subln_g[0], w_o[0], norm2_g[0], w_rg[0], b_rg[0], w_re[0], b_re[0],
        w_gate[0], w_up[0], w_down[0])
    out = _combine(dest, y_packed, x1, wcol, mod, norm_f_g.reshape(1, d), seq)
    return out.reshape(bsz, seq, d)
```

```python
import functools
import math

import jax
import jax.numpy as jnp
from jax import lax
from jax.experimental import pallas as pl
from jax.experimental.pallas import tpu as pltpu

f32 = jnp.float32
bf16 = jnp.bfloat16
i32 = jnp.int32

D_MODEL = 1024
HEAD_DIM = 64
FOX_HEADS = 8
FOX_WIDTH = FOX_HEADS * HEAD_DIM
DIFF_HEADS = 4
DIFF_QK_WIDTH = DIFF_HEADS * 2 * HEAD_DIM
DIFF_WIDTH = DIFF_HEADS * 2 * HEAD_DIM
CHUNK = 64
N_GROUPS = 4
EXPERTS_PER_GROUP = 8
N_EXPERTS = N_GROUPS * EXPERTS_PER_GROUP
D_EXPERT = 512
NORM_EPS = 1e-6
SUBLN_EPS = 1e-5

LANES = 128
LOG2E = 1.4426950408889634
Q_SCALE = HEAD_DIM ** -0.5 * LOG2E
NEG = -1e30
HALF = D_MODEL // 2
ROW_CHUNKS = HALF // LANES

ROW_TILE = 512
ATT_TQ = 1024
ATT_WIDE = 1024
ATT_COLS = 256
ATT_KEYS = 512
MOE_BLOCK = 256
DISPATCH_TILE = 512
GATHER_TILE = 512
VMEM_LIMIT = 48 * 1024 * 1024


def _cparams(*sem):
    return pltpu.CompilerParams(dimension_semantics=sem, vmem_limit_bytes=VMEM_LIMIT)


def _split3(c):
    hi = c.astype(bf16).astype(f32)
    r = c - hi
    mid = r.astype(bf16).astype(f32)
    lo = r - mid
    return hi, mid, lo


def _pack_rows(y):
    a = pltpu.bitcast(y[:, :HALF].astype(bf16).astype(f32), i32)
    b = pltpu.bitcast(y[:, HALF:].astype(bf16).astype(f32), i32)
    return lax.shift_right_logical(a, 16) | (b & jnp.int32(-65536))


def _store_slabs(ref, packed):
    n = packed.shape[0]
    for j in range(ROW_CHUNKS):
        ref[pl.ds(j, n, stride=ROW_CHUNKS), :] = packed[:, j * LANES:(j + 1) * LANES]


def _load_slabs(ref, n):
    return jnp.concatenate([ref[pl.ds(j, n, stride=ROW_CHUNKS), :] for j in range(ROW_CHUNKS)], axis=-1)


def _unpack_rows(w):
    lo = pltpu.bitcast(lax.shift_left(w, 16), f32)
    hi = pltpu.bitcast(w & jnp.int32(-65536), f32)
    return lo, hi


def _ada_kernel(c_ref, w_ref, b_ref, o_ref):
    c = c_ref[...]
    w = w_ref[...]
    c_hi = c.astype(bf16)
    c_lo = (c - c_hi.astype(f32)).astype(bf16)
    w_hi = w.astype(bf16)
    w_lo = (w - w_hi.astype(f32)).astype(bf16)
    acc = jnp.dot(c_hi, w_hi, preferred_element_type=f32)
    acc += jnp.dot(c_hi, w_lo, preferred_element_type=f32)
    acc += jnp.dot(c_lo, w_hi, preferred_element_type=f32)
    o_ref[...] = acc + b_ref[...]


def _ada(c, w, b):
    bsz, d = c.shape
    n = w.shape[1]
    tn = 1024
    return pl.pallas_call(
        _ada_kernel,
        grid=(n // tn,),
        in_specs=[pl.BlockSpec((bsz, d), lambda j: (0, 0)),
                  pl.BlockSpec((d, tn), lambda j: (0, j)),
                  pl.BlockSpec((1, tn), lambda j: (0, j))],
        out_specs=pl.BlockSpec((bsz, tn), lambda j: (0, j)),
        out_shape=jax.ShapeDtypeStruct((bsz, n), f32),
        compiler_params=_cparams("parallel"),
        name="ada_mod",
    )(c, w, b.reshape(1, n))


def _inproj_kernel(x_ref, mod_ref, g_ref, wm_ref, wz_ref, bf_ref, tril_ref,
                   proj_ref, cum_ref, carry_ref, *, tiles_per_batch):
    i = pl.program_id(0)
    x = x_ref[...]
    ms = jnp.mean(x * x, axis=-1, keepdims=True)
    y = x * lax.rsqrt(ms + NORM_EPS) * g_ref[...]
    h = (y * (1.0 + mod_ref[0, 1:2, :]) + mod_ref[0, 0:1, :]).astype(bf16)
    n_chunks = proj_ref.shape[1] // 512
    for j in range(n_chunks):
        acc = jnp.dot(h, wm_ref[:, j * 512:(j + 1) * 512], preferred_element_type=f32)
        if j in (0, 3):
            acc = acc * Q_SCALE
        proj_ref[:, j * 512:(j + 1) * 512] = acc.astype(bf16)
    fz = jnp.dot(h, wz_ref[...], preferred_element_type=f32) + bf_ref[...]
    ls = (jnp.minimum(fz, 0.0) - jnp.log(1.0 + jnp.exp(-jnp.abs(fz)))) * LOG2E
    lane = lax.broadcasted_iota(i32, (1, LANES), 1)
    hi, mid, lo = _split3(jnp.where(lane < FOX_HEADS, ls, 0.0))
    parts = hi + pltpu.roll(mid, FOX_HEADS, 1) + pltpu.roll(lo, 2 * FOX_HEADS, 1)
    sums = jnp.dot(tril_ref[...], parts.astype(bf16), preferred_element_type=f32)
    local = sums + pltpu.roll(sums, LANES - FOX_HEADS, 1) + pltpu.roll(sums, LANES - 2 * FOX_HEADS, 1)

    @pl.when(i % tiles_per_batch == 0)
    def _():
        carry_ref[...] = jnp.zeros_like(carry_ref)

    cum = local + carry_ref[0:1, :]
    cum_ref[...] = cum
    tm = x.shape[0]
    carry_ref[0:1, :] = cum[tm - 1:tm, :]


def _inproj(x2, mod, g1, w_main, w_fz, b_fz, seq):
    t, d = x2.shape
    tm = ROW_TILE
    tiles_per_batch = seq // tm
    n_main = w_main.shape[1]
    tril = (jnp.arange(tm)[:, None] >= jnp.arange(tm)[None, :]).astype(bf16)
    return pl.pallas_call(
        functools.partial(_inproj_kernel, tiles_per_batch=tiles_per_batch),
        grid=(t // tm,),
        in_specs=[pl.BlockSpec((tm, d), lambda i: (i, 0)),
                  pl.BlockSpec((1, 6, d), lambda i: (i // tiles_per_batch, 0, 0)),
                  pl.BlockSpec((1, d), lambda i: (0, 0)),
                  pl.BlockSpec((d, n_main), lambda i: (0, 0)),
                  pl.BlockSpec((d, LANES), lambda i: (0, 0)),
                  pl.BlockSpec((1, LANES), lambda i: (0, 0)),
                  pl.BlockSpec((tm, tm), lambda i: (0, 0))],
        out_specs=[pl.BlockSpec((tm, n_main), lambda i: (i, 0)),
                   pl.BlockSpec((tm, LANES), lambda i: (i, 0))],
        out_shape=[jax.ShapeDtypeStruct((t, n_main), bf16),
                   jax.ShapeDtypeStruct((t, LANES), f32)],
        scratch_shapes=[pltpu.VMEM((8, LANES), f32)],
        compiler_params=_cparams("arbitrary"),
        name="inproj",
    )(x2, mod, g1, w_main, w_fz, b_fz, tril)


def _aug(data, lane, low_map, first, last):
    base = 64 if low_map else 0
    out = jnp.zeros_like(data)
    for n, val in enumerate(tuple(first) + tuple(last)):
        out = jnp.where(lane == base + n, val, out)
    keep = (lane < 64) if low_map else (lane >= 64)
    return jnp.where(keep, data, out)


_ONES3 = (1.0, 1.0, 1.0)


def _q_aug(q, lane, low_map, c):
    return _aug(q, lane, low_map, _split3(c), _ONES3).astype(bf16)


def _k_aug(k, lane, low_map, c):
    hi, mid, lo = _split3(c)
    return _aug(k, lane, low_map, _ONES3, (-hi, -mid, -lo)).astype(bf16)


ONES_ROWS = 16


def _vt_aug(vt):
    row = lax.broadcasted_iota(i32, (ONES_ROWS, vt.shape[1]), 0)
    extra = jnp.where(row == 0, 1.0, 0.0).astype(vt.dtype)
    return jnp.concatenate([vt, extra], axis=0).astype(bf16)


def _flash_scratch(tq, acc_rows):
    return ([pltpu.VMEM((ATT_WIDE, tq), f32)] * 2
            + [pltpu.VMEM((acc_rows, tq), f32)] * 2 + [pltpu.VMEM((1, tq), f32)] * 4)


def _flash_sweep(make_q, n_q, ka_s, kb_s, vta_s, vtb_s, mask_ref, tq, finish, scratch):
    s_a, s_b, acc_a, acc_b, m_a, m_b, tmax_a, tmax_b = scratch
    nt = (((1,), (1,)), ((), ()))
    wide = ATT_WIDE
    assert tq == wide
    s_refs, tmax_refs = (s_a, s_b), (tmax_a, tmax_b)
    k_refs, vt_refs, m_refs, acc_refs = (ka_s, kb_s), (vta_s, vtb_s), (m_a, m_b), (acc_a, acc_b)
    items = [(x, c * ATT_COLS) for x in range(2) for c in range(tq // ATT_COLS)]
    steps = [(qi, j) for qi in range(n_q) for j in range(qi + 1)]
    q_cache = {}

    def q_of(qi):
        if qi not in q_cache:
            q_cache[qi] = make_q(qi)
        return q_cache[qi]

    def key_chunks(width):
        return [(k0, min(ATT_KEYS, width - k0)) for k0 in range(0, width, ATT_KEYS)]

    def scores_to(step, item):
        (qi, j), (x, c0) = step, item
        cols = slice(c0, c0 + ATT_COLS)
        q_cols = q_of(qi)[x][cols, :]
        tile_max = None
        for k0, kn in key_chunks(wide):
            s = lax.dot_general(k_refs[x][j * wide + k0:j * wide + k0 + kn, :], q_cols, nt,
                                preferred_element_type=f32)
            s_refs[x][k0:k0 + kn, cols] = s
            c_max = jnp.max(s, axis=0, keepdims=True)
            tile_max = c_max if tile_max is None else jnp.maximum(tile_max, c_max)
        tmax_refs[x][:, cols] = tile_max

    def update(step, item):
        (qi, j), (x, c0) = step, item
        cols = slice(c0, c0 + ATT_COLS)
        diagonal = j == qi
        width = c0 + ATT_COLS if diagonal else wide

        def chunk(k0, kn):
            s = s_refs[x][k0:k0 + kn, cols]
            return s + mask_ref[k0:k0 + kn, cols] if diagonal else s

        if diagonal:
            tile_max = None
            for k0, kn in key_chunks(width):
                c_max = jnp.max(chunk(k0, kn), axis=0, keepdims=True)
                tile_max = c_max if tile_max is None else jnp.maximum(tile_max, c_max)
        else:
            tile_max = tmax_refs[x][:, cols]
        m = m_refs[x][:, cols]
        m_new = jnp.maximum(m, tile_max)
        pv = None
        for k0, kn in key_chunks(width):
            p = jnp.exp2(chunk(k0, kn) - m_new).astype(bf16)
            part = jnp.dot(vt_refs[x][:, j * wide + k0:j * wide + k0 + kn], p,
                           preferred_element_type=f32)
            pv = part if pv is None else pv + part
        acc_refs[x][:, cols] = jnp.exp2(m - m_new) * acc_refs[x][:, cols] + pv
        m_refs[x][:, cols] = m_new

    def reset():
        for x in range(2):
            m_refs[x][...] = jnp.full(m_refs[x].shape, NEG, f32)
            acc_refs[x][...] = jnp.zeros(acc_refs[x].shape, f32)

    reset()
    for item in items:
        scores_to(steps[0], item)
    for n, step in enumerate(steps):
        nxt = steps[n + 1] if n + 1 < len(steps) else None
        for item in items:
            update(step, item)
            if nxt is not None:
                scores_to(nxt, item)
        if step[1] == step[0]:
            finish(step[0], acc_a[...], acc_b[...])
            if nxt is not None:
                reset()


def _fox_kernel(q_ref, k_ref, v_ref, cum_ref, mask_ref, o_ref, ka_s, kb_s, vta_s, vtb_s,
                *flash_scratch, seq, tq):
    hp = pl.program_id(1)
    lane = lax.broadcasted_iota(i32, (1, LANES), 1)

    def head_cums(cm):
        c_a = jnp.sum(jnp.where(lane == 2 * hp, cm, 0.0), axis=-1, keepdims=True)
        c_b = jnp.sum(jnp.where(lane == 2 * hp + 1, cm, 0.0), axis=-1, keepdims=True)
        return c_a, c_b

    for c in range(seq // 512):
        rows = pl.ds(c * 512, 512)
        kk = k_ref[0, rows, :].astype(f32)
        c_a, c_b = head_cums(cum_ref[0, rows, :])
        ka_s[rows, :] = _k_aug(kk, lane, True, c_a)
        kb_s[rows, :] = _k_aug(kk, lane, False, c_b)
        vt = v_ref[0, rows, :].astype(f32).T
        vta_s[:, rows] = _vt_aug(vt[:HEAD_DIM])
        vtb_s[:, rows] = _vt_aug(vt[HEAD_DIM:])

    def make_q(qi):
        rows = pl.ds(qi * tq, tq)
        q = q_ref[0, rows, :].astype(f32)
        c_a, c_b = head_cums(cum_ref[0, rows, :])
        return _q_aug(q, lane, True, c_a), _q_aug(q, lane, False, c_b)

    def finish(qi, acc_a, acc_b):
        o_a = acc_a[:HEAD_DIM] / acc_a[HEAD_DIM:HEAD_DIM + 1]
        o_b = acc_b[:HEAD_DIM] / acc_b[HEAD_DIM:HEAD_DIM + 1]
        o_ref[0, pl.ds(qi * tq, tq), :] = jnp.concatenate([o_a, o_b], axis=0).T.astype(bf16)

    _flash_sweep(make_q, seq // tq, ka_s, kb_s, vta_s, vtb_s, mask_ref, tq, finish, flash_scratch)


def _tail_mask(diag):
    diag_t = jnp.swapaxes(diag, -1, -2)
    pad = [(0, 0)] * (diag.ndim - 2) + [(ATT_WIDE - diag.shape[-1], 0), (0, 0)]
    return jnp.pad(diag_t, pad)


def _fox_attention(proj3, cum3):
    bsz, seq, _ = proj3.shape
    tq, tk = ATT_TQ, ATT_WIDE
    n_pairs = FOX_HEADS // 2
    r = jnp.arange(tq)
    mask = _tail_mask(jnp.where(r[:, None] >= r[None, :], 0.0, NEG).astype(f32))
    return pl.pallas_call(
        functools.partial(_fox_kernel, seq=seq, tq=tq),
        grid=(bsz, n_pairs),
        in_specs=[pl.BlockSpec((1, seq, LANES), lambda b, h: (b, 0, h)),
                  pl.BlockSpec((1, seq, LANES), lambda b, h: (b, 0, 4 + h)),
                  pl.BlockSpec((1, seq, LANES), lambda b, h: (b, 0, 8 + h)),
                  pl.BlockSpec((1, seq, LANES), lambda b, h: (b, 0, 0)),
                  pl.BlockSpec((tk, tq), lambda b, h: (0, 0))],
        out_specs=pl.BlockSpec((1, seq, LANES), lambda b, h: (b, 0, h)),
        out_shape=jax.ShapeDtypeStruct((bsz, seq, FOX_WIDTH), bf16),
        scratch_shapes=[pltpu.VMEM((seq, LANES), bf16), pltpu.VMEM((seq, LANES), bf16),
                        pltpu.VMEM((HEAD_DIM + ONES_ROWS, seq), bf16),
                        pltpu.VMEM((HEAD_DIM + ONES_ROWS, seq), bf16)]
        + _flash_scratch(tq, HEAD_DIM + ONES_ROWS),
        compiler_params=_cparams("parallel", "parallel"),
        name="fox_attn",
    )(proj3, proj3, proj3, cum3, mask)


def _diff_kernel(slope_ref, q_ref, k_ref, v_ref, mask_ref, lamv_ref, g_ref, o_ref,
                 ka_s, kb_s, vt_s, *flash_scratch, seq, tq, lam_init):
    h = pl.program_id(1)
    lane = lax.broadcasted_iota(i32, (1, LANES), 1)
    slope = slope_ref[h]

    def pos_bias(start, n):
        pos = (start + lax.broadcasted_iota(i32, (n, 1), 0)).astype(f32)
        return -(slope * pos)

    for c in range(seq // 512):
        rows = pl.ds(c * 512, 512)
        kk = k_ref[0, rows, :].astype(f32)
        cb = pos_bias(c * 512, 512)
        ka_s[rows, :] = _k_aug(kk, lane, True, cb)
        kb_s[rows, :] = _k_aug(kk, lane, False, cb)
        vt_s[:, rows] = _vt_aug(v_ref[0, rows, :].astype(f32).T)

    def make_q(qi):
        q = q_ref[0, pl.ds(qi * tq, tq), :].astype(f32)
        cq = pos_bias(qi * tq, tq)
        return _q_aug(q, lane, True, cq), _q_aug(q, lane, False, cq)

    def finish(qi, acc_a, acc_b):
        lv = lamv_ref[...]
        s1 = jnp.sum(lv[0:1, :] * lv[1:2, :], axis=-1, keepdims=True)
        s2 = jnp.sum(lv[2:3, :] * lv[3:4, :], axis=-1, keepdims=True)
        lam = jnp.exp(s1) - jnp.exp(s2) + lam_init
        o_a = acc_a[:LANES] / acc_a[LANES:LANES + 1]
        o_b = acc_b[:LANES] / acc_b[LANES:LANES + 1]
        d = (o_a - lam * o_b).T
        y = d * lax.rsqrt(jnp.mean(d * d, axis=-1, keepdims=True) + SUBLN_EPS) * g_ref[...]
        o_ref[0, pl.ds(qi * tq, tq), :] = (y * (1.0 - lam_init)).astype(bf16)

    _flash_sweep(make_q, seq // tq, ka_s, kb_s, vt_s, vt_s, mask_ref.at[0], tq, finish, flash_scratch)


def _diff_attention(proj3, lam_vecs, subln_g, lam_init):
    bsz, seq, _ = proj3.shape
    tq, tk = ATT_TQ, ATT_WIDE
    slopes = jnp.asarray([2.0 ** (-8.0 * (i + 1) / DIFF_HEADS) for i in range(DIFF_HEADS)], f32) * LOG2E
    r = jnp.arange(tq)
    tq_i, tk_i = r[:, None], r[None, :]
    chunk_ok = (tq_i // CHUNK) >= (tk_i // CHUNK)
    ahead = jnp.maximum(tk_i - tq_i, 0).astype(f32)
    mask = _tail_mask(jnp.where(chunk_ok[None], -2.0 * slopes[:, None, None] * ahead[None], NEG).astype(f32))
    grid_spec = pltpu.PrefetchScalarGridSpec(
        num_scalar_prefetch=1,
        grid=(bsz, DIFF_HEADS),
        in_specs=[pl.BlockSpec((1, seq, LANES), lambda b, h, s: (b, 0, 12 + h)),
                  pl.BlockSpec((1, seq, LANES), lambda b, h, s: (b, 0, 16 + h)),
                  pl.BlockSpec((1, seq, LANES), lambda b, h, s: (b, 0, 20 + h)),
                  pl.BlockSpec((1, tk, tq), lambda b, h, s: (h, 0, 0)),
                  pl.BlockSpec((8, LANES), lambda b, h, s: (0, 0)),
                  pl.BlockSpec((1, LANES), lambda b, h, s: (0, 0))],
        out_specs=pl.BlockSpec((1, seq, LANES), lambda b, h, s: (b, 0, h)),
        scratch_shapes=[pltpu.VMEM((seq, LANES), bf16), pltpu.VMEM((seq, LANES), bf16),
                        pltpu.VMEM((LANES + ONES_ROWS, seq), bf16)]
        + _flash_scratch(tq, LANES + ONES_ROWS),
    )
    return pl.pallas_call(
        functools.partial(_diff_kernel, seq=seq, tq=tq, lam_init=lam_init),
        grid_spec=grid_spec,
        out_shape=jax.ShapeDtypeStruct((bsz, seq, DIFF_WIDTH), bf16),
        compiler_params=_cparams("parallel", "parallel"),
        name="diff_attn",
    )(slopes, proj3, proj3, proj3, mask, lam_vecs, subln_g)


ROUTER_ROWS = 8 + N_EXPERTS


def _outproj_kernel(fox_ref, diff_ref, x_ref, mod_ref, g_ref, wo_ref, wr_ref,
                    br_ref, triu_ref, x1_ref, hp_ref, route_ref, wcol_ref, cnt_ref):
    tm = x_ref.shape[0]
    y = jnp.dot(fox_ref[...], wo_ref[0:FOX_WIDTH, :], preferred_element_type=f32)
    y += jnp.dot(diff_ref[...], wo_ref[FOX_WIDTH:, :], preferred_element_type=f32)
    x1 = x_ref[...] + mod_ref[0, 2:3, :] * y
    x1_ref[...] = x1
    ms = jnp.mean(x1 * x1, axis=-1, keepdims=True)
    h2 = (x1 * lax.rsqrt(ms + NORM_EPS) * g_ref[...]) * (1.0 + mod_ref[0, 4:5, :]) + mod_ref[0, 3:4, :]
    _store_slabs(hp_ref, _pack_rows(h2))

    nt = (((1,), (1,)), ((), ()))
    h_hi = h2.astype(bf16)
    h_lo = (h2 - h_hi.astype(f32)).astype(bf16)
    stacked = lax.dot_general(wr_ref[...], h_hi, nt, preferred_element_type=f32)
    logits = stacked[:ROUTER_ROWS] + stacked[ROUTER_ROWS:]
    logits += lax.dot_general(wr_ref[:ROUTER_ROWS, :], h_lo, nt, preferred_element_type=f32)
    logits = logits + br_ref[...]

    row8 = lax.broadcasted_iota(i32, (8, tm), 0)
    gl = jnp.where(row8 < N_GROUPS, logits[0:8, :], NEG)
    gmax = jnp.max(gl, axis=0, keepdims=True)
    grp = jnp.min(jnp.where(gl == gmax, row8, 8), axis=0, keepdims=True)
    p_g = 1.0 / jnp.sum(jnp.exp(gl - gmax), axis=0, keepdims=True)
    sel = logits[8:16, :]
    for g in range(1, N_GROUPS):
        sel = jnp.where(grp == g, logits[8 + 8 * g:16 + 8 * g, :], sel)
    v1 = jnp.max(sel, axis=0, keepdims=True)
    i1 = jnp.min(jnp.where(sel == v1, row8, 8), axis=0, keepdims=True)
    sel2 = jnp.where(row8 == i1, -jnp.inf, sel)
    v2 = jnp.max(sel2, axis=0, keepdims=True)
    i2 = jnp.min(jnp.where(sel2 == v2, row8, 8), axis=0, keepdims=True)
    e21 = jnp.exp(v2 - v1)
    w1 = p_g / (1.0 + e21)
    w2 = w1 * e21
    e1 = grp * EXPERTS_PER_GROUP + i1
    e2 = grp * EXPERTS_PER_GROUP + i2

    row32 = lax.broadcasted_iota(i32, (N_EXPERTS, tm), 0)
    oh1 = row32 == e1
    oh2 = row32 == e2
    both = jnp.where(oh1 | oh2, 1.0, 0.0)
    prefix = jnp.dot(both.astype(bf16), triu_ref[...], preferred_element_type=f32)
    r1 = jnp.sum(jnp.where(oh1, prefix, 0.0), axis=0, keepdims=True).astype(i32)
    r2 = jnp.sum(jnp.where(oh2, prefix, 0.0), axis=0, keepdims=True).astype(i32)
    route = jnp.where(row8 == 0, e1, jnp.where(row8 == 1, e2,
                      jnp.where(row8 == 2, r1, jnp.where(row8 == 3, r2, 0))))
    route_ref[...] = route
    cnt = jnp.sum(both, axis=1, keepdims=True)
    cnt_ref[0] = jnp.broadcast_to(cnt, (N_EXPERTS, LANES)).astype(i32)
    row128 = lax.broadcasted_iota(i32, (LANES, tm), 0)
    w_rows = jnp.where(row128 == 0, w1, jnp.where(row128 == 1, w2, 0.0))
    wcol_ref[...] = w_rows.T


def _outproj(fox2, diff2, x2, mod, g2, w_o, wr_parts, b_r, seq):
    t, d = x2.shape
    tm = ROW_TILE
    tiles_per_batch = seq // tm
    n_tiles = t // tm
    triu = (jnp.arange(tm)[:, None] < jnp.arange(tm)[None, :]).astype(bf16)
    row = lambda i: (i, 0)
    const = lambda i: (0, 0)
    return pl.pallas_call(
        _outproj_kernel,
        grid=(n_tiles,),
        in_specs=[pl.BlockSpec((tm, FOX_WIDTH), row),
                  pl.BlockSpec((tm, DIFF_WIDTH), row),
                  pl.BlockSpec((tm, d), row),
                  pl.BlockSpec((1, 6, d), lambda i: (i // tiles_per_batch, 0, 0)),
                  pl.BlockSpec((1, d), const),
                  pl.BlockSpec((d, d), const),
                  pl.BlockSpec((2 * ROUTER_ROWS, d), const),
                  pl.BlockSpec((ROUTER_ROWS, 1), const),
                  pl.BlockSpec((tm, tm), const)],
        out_specs=[pl.BlockSpec((tm, d), row),
                   pl.BlockSpec((ROW_CHUNKS * tm, LANES), row),
                   pl.BlockSpec((8, tm), lambda i: (0, i)),
                   pl.BlockSpec((tm, LANES), row),
                   pl.BlockSpec((1, N_EXPERTS, LANES), lambda i: (i, 0, 0))],
        out_shape=[jax.ShapeDtypeStruct((t, d), f32),
                   jax.ShapeDtypeStruct((ROW_CHUNKS * t, LANES), i32),
                   jax.ShapeDtypeStruct((8, t), i32),
                   jax.ShapeDtypeStruct((t, LANES), f32),
                   jax.ShapeDtypeStruct((n_tiles, N_EXPERTS, LANES), i32)],
        compiler_params=_cparams("parallel"),
        name="outproj_router",
    )(fox2, diff2, x2, mod, g2, w_o, wr_parts, b_r, triu)


def _dispatch_kernel(dest_ref, h_ref, xs_in_ref, xs_ref, sem, *, tm):
    del xs_in_ref
    i = pl.program_id(0)
    base = i * (2 * tm)

    def row_copy(r, k):
        d = dest_ref[base + 2 * r + k]
        src = h_ref.at[pl.ds(pl.multiple_of(ROW_CHUNKS * r, ROW_CHUNKS), ROW_CHUNKS), :]
        return pltpu.make_async_copy(src, xs_ref.at[d], sem)

    def issue(r, _):
        row_copy(r, 0).start()
        row_copy(r, 1).start()
        return 0

    lax.fori_loop(0, tm, issue, 0, unroll=8)

    def drain(r, _):
        row_copy(r, 0).wait()
        row_copy(r, 1).wait()
        return 0

    lax.fori_loop(0, tm, drain, 0, unroll=8)


def _dispatch(dest, h_packed, n_slots):
    t = h_packed.shape[0] // ROW_CHUNKS
    tm = DISPATCH_TILE
    xs0 = jnp.zeros((n_slots, ROW_CHUNKS, LANES), i32)
    grid_spec = pltpu.PrefetchScalarGridSpec(
        num_scalar_prefetch=1,
        grid=(t // tm,),
        in_specs=[pl.BlockSpec((ROW_CHUNKS * tm, LANES), lambda i, d: (i, 0)),
                  pl.BlockSpec(memory_space=pl.ANY)],
        out_specs=pl.BlockSpec(memory_space=pl.ANY),
        scratch_shapes=[pltpu.SemaphoreType.DMA(())],
    )
    return pl.pallas_call(
        functools.partial(_dispatch_kernel, tm=tm),
        grid_spec=grid_spec,
        out_shape=jax.ShapeDtypeStruct((n_slots, ROW_CHUNKS, LANES), i32),
        input_output_aliases={2: 0},
        compiler_params=_cparams("arbitrary"),
        name="moe_dispatch",
    )(dest, h_packed, xs0)


def _experts_kernel(be_ref, nu_ref, xs_ref, wg_ref, wu_ref, wd_ref, y_ref, wg_s, wu_s, wd_s):
    i = pl.program_id(0)
    prev = be_ref[jnp.maximum(i - 1, 0)]
    fresh = jnp.logical_or(i == 0, be_ref[i] != prev)

    @pl.when(jnp.logical_and(fresh, i < nu_ref[0]))
    def _():
        wg_s[...] = wg_ref[0].astype(bf16)
        wu_s[...] = wu_ref[0].astype(bf16)
        wd_s[...] = wd_ref[0].astype(bf16)

    @pl.when(i < nu_ref[0])
    def _():
        lo, hi = _unpack_rows(_load_slabs(xs_ref, MOE_BLOCK))
        xb = jnp.concatenate([lo.astype(bf16), hi.astype(bf16)], axis=-1)
        g = jnp.dot(xb, wg_s[...], preferred_element_type=f32)
        u = jnp.dot(xb, wu_s[...], preferred_element_type=f32)
        hid = (g * (1.0 / (1.0 + jnp.exp(-g)))) * u
        y = jnp.dot(hid.astype(bf16), wd_s[...], preferred_element_type=f32)
        _store_slabs(y_ref, _pack_rows(y))

    @pl.when(i >= nu_ref[0])
    def _():
        y_ref[...] = jnp.zeros_like(y_ref)


def _experts(block_e, n_used, xs, w_gate, w_up, w_down):
    n_slots = xs.shape[0]
    bk = MOE_BLOCK
    grid_spec = pltpu.PrefetchScalarGridSpec(
        num_scalar_prefetch=2,
        grid=(n_slots // bk,),
        in_specs=[pl.BlockSpec((ROW_CHUNKS * bk, LANES), lambda i, be, nu: (i, 0)),
                  pl.BlockSpec((1, D_MODEL, D_EXPERT), lambda i, be, nu: (be[i], 0, 0)),
                  pl.BlockSpec((1, D_MODEL, D_EXPERT), lambda i, be, nu: (be[i], 0, 0)),
                  pl.BlockSpec((1, D_EXPERT, D_MODEL), lambda i, be, nu: (be[i], 0, 0))],
        out_specs=pl.BlockSpec((ROW_CHUNKS * bk, LANES), lambda i, be, nu: (i, 0)),
        scratch_shapes=[pltpu.VMEM((D_MODEL, D_EXPERT), bf16),
                        pltpu.VMEM((D_MODEL, D_EXPERT), bf16),
                        pltpu.VMEM((D_EXPERT, D_MODEL), bf16)],
    )
    return pl.pallas_call(
        _experts_kernel,
        grid_spec=grid_spec,
        out_shape=jax.ShapeDtypeStruct((ROW_CHUNKS * n_slots, LANES), i32),
        compiler_params=_cparams("arbitrary"),
        name="moe_experts",
    )(block_e, n_used, xs.reshape(-1, LANES), w_gate, w_up, w_down)


def _combine_kernel(dest_ref, y_ref, x1_ref, wcol_ref, mod_ref, g_ref, o_ref, buf, sems, *, tm):
    i = pl.program_id(0)
    n = pl.num_programs(0)

    def row_copy(tile, slot, r, k):
        d = dest_ref[tile * (2 * tm) + 2 * r + k]
        dst = buf.at[slot, k, pl.ds(pl.multiple_of(ROW_CHUNKS * r, ROW_CHUNKS), ROW_CHUNKS), :]
        return pltpu.make_async_copy(y_ref.at[d], dst, sems.at[slot])

    def issue_tile(tile, slot):
        def body(r, _):
            row_copy(tile, slot, r, 0).start()
            row_copy(tile, slot, r, 1).start()
            return 0
        lax.fori_loop(0, tm, body, 0, unroll=8)

    @pl.when(i == 0)
    def _():
        issue_tile(0, 0)

    @pl.when(i + 1 < n)
    def _():
        issue_tile(i + 1, (i + 1) % 2)

    slot = i % 2

    def drain(r, _):
        row_copy(i, slot, r, 0).wait()
        row_copy(i, slot, r, 1).wait()
        return 0

    lax.fori_loop(0, tm, drain, 0, unroll=8)

    wc = wcol_ref[...]
    w0 = wc[:, 0:1]
    w1 = wc[:, 1:2]
    lo0, hi0 = _unpack_rows(_load_slabs(buf.at[slot, 0], tm))
    lo1, hi1 = _unpack_rows(_load_slabs(buf.at[slot, 1], tm))
    moe = jnp.concatenate([w0 * lo0 + w1 * lo1, w0 * hi0 + w1 * hi1], axis=-1)
    x = x1_ref[...] + mod_ref[0, 5:6, :] * moe
    ms = jnp.mean(x * x, axis=-1, keepdims=True)
    o_ref[...] = x * lax.rsqrt(ms + NORM_EPS) * g_ref[...]


def _combine(dest, y_packed, x1, wcol, mod, g_f, seq):
    t, d = x1.shape
    tm = GATHER_TILE
    tiles_per_batch = seq // tm
    grid_spec = pltpu.PrefetchScalarGridSpec(
        num_scalar_prefetch=1,
        grid=(t // tm,),
        in_specs=[pl.BlockSpec(memory_space=pl.ANY),
                  pl.BlockSpec((tm, d), lambda i, ds: (i, 0)),
                  pl.BlockSpec((tm, LANES), lambda i, ds: (i, 0)),
                  pl.BlockSpec((1, 6, d), lambda i, ds: (i // tiles_per_batch, 0, 0)),
                  pl.BlockSpec((1, d), lambda i, ds: (0, 0))],
        out_specs=pl.BlockSpec((tm, d), lambda i, ds: (i, 0)),
        scratch_shapes=[pltpu.VMEM((2, 2, ROW_CHUNKS * tm, LANES), i32),
                        pltpu.SemaphoreType.DMA((2,))],
    )
    return pl.pallas_call(
        functools.partial(_combine_kernel, tm=tm),
        grid_spec=grid_spec,
        out_shape=jax.ShapeDtypeStruct((t, d), f32),
        compiler_params=_cparams("arbitrary"),
        name="moe_combine",
    )(dest, y_packed.reshape(-1, ROW_CHUNKS, LANES), x1, wcol, mod, g_f)


def _route_tables(route, cnt, tm):
    n_tiles = cnt.shape[0]
    t = route.shape[1]
    bk = MOE_BLOCK
    n_blocks = (2 * t) // bk + N_EXPERTS
    ti = jnp.arange(n_tiles)
    ei = jnp.arange(N_EXPERTS)
    tile_base = jnp.sum(jnp.where((ti[:, None] > ti[None, :])[:, :, None], cnt[None], 0), axis=1)
    total = jnp.sum(cnt, axis=0)
    padded = (total + bk - 1) // bk * bk
    pends = jnp.sum(jnp.where(ei[None, :] <= ei[:, None], padded[None, :], 0), axis=1)
    base = (pends - padded)[None, :] + tile_base
    base_tok = jnp.broadcast_to(base[:, None, :], (n_tiles, tm, N_EXPERTS)).reshape(t, N_EXPERTS)

    def slot_base(e):
        return jnp.sum(jnp.where(ei[None, :] == e[:, None], base_tok, 0), axis=1)

    d0 = slot_base(route[0]) + route[2]
    d1 = slot_base(route[1]) + route[3]
    dest = jnp.stack([d0, d1], axis=1).reshape(-1).astype(i32)
    block_start = jnp.arange(n_blocks, dtype=i32) * bk
    block_e = jnp.clip(jnp.sum(pends[None, :] <= block_start[:, None], axis=1),
                       0, N_EXPERTS - 1).astype(i32)
    n_used = (pends[-1] // bk).astype(i32).reshape(1)
    return dest, block_e, n_used, n_blocks * bk


def _layer(x2, c, seq, l, ada_w, ada_b, norm1_g, w_in, b_f, lam_q1, lam_k1, lam_q2, lam_k2,
           subln_g, w_o, norm2_g, w_rg, b_rg, w_re, b_re, w_gate, w_up, w_down):
    t, d = x2.shape
    bsz = t // seq
    lam_init = 0.8 - 0.6 * math.exp(-0.3 * l)
    mod = _ada(c, ada_w, ada_b).reshape(bsz, 6, d)

    z0 = 3 * FOX_WIDTH
    w_main = jnp.concatenate([w_in[:, :z0], w_in[:, z0 + FOX_HEADS:]], axis=1).astype(bf16)
    w_fz = jnp.pad(w_in[:, z0:z0 + FOX_HEADS], ((0, 0), (0, LANES - FOX_HEADS))).astype(bf16)
    b_fz = jnp.pad(b_f, (0, LANES - FOX_HEADS)).reshape(1, LANES)
    proj, cum = _inproj(x2, mod, norm1_g.reshape(1, d), w_main, w_fz, b_fz, seq)
    proj3 = proj.reshape(bsz, seq, -1)

    fox = _fox_attention(proj3, cum.reshape(bsz, seq, LANES))
    lam_vecs = jnp.pad(jnp.stack([lam_q1, lam_k1, lam_q2, lam_k2]).astype(f32),
                       ((0, 4), (0, LANES - HEAD_DIM)))
    diff = _diff_attention(proj3, lam_vecs, subln_g.reshape(1, LANES), lam_init)

    w_r = jnp.concatenate([w_rg.T, jnp.zeros((8 - N_GROUPS, d), f32), w_re.T], axis=0)
    wr_hi = w_r.astype(bf16)
    wr_lo = (w_r - wr_hi.astype(f32)).astype(bf16)
    b_r = jnp.concatenate([b_rg, jnp.zeros((8 - N_GROUPS,), f32), b_re]).reshape(ROUTER_ROWS, 1)
    x1, h_packed, route, wcol, cnt = _outproj(
        fox.reshape(t, FOX_WIDTH), diff.reshape(t, DIFF_WIDTH), x2, mod, norm2_g.reshape(1, d),
        w_o.astype(bf16), jnp.concatenate([wr_hi, wr_lo], axis=0), b_r, seq)

    dest, block_e, n_used, n_slots = _route_tables(route, cnt[:, :, 0], ROW_TILE)
    xs = _dispatch(dest, h_packed, n_slots)
    y_packed = _experts(block_e, n_used, xs, w_gate, w_up, w_down)
    return dest, y_packed, x1, wcol, mod


def kernel(x, c, ada_w, ada_b, norm1_g, w_in, b_f, lam_q1, lam_k1, lam_q2, lam_k2, subln_g, w_o,
           norm2_g, w_rg, b_rg, w_re, b_re, w_gate, w_up, w_down, norm_f_g):
    bsz, seq, d = x.shape
    depth = ada_w.shape[0]
    assert depth == 1 and d == D_MODEL and seq % ROW_TILE == 0 and seq % ATT_WIDE == 0
    x2 = x.reshape(bsz * seq, d)
    dest, y_packed, x1, wcol, mod = _layer(
        x2, c, seq, 0, ada_w[0], ada_b[0], norm1_g[0], w_in[0], b_f[0], lam_q1[0], lam_k1[0],
        lam_q2[0], lam_k2[0], subln_g[0], w_o[0], norm2_g[0], w_rg[0], b_rg[0], w_re[0], b_re[0],
        w_gate[0], w_up[0], w_down[0])
    out = _combine(dest, y_packed, x1, wcol, mod, norm_f_g.reshape(1, d), seq)
    return out.reshape(bsz, seq, d)
```

```python
import functools
import math

import jax
import jax.numpy as jnp
from jax import lax
from jax.experimental import pallas as pl
from jax.experimental.pallas import tpu as pltpu

f32 = jnp.float32
bf16 = jnp.bfloat16
i32 = jnp.int32

D_MODEL = 1024
HEAD_DIM = 64
FOX_HEADS = 8
FOX_WIDTH = FOX_HEADS * HEAD_DIM
DIFF_HEADS = 4
DIFF_QK_WIDTH = DIFF_HEADS * 2 * HEAD_DIM
DIFF_WIDTH = DIFF_HEADS * 2 * HEAD_DIM
CHUNK = 64
N_GROUPS = 4
EXPERTS_PER_GROUP = 8
N_EXPERTS = N_GROUPS * EXPERTS_PER_GROUP
D_EXPERT = 512
NORM_EPS = 1e-6
SUBLN_EPS = 1e-5

LANES = 128
LOG2E = 1.4426950408889634
Q_SCALE = HEAD_DIM ** -0.5 * LOG2E
NEG = -1e30
HALF = D_MODEL // 2
ROW_CHUNKS = HALF // LANES

ROW_TILE = 512
ATT_TQ = 1024
ATT_WIDE = 1024
ATT_COLS = 256
ATT_KEYS = 1024
MOE_BLOCK = 256
DISPATCH_TILE = 512
GATHER_TILE = 512
VMEM_LIMIT = 48 * 1024 * 1024


def _cparams(*sem):
    return pltpu.CompilerParams(dimension_semantics=sem, vmem_limit_bytes=VMEM_LIMIT)


def _split3(c):
    hi = c.astype(bf16).astype(f32)
    r = c - hi
    mid = r.astype(bf16).astype(f32)
    lo = r - mid
    return hi, mid, lo


def _pack_rows(y):
    a = pltpu.bitcast(y[:, :HALF].astype(bf16).astype(f32), i32)
    b = pltpu.bitcast(y[:, HALF:].astype(bf16).astype(f32), i32)
    return lax.shift_right_logical(a, 16) | (b & jnp.int32(-65536))


def _store_slabs(ref, packed):
    n = packed.shape[0]
    for j in range(ROW_CHUNKS):
        ref[pl.ds(j, n, stride=ROW_CHUNKS), :] = packed[:, j * LANES:(j + 1) * LANES]


def _load_slabs(ref, n):
    return jnp.concatenate([ref[pl.ds(j, n, stride=ROW_CHUNKS), :] for j in range(ROW_CHUNKS)], axis=-1)


def _unpack_rows(w):
    lo = pltpu.bitcast(lax.shift_left(w, 16), f32)
    hi = pltpu.bitcast(w & jnp.int32(-65536), f32)
    return lo, hi


def _ada_kernel(c_ref, w_ref, b_ref, o_ref):
    c = c_ref[...]
    w = w_ref[...]
    c_hi = c.astype(bf16)
    c_lo = (c - c_hi.astype(f32)).astype(bf16)
    w_hi = w.astype(bf16)
    w_lo = (w - w_hi.astype(f32)).astype(bf16)
    acc = jnp.dot(c_hi, w_hi, preferred_element_type=f32)
    acc += jnp.dot(c_hi, w_lo, preferred_element_type=f32)
    acc += jnp.dot(c_lo, w_hi, preferred_element_type=f32)
    o_ref[...] = acc + b_ref[...]


def _ada(c, w, b):
    bsz, d = c.shape
    n = w.shape[1]
    tn = 1024
    return pl.pallas_call(
        _ada_kernel,
        grid=(n // tn,),
        in_specs=[pl.BlockSpec((bsz, d), lambda j: (0, 0)),
                  pl.BlockSpec((d, tn), lambda j: (0, j)),
                  pl.BlockSpec((1, tn), lambda j: (0, j))],
        out_specs=pl.BlockSpec((bsz, tn), lambda j: (0, j)),
        out_shape=jax.ShapeDtypeStruct((bsz, n), f32),
        compiler_params=_cparams("parallel"),
        name="ada_mod",
    )(c, w, b.reshape(1, n))


def _inproj_kernel(x_ref, mod_ref, g_ref, wm_ref, wz_ref, bf_ref, tril_ref,
                   proj_ref, cum_ref, carry_ref, *, tiles_per_batch):
    i = pl.program_id(0)
    x = x_ref[...]
    ms = jnp.mean(x * x, axis=-1, keepdims=True)
    y = x * lax.rsqrt(ms + NORM_EPS) * g_ref[...]
    h = (y * (1.0 + mod_ref[0, 1:2, :]) + mod_ref[0, 0:1, :]).astype(bf16)
    n_chunks = proj_ref.shape[1] // 512
    for j in range(n_chunks):
        acc = jnp.dot(h, wm_ref[:, j * 512:(j + 1) * 512], preferred_element_type=f32)
        if j in (0, 3):
            acc = acc * Q_SCALE
        proj_ref[:, j * 512:(j + 1) * 512] = acc.astype(bf16)
    fz = jnp.dot(h, wz_ref[...], preferred_element_type=f32) + bf_ref[...]
    ls = (jnp.minimum(fz, 0.0) - jnp.log(1.0 + jnp.exp(-jnp.abs(fz)))) * LOG2E
    lane = lax.broadcasted_iota(i32, (1, LANES), 1)
    hi, mid, lo = _split3(jnp.where(lane < FOX_HEADS, ls, 0.0))
    parts = hi + pltpu.roll(mid, FOX_HEADS, 1) + pltpu.roll(lo, 2 * FOX_HEADS, 1)
    sums = jnp.dot(tril_ref[...], parts.astype(bf16), preferred_element_type=f32)
    local = sums + pltpu.roll(sums, LANES - FOX_HEADS, 1) + pltpu.roll(sums, LANES - 2 * FOX_HEADS, 1)

    @pl.when(i % tiles_per_batch == 0)
    def _():
        carry_ref[...] = jnp.zeros_like(carry_ref)

    cum = local + carry_ref[0:1, :]
    cum_ref[...] = cum
    tm = x.shape[0]
    carry_ref[0:1, :] = cum[tm - 1:tm, :]


def _inproj(x2, mod, g1, w_main, w_fz, b_fz, seq):
    t, d = x2.shape
    tm = ROW_TILE
    tiles_per_batch = seq // tm
    n_main = w_main.shape[1]
    tril = (jnp.arange(tm)[:, None] >= jnp.arange(tm)[None, :]).astype(bf16)
    return pl.pallas_call(
        functools.partial(_inproj_kernel, tiles_per_batch=tiles_per_batch),
        grid=(t // tm,),
        in_specs=[pl.BlockSpec((tm, d), lambda i: (i, 0)),
                  pl.BlockSpec((1, 6, d), lambda i: (i // tiles_per_batch, 0, 0)),
                  pl.BlockSpec((1, d), lambda i: (0, 0)),
                  pl.BlockSpec((d, n_main), lambda i: (0, 0)),
                  pl.BlockSpec((d, LANES), lambda i: (0, 0)),
                  pl.BlockSpec((1, LANES), lambda i: (0, 0)),
                  pl.BlockSpec((tm, tm), lambda i: (0, 0))],
        out_specs=[pl.BlockSpec((tm, n_main), lambda i: (i, 0)),
                   pl.BlockSpec((tm, LANES), lambda i: (i, 0))],
        out_shape=[jax.ShapeDtypeStruct((t, n_main), bf16),
                   jax.ShapeDtypeStruct((t, LANES), f32)],
        scratch_shapes=[pltpu.VMEM((8, LANES), f32)],
        compiler_params=_cparams("arbitrary"),
        name="inproj",
    )(x2, mod, g1, w_main, w_fz, b_fz, tril)


def _aug(data, lane, low_map, first, last):
    base = 64 if low_map else 0
    out = jnp.zeros_like(data)
    for n, val in enumerate(tuple(first) + tuple(last)):
        out = jnp.where(lane == base + n, val, out)
    keep = (lane < 64) if low_map else (lane >= 64)
    return jnp.where(keep, data, out)


_ONES3 = (1.0, 1.0, 1.0)


def _q_aug(q, lane, low_map, c):
    return _aug(q, lane, low_map, _split3(c), _ONES3).astype(bf16)


def _k_aug(k, lane, low_map, c):
    hi, mid, lo = _split3(c)
    return _aug(k, lane, low_map, _ONES3, (-hi, -mid, -lo)).astype(bf16)


ONES_ROWS = 16


def _vt_aug(vt):
    row = lax.broadcasted_iota(i32, (ONES_ROWS, vt.shape[1]), 0)
    extra = jnp.where(row == 0, 1.0, 0.0).astype(vt.dtype)
    return jnp.concatenate([vt, extra], axis=0).astype(bf16)


def _flash_scratch(tq, acc_rows):
    return ([pltpu.VMEM((ATT_WIDE, tq), f32)] * 2
            + [pltpu.VMEM((acc_rows, tq), f32)] * 2 + [pltpu.VMEM((1, tq), f32)] * 4)


def _flash_sweep(make_q, n_q, ka_s, kb_s, vta_s, vtb_s, mask_ref, tq, finish, scratch):
    s_a, s_b, acc_a, acc_b, m_a, m_b, tmax_a, tmax_b = scratch
    nt = (((1,), (1,)), ((), ()))
    wide = ATT_WIDE
    assert tq == wide
    s_refs, tmax_refs = (s_a, s_b), (tmax_a, tmax_b)
    k_refs, vt_refs, m_refs, acc_refs = (ka_s, kb_s), (vta_s, vtb_s), (m_a, m_b), (acc_a, acc_b)
    items = [(x, c * ATT_COLS) for x in range(2) for c in range(tq // ATT_COLS)]
    steps = [(qi, j) for qi in range(n_q) for j in range(qi + 1)]
    q_cache = {}

    def q_of(qi):
        if qi not in q_cache:
            q_cache[qi] = make_q(qi)
        return q_cache[qi]

    def key_chunks(width):
        return [(k0, min(ATT_KEYS, width - k0)) for k0 in range(0, width, ATT_KEYS)]

    def scores_to(step, item):
        (qi, j), (x, c0) = step, item
        cols = slice(c0, c0 + ATT_COLS)
        q_cols = q_of(qi)[x][cols, :]
        tile_max = None
        for k0, kn in key_chunks(wide):
            s = lax.dot_general(k_refs[x][j * wide + k0:j * wide + k0 + kn, :], q_cols, nt,
                                preferred_element_type=f32)
            s_refs[x][k0:k0 + kn, cols] = s
            c_max = jnp.max(s, axis=0, keepdims=True)
            tile_max = c_max if tile_max is None else jnp.maximum(tile_max, c_max)
        tmax_refs[x][:, cols] = tile_max

    def update(step, item):
        (qi, j), (x, c0) = step, item
        cols = slice(c0, c0 + ATT_COLS)
        diagonal = j == qi
        width = c0 + ATT_COLS if diagonal else wide

        def chunk(k0, kn):
            s = s_refs[x][k0:k0 + kn, cols]
            return s + mask_ref[k0:k0 + kn, cols] if diagonal else s

        if diagonal:
            tile_max = None
            for k0, kn in key_chunks(width):
                c_max = jnp.max(chunk(k0, kn), axis=0, keepdims=True)
                tile_max = c_max if tile_max is None else jnp.maximum(tile_max, c_max)
        else:
            tile_max = tmax_refs[x][:, cols]
        m = m_refs[x][:, cols]
        m_new = jnp.maximum(m, tile_max)
        pv = None
        for k0, kn in key_chunks(width):
            p = jnp.exp2(chunk(k0, kn) - m_new).astype(bf16)
            part = jnp.dot(vt_refs[x][:, j * wide + k0:j * wide + k0 + kn], p,
                           preferred_element_type=f32)
            pv = part if pv is None else pv + part
        acc_refs[x][:, cols] = jnp.exp2(m - m_new) * acc_refs[x][:, cols] + pv
        m_refs[x][:, cols] = m_new

    def reset():
        for x in range(2):
            m_refs[x][...] = jnp.full(m_refs[x].shape, NEG, f32)
            acc_refs[x][...] = jnp.zeros(acc_refs[x].shape, f32)

    reset()
    for item in items:
        scores_to(steps[0], item)
    for n, step in enumerate(steps):
        nxt = steps[n + 1] if n + 1 < len(steps) else None
        for item in items:
            update(step, item)
            if nxt is not None:
                scores_to(nxt, item)
        if step[1] == step[0]:
            finish(step[0], acc_a[...], acc_b[...])
            if nxt is not None:
                reset()


def _fox_kernel(q_ref, k_ref, v_ref, cum_ref, mask_ref, o_ref, ka_s, kb_s, vta_s, vtb_s,
                *flash_scratch, seq, tq):
    hp = pl.program_id(1)
    lane = lax.broadcasted_iota(i32, (1, LANES), 1)

    def head_cums(cm):
        c_a = jnp.sum(jnp.where(lane == 2 * hp, cm, 0.0), axis=-1, keepdims=True)
        c_b = jnp.sum(jnp.where(lane == 2 * hp + 1, cm, 0.0), axis=-1, keepdims=True)
        return c_a, c_b

    for c in range(seq // 512):
        rows = pl.ds(c * 512, 512)
        kk = k_ref[0, rows, :].astype(f32)
        c_a, c_b = head_cums(cum_ref[0, rows, :])
        ka_s[rows, :] = _k_aug(kk, lane, True, c_a)
        kb_s[rows, :] = _k_aug(kk, lane, False, c_b)
        vt = v_ref[0, rows, :].astype(f32).T
        vta_s[:, rows] = _vt_aug(vt[:HEAD_DIM])
        vtb_s[:, rows] = _vt_aug(vt[HEAD_DIM:])

    def make_q(qi):
        rows = pl.ds(qi * tq, tq)
        q = q_ref[0, rows, :].astype(f32)
        c_a, c_b = head_cums(cum_ref[0, rows, :])
        return _q_aug(q, lane, True, c_a), _q_aug(q, lane, False, c_b)

    def finish(qi, acc_a, acc_b):
        o_a = acc_a[:HEAD_DIM] / acc_a[HEAD_DIM:HEAD_DIM + 1]
        o_b = acc_b[:HEAD_DIM] / acc_b[HEAD_DIM:HEAD_DIM + 1]
        o_ref[0, pl.ds(qi * tq, tq), :] = jnp.concatenate([o_a, o_b], axis=0).T.astype(bf16)

    _flash_sweep(make_q, seq // tq, ka_s, kb_s, vta_s, vtb_s, mask_ref, tq, finish, flash_scratch)


def _tail_mask(diag):
    diag_t = jnp.swapaxes(diag, -1, -2)
    pad = [(0, 0)] * (diag.ndim - 2) + [(ATT_WIDE - diag.shape[-1], 0), (0, 0)]
    return jnp.pad(diag_t, pad)


def _fox_attention(proj3, cum3):
    bsz, seq, _ = proj3.shape
    tq, tk = ATT_TQ, ATT_WIDE
    n_pairs = FOX_HEADS // 2
    r = jnp.arange(tq)
    mask = _tail_mask(jnp.where(r[:, None] >= r[None, :], 0.0, NEG).astype(f32))
    return pl.pallas_call(
        functools.partial(_fox_kernel, seq=seq, tq=tq),
        grid=(bsz, n_pairs),
        in_specs=[pl.BlockSpec((1, seq, LANES), lambda b, h: (b, 0, h)),
                  pl.BlockSpec((1, seq, LANES), lambda b, h: (b, 0, 4 + h)),
                  pl.BlockSpec((1, seq, LANES), lambda b, h: (b, 0, 8 + h)),
                  pl.BlockSpec((1, seq, LANES), lambda b, h: (b, 0, 0)),
                  pl.BlockSpec((tk, tq), lambda b, h: (0, 0))],
        out_specs=pl.BlockSpec((1, seq, LANES), lambda b, h: (b, 0, h)),
        out_shape=jax.ShapeDtypeStruct((bsz, seq, FOX_WIDTH), bf16),
        scratch_shapes=[pltpu.VMEM((seq, LANES), bf16), pltpu.VMEM((seq, LANES), bf16),
                        pltpu.VMEM((HEAD_DIM + ONES_ROWS, seq), bf16),
                        pltpu.VMEM((HEAD_DIM + ONES_ROWS, seq), bf16)]
        + _flash_scratch(tq, HEAD_DIM + ONES_ROWS),
        compiler_params=_cparams("parallel", "parallel"),
        name="fox_attn",
    )(proj3, proj3, proj3, cum3, mask)


def _diff_kernel(slope_ref, q_ref, k_ref, v_ref, mask_ref, lamv_ref, g_ref, o_ref,
                 ka_s, kb_s, vt_s, *flash_scratch, seq, tq, lam_init):
    h = pl.program_id(1)
    lane = lax.broadcasted_iota(i32, (1, LANES), 1)
    slope = slope_ref[h]

    def pos_bias(start, n):
        pos = (start + lax.broadcasted_iota(i32, (n, 1), 0)).astype(f32)
        return -(slope * pos)

    for c in range(seq // 512):
        rows = pl.ds(c * 512, 512)
        kk = k_ref[0, rows, :].astype(f32)
        cb = pos_bias(c * 512, 512)
        ka_s[rows, :] = _k_aug(kk, lane, True, cb)
        kb_s[rows, :] = _k_aug(kk, lane, False, cb)
        vt_s[:, rows] = _vt_aug(v_ref[0, rows, :].astype(f32).T)

    def make_q(qi):
        q = q_ref[0, pl.ds(qi * tq, tq), :].astype(f32)
        cq = pos_bias(qi * tq, tq)
        return _q_aug(q, lane, True, cq), _q_aug(q, lane, False, cq)

    def finish(qi, acc_a, acc_b):
        lv = lamv_ref[...]
        s1 = jnp.sum(lv[0:1, :] * lv[1:2, :], axis=-1, keepdims=True)
        s2 = jnp.sum(lv[2:3, :] * lv[3:4, :], axis=-1, keepdims=True)
        lam = jnp.exp(s1) - jnp.exp(s2) + lam_init
        o_a = acc_a[:LANES] / acc_a[LANES:LANES + 1]
        o_b = acc_b[:LANES] / acc_b[LANES:LANES + 1]
        d = (o_a - lam * o_b).T
        y = d * lax.rsqrt(jnp.mean(d * d, axis=-1, keepdims=True) + SUBLN_EPS) * g_ref[...]
        o_ref[0, pl.ds(qi * tq, tq), :] = (y * (1.0 - lam_init)).astype(bf16)

    _flash_sweep(make_q, seq // tq, ka_s, kb_s, vt_s, vt_s, mask_ref.at[0], tq, finish, flash_scratch)


def _diff_attention(proj3, lam_vecs, subln_g, lam_init):
    bsz, seq, _ = proj3.shape
    tq, tk = ATT_TQ, ATT_WIDE
    slopes = jnp.asarray([2.0 ** (-8.0 * (i + 1) / DIFF_HEADS) for i in range(DIFF_HEADS)], f32) * LOG2E
    r = jnp.arange(tq)
    tq_i, tk_i = r[:, None], r[None, :]
    chunk_ok = (tq_i // CHUNK) >= (tk_i // CHUNK)
    ahead = jnp.maximum(tk_i - tq_i, 0).astype(f32)
    mask = _tail_mask(jnp.where(chunk_ok[None], -2.0 * slopes[:, None, None] * ahead[None], NEG).astype(f32))
    grid_spec = pltpu.PrefetchScalarGridSpec(
        num_scalar_prefetch=1,
        grid=(bsz, DIFF_HEADS),
        in_specs=[pl.BlockSpec((1, seq, LANES), lambda b, h, s: (b, 0, 12 + h)),
                  pl.BlockSpec((1, seq, LANES), lambda b, h, s: (b, 0, 16 + h)),
                  pl.BlockSpec((1, seq, LANES), lambda b, h, s: (b, 0, 20 + h)),
                  pl.BlockSpec((1, tk, tq), lambda b, h, s: (h, 0, 0)),
                  pl.BlockSpec((8, LANES), lambda b, h, s: (0, 0)),
                  pl.BlockSpec((1, LANES), lambda b, h, s: (0, 0))],
        out_specs=pl.BlockSpec((1, seq, LANES), lambda b, h, s: (b, 0, h)),
        scratch_shapes=[pltpu.VMEM((seq, LANES), bf16), pltpu.VMEM((seq, LANES), bf16),
                        pltpu.VMEM((LANES + ONES_ROWS, seq), bf16)]
        + _flash_scratch(tq, LANES + ONES_ROWS),
    )
    return pl.pallas_call(
        functools.partial(_diff_kernel, seq=seq, tq=tq, lam_init=lam_init),
        grid_spec=grid_spec,
        out_shape=jax.ShapeDtypeStruct((bsz, seq, DIFF_WIDTH), bf16),
        compiler_params=_cparams("parallel", "parallel"),
        name="diff_attn",
    )(slopes, proj3, proj3, proj3, mask, lam_vecs, subln_g)


ROUTER_ROWS = 8 + N_EXPERTS


def _outproj_kernel(fox_ref, diff_ref, x_ref, mod_ref, g_ref, wo_ref, wr_ref,
                    br_ref, triu_ref, x1_ref, hp_ref, route_ref, wcol_ref, cnt_ref):
    tm = x_ref.shape[0]
    y = jnp.dot(fox_ref[...], wo_ref[0:FOX_WIDTH, :], preferred_element_type=f32)
    y += jnp.dot(diff_ref[...], wo_ref[FOX_WIDTH:, :], preferred_element_type=f32)
    x1 = x_ref[...] + mod_ref[0, 2:3, :] * y
    x1_ref[...] = x1
    ms = jnp.mean(x1 * x1, axis=-1, keepdims=True)
    h2 = (x1 * lax.rsqrt(ms + NORM_EPS) * g_ref[...]) * (1.0 + mod_ref[0, 4:5, :]) + mod_ref[0, 3:4, :]
    _store_slabs(hp_ref, _pack_rows(h2))

    nt = (((1,), (1,)), ((), ()))
    h_hi = h2.astype(bf16)
    h_lo = (h2 - h_hi.astype(f32)).astype(bf16)
    stacked = lax.dot_general(wr_ref[...], h_hi, nt, preferred_element_type=f32)
    logits = stacked[:ROUTER_ROWS] + stacked[ROUTER_ROWS:]
    logits += lax.dot_general(wr_ref[:ROUTER_ROWS, :], h_lo, nt, preferred_element_type=f32)
    logits = logits + br_ref[...]

    row8 = lax.broadcasted_iota(i32, (8, tm), 0)
    gl = jnp.where(row8 < N_GROUPS, logits[0:8, :], NEG)
    gmax = jnp.max(gl, axis=0, keepdims=True)
    grp = jnp.min(jnp.where(gl == gmax, row8, 8), axis=0, keepdims=True)
    p_g = 1.0 / jnp.sum(jnp.exp(gl - gmax), axis=0, keepdims=True)
    sel = logits[8:16, :]
    for g in range(1, N_GROUPS):
        sel = jnp.where(grp == g, logits[8 + 8 * g:16 + 8 * g, :], sel)
    v1 = jnp.max(sel, axis=0, keepdims=True)
    i1 = jnp.min(jnp.where(sel == v1, row8, 8), axis=0, keepdims=True)
    sel2 = jnp.where(row8 == i1, -jnp.inf, sel)
    v2 = jnp.max(sel2, axis=0, keepdims=True)
    i2 = jnp.min(jnp.where(sel2 == v2, row8, 8), axis=0, keepdims=True)
    e21 = jnp.exp(v2 - v1)
    w1 = p_g / (1.0 + e21)
    w2 = w1 * e21
    e1 = grp * EXPERTS_PER_GROUP + i1
    e2 = grp * EXPERTS_PER_GROUP + i2

    row32 = lax.broadcasted_iota(i32, (N_EXPERTS, tm), 0)
    oh1 = row32 == e1
    oh2 = row32 == e2
    both = jnp.where(oh1 | oh2, 1.0, 0.0)
    prefix = jnp.dot(both.astype(bf16), triu_ref[...], preferred_element_type=f32)
    r1 = jnp.sum(jnp.where(oh1, prefix, 0.0), axis=0, keepdims=True).astype(i32)
    r2 = jnp.sum(jnp.where(oh2, prefix, 0.0), axis=0, keepdims=True).astype(i32)
    route = jnp.where(row8 == 0, e1, jnp.where(row8 == 1, e2,
                      jnp.where(row8 == 2, r1, jnp.where(row8 == 3, r2, 0))))
    route_ref[...] = route
    cnt = jnp.sum(both, axis=1, keepdims=True)
    cnt_ref[0] = jnp.broadcast_to(cnt, (N_EXPERTS, LANES)).astype(i32)
    row128 = lax.broadcasted_iota(i32, (LANES, tm), 0)
    w_rows = jnp.where(row128 == 0, w1, jnp.where(row128 == 1, w2, 0.0))
    wcol_ref[...] = w_rows.T


def _outproj(fox2, diff2, x2, mod, g2, w_o, wr_parts, b_r, seq):
    t, d = x2.shape
    tm = ROW_TILE
    tiles_per_batch = seq // tm
    n_tiles = t // tm
    triu = (jnp.arange(tm)[:, None] < jnp.arange(tm)[None, :]).astype(bf16)
    row = lambda i: (i, 0)
    const = lambda i: (0, 0)
    return pl.pallas_call(
        _outproj_kernel,
        grid=(n_tiles,),
        in_specs=[pl.BlockSpec((tm, FOX_WIDTH), row),
                  pl.BlockSpec((tm, DIFF_WIDTH), row),
                  pl.BlockSpec((tm, d), row),
                  pl.BlockSpec((1, 6, d), lambda i: (i // tiles_per_batch, 0, 0)),
                  pl.BlockSpec((1, d), const),
                  pl.BlockSpec((d, d), const),
                  pl.BlockSpec((2 * ROUTER_ROWS, d), const),
                  pl.BlockSpec((ROUTER_ROWS, 1), const),
                  pl.BlockSpec((tm, tm), const)],
        out_specs=[pl.BlockSpec((tm, d), row),
                   pl.BlockSpec((ROW_CHUNKS * tm, LANES), row),
                   pl.BlockSpec((8, tm), lambda i: (0, i)),
                   pl.BlockSpec((tm, LANES), row),
                   pl.BlockSpec((1, N_EXPERTS, LANES), lambda i: (i, 0, 0))],
        out_shape=[jax.ShapeDtypeStruct((t, d), f32),
                   jax.ShapeDtypeStruct((ROW_CHUNKS * t, LANES), i32),
                   jax.ShapeDtypeStruct((8, t), i32),
                   jax.ShapeDtypeStruct((t, LANES), f32),
                   jax.ShapeDtypeStruct((n_tiles, N_EXPERTS, LANES), i32)],
        compiler_params=_cparams("parallel"),
        name="outproj_router",
    )(fox2, diff2, x2, mod, g2, w_o, wr_parts, b_r, triu)


def _dispatch_kernel(dest_ref, h_ref, xs_in_ref, xs_ref, sem, *, tm):
    del xs_in_ref
    i = pl.program_id(0)
    base = i * (2 * tm)

    def row_copy(r, k):
        d = dest_ref[base + 2 * r + k]
        src = h_ref.at[pl.ds(pl.multiple_of(ROW_CHUNKS * r, ROW_CHUNKS), ROW_CHUNKS), :]
        return pltpu.make_async_copy(src, xs_ref.at[d], sem)

    def issue(r, _):
        row_copy(r, 0).start()
        row_copy(r, 1).start()
        return 0

    lax.fori_loop(0, tm, issue, 0, unroll=8)

    def drain(r, _):
        row_copy(r, 0).wait()
        row_copy(r, 1).wait()
        return 0

    lax.fori_loop(0, tm, drain, 0, unroll=8)


def _dispatch(dest, h_packed, n_slots):
    t = h_packed.shape[0] // ROW_CHUNKS
    tm = DISPATCH_TILE
    xs0 = jnp.zeros((n_slots, ROW_CHUNKS, LANES), i32)
    grid_spec = pltpu.PrefetchScalarGridSpec(
        num_scalar_prefetch=1,
        grid=(t // tm,),
        in_specs=[pl.BlockSpec((ROW_CHUNKS * tm, LANES), lambda i, d: (i, 0)),
                  pl.BlockSpec(memory_space=pl.ANY)],
        out_specs=pl.BlockSpec(memory_space=pl.ANY),
        scratch_shapes=[pltpu.SemaphoreType.DMA(())],
    )
    return pl.pallas_call(
        functools.partial(_dispatch_kernel, tm=tm),
        grid_spec=grid_spec,
        out_shape=jax.ShapeDtypeStruct((n_slots, ROW_CHUNKS, LANES), i32),
        input_output_aliases={2: 0},
        compiler_params=_cparams("arbitrary"),
        name="moe_dispatch",
    )(dest, h_packed, xs0)


def _experts_kernel(be_ref, nu_ref, xs_ref, wg_ref, wu_ref, wd_ref, y_ref, wg_s, wu_s, wd_s):
    i = pl.program_id(0)
    prev = be_ref[jnp.maximum(i - 1, 0)]
    fresh = jnp.logical_or(i == 0, be_ref[i] != prev)

    @pl.when(jnp.logical_and(fresh, i < nu_ref[0]))
    def _():
        wg_s[...] = wg_ref[0].astype(bf16)
        wu_s[...] = wu_ref[0].astype(bf16)
        wd_s[...] = wd_ref[0].astype(bf16)

    @pl.when(i < nu_ref[0])
    def _():
        lo, hi = _unpack_rows(_load_slabs(xs_ref, MOE_BLOCK))
        xb = jnp.concatenate([lo.astype(bf16), hi.astype(bf16)], axis=-1)
        g = jnp.dot(xb, wg_s[...], preferred_element_type=f32)
        u = jnp.dot(xb, wu_s[...], preferred_element_type=f32)
        hid = (g * (1.0 / (1.0 + jnp.exp(-g)))) * u
        y = jnp.dot(hid.astype(bf16), wd_s[...], preferred_element_type=f32)
        _store_slabs(y_ref, _pack_rows(y))

    @pl.when(i >= nu_ref[0])
    def _():
        y_ref[...] = jnp.zeros_like(y_ref)


def _experts(block_e, n_used, xs, w_gate, w_up, w_down):
    n_slots = xs.shape[0]
    bk = MOE_BLOCK
    grid_spec = pltpu.PrefetchScalarGridSpec(
        num_scalar_prefetch=2,
        grid=(n_slots // bk,),
        in_specs=[pl.BlockSpec((ROW_CHUNKS * bk, LANES), lambda i, be, nu: (i, 0)),
                  pl.BlockSpec((1, D_MODEL, D_EXPERT), lambda i, be, nu: (be[i], 0, 0)),
                  pl.BlockSpec((1, D_MODEL, D_EXPERT), lambda i, be, nu: (be[i], 0, 0)),
                  pl.BlockSpec((1, D_EXPERT, D_MODEL), lambda i, be, nu: (be[i], 0, 0))],
        out_specs=pl.BlockSpec((ROW_CHUNKS * bk, LANES), lambda i, be, nu: (i, 0)),
        scratch_shapes=[pltpu.VMEM((D_MODEL, D_EXPERT), bf16),
                        pltpu.VMEM((D_MODEL, D_EXPERT), bf16),
                        pltpu.VMEM((D_EXPERT, D_MODEL), bf16)],
    )
    return pl.pallas_call(
        _experts_kernel,
        grid_spec=grid_spec,
        out_shape=jax.ShapeDtypeStruct((ROW_CHUNKS * n_slots, LANES), i32),
        compiler_params=_cparams("arbitrary"),
        name="moe_experts",
    )(block_e, n_used, xs.reshape(-1, LANES), w_gate, w_up, w_down)


def _combine_kernel(dest_ref, y_ref, x1_ref, wcol_ref, mod_ref, g_ref, o_ref, buf, sems, *, tm):
    i = pl.program_id(0)
    n = pl.num_programs(0)

    def row_copy(tile, slot, r, k):
        d = dest_ref[tile * (2 * tm) + 2 * r + k]
        dst = buf.at[slot, k, pl.ds(pl.multiple_of(ROW_CHUNKS * r, ROW_CHUNKS), ROW_CHUNKS), :]
        return pltpu.make_async_copy(y_ref.at[d], dst, sems.at[slot])

    def issue_tile(tile, slot):
        def body(r, _):
            row_copy(tile, slot, r, 0).start()
            row_copy(tile, slot, r, 1).start()
            return 0
        lax.fori_loop(0, tm, body, 0, unroll=8)

    @pl.when(i == 0)
    def _():
        issue_tile(0, 0)

    @pl.when(i + 1 < n)
    def _():
        issue_tile(i + 1, (i + 1) % 2)

    slot = i % 2

    def drain(r, _):
        row_copy(i, slot, r, 0).wait()
        row_copy(i, slot, r, 1).wait()
        return 0

    lax.fori_loop(0, tm, drain, 0, unroll=8)

    wc = wcol_ref[...]
    w0 = wc[:, 0:1]
    w1 = wc[:, 1:2]
    lo0, hi0 = _unpack_rows(_load_slabs(buf.at[slot, 0], tm))
    lo1, hi1 = _unpack_rows(_load_slabs(buf.at[slot, 1], tm))
    moe = jnp.concatenate([w0 * lo0 + w1 * lo1, w0 * hi0 + w1 * hi1], axis=-1)
    x = x1_ref[...] + mod_ref[0, 5:6, :] * moe
    ms = jnp.mean(x * x, axis=-1, keepdims=True)
    o_ref[...] = x * lax.rsqrt(ms + NORM_EPS) * g_ref[...]


def _combine(dest, y_packed, x1, wcol, mod, g_f, seq):
    t, d = x1.shape
    tm = GATHER_TILE
    tiles_per_batch = seq // tm
    grid_spec = pltpu.PrefetchScalarGridSpec(
        num_scalar_prefetch=1,
        grid=(t // tm,),
        in_specs=[pl.BlockSpec(memory_space=pl.ANY),
                  pl.BlockSpec((tm, d), lambda i, ds: (i, 0)),
                  pl.BlockSpec((tm, LANES), lambda i, ds: (i, 0)),
                  pl.BlockSpec((1, 6, d), lambda i, ds: (i // tiles_per_batch, 0, 0)),
                  pl.BlockSpec((1, d), lambda i, ds: (0, 0))],
        out_specs=pl.BlockSpec((tm, d), lambda i, ds: (i, 0)),
        scratch_shapes=[pltpu.VMEM((2, 2, ROW_CHUNKS * tm, LANES), i32),
                        pltpu.SemaphoreType.DMA((2,))],
    )
    return pl.pallas_call(
        functools.partial(_combine_kernel, tm=tm),
        grid_spec=grid_spec,
        out_shape=jax.ShapeDtypeStruct((t, d), f32),
        compiler_params=_cparams("arbitrary"),
        name="moe_combine",
    )(dest, y_packed.reshape(-1, ROW_CHUNKS, LANES), x1, wcol, mod, g_f)


def _route_tables(route, cnt, tm):
    n_tiles = cnt.shape[0]
    t = route.shape[1]
    bk = MOE_BLOCK
    n_blocks = (2 * t) // bk + N_EXPERTS
    ti = jnp.arange(n_tiles)
    ei = jnp.arange(N_EXPERTS)
    tile_base = jnp.sum(jnp.where((ti[:, None] > ti[None, :])[:, :, None], cnt[None], 0), axis=1)
    total = jnp.sum(cnt, axis=0)
    padded = (total + bk - 1) // bk * bk
    pends = jnp.sum(jnp.where(ei[None, :] <= ei[:, None], padded[None, :], 0), axis=1)
    base = (pends - padded)[None, :] + tile_base
    base_tok = jnp.broadcast_to(base[:, None, :], (n_tiles, tm, N_EXPERTS)).reshape(t, N_EXPERTS)

    def slot_base(e):
        return jnp.sum(jnp.where(ei[None, :] == e[:, None], base_tok, 0), axis=1)

    d0 = slot_base(route[0]) + route[2]
    d1 = slot_base(route[1]) + route[3]
    dest = jnp.stack([d0, d1], axis=1).reshape(-1).astype(i32)
    block_start = jnp.arange(n_blocks, dtype=i32) * bk
    block_e = jnp.clip(jnp.sum(pends[None, :] <= block_start[:, None], axis=1),
                       0, N_EXPERTS - 1).astype(i32)
    n_used = (pends[-1] // bk).astype(i32).reshape(1)
    return dest, block_e, n_used, n_blocks * bk


def _layer(x2, c, seq, l, ada_w, ada_b, norm1_g, w_in, b_f, lam_q1, lam_k1, lam_q2, lam_k2,
           subln_g, w_o, norm2_g, w_rg, b_rg, w_re, b_re, w_gate, w_up, w_down):
    t, d = x2.shape
    bsz = t // seq
    lam_init = 0.8 - 0.6 * math.exp(-0.3 * l)
    mod = _ada(c, ada_w, ada_b).reshape(bsz, 6, d)

    z0 = 3 * FOX_WIDTH
    w_main = jnp.concatenate([w_in[:, :z0], w_in[:, z0 + FOX_HEADS:]], axis=1).astype(bf16)
    w_fz = jnp.pad(w_in[:, z0:z0 + FOX_HEADS], ((0, 0), (0, LANES - FOX_HEADS))).astype(bf16)
    b_fz = jnp.pad(b_f, (0, LANES - FOX_HEADS)).reshape(1, LANES)
    proj, cum = _inproj(x2, mod, norm1_g.reshape(1, d), w_main, w_fz, b_fz, seq)
    proj3 = proj.reshape(bsz, seq, -1)

    fox = _fox_attention(proj3, cum.reshape(bsz, seq, LANES))
    lam_vecs = jnp.pad(jnp.stack([lam_q1, lam_k1, lam_q2, lam_k2]).astype(f32),
                       ((0, 4), (0, LANES - HEAD_DIM)))
    diff = _diff_attention(proj3, lam_vecs, subln_g.reshape(1, LANES), lam_init)

    w_r = jnp.concatenate([w_rg.T, jnp.zeros((8 - N_GROUPS, d), f32), w_re.T], axis=0)
    wr_hi = w_r.astype(bf16)
    wr_lo = (w_r - wr_hi.astype(f32)).astype(bf16)
    b_r = jnp.concatenate([b_rg, jnp.zeros((8 - N_GROUPS,), f32), b_re]).reshape(ROUTER_ROWS, 1)
    x1, h_packed, route, wcol, cnt = _outproj(
        fox.reshape(t, FOX_WIDTH), diff.reshape(t, DIFF_WIDTH), x2, mod, norm2_g.reshape(1, d),
        w_o.astype(bf16), jnp.concatenate([wr_hi, wr_lo], axis=0), b_r, seq)

    dest, block_e, n_used, n_slots = _route_tables(route, cnt[:, :, 0], ROW_TILE)
    xs = _dispatch(dest, h_packed, n_slots)
    y_packed = _experts(block_e, n_used, xs, w_gate, w_up, w_down)
    return dest, y_packed, x1, wcol, mod


def kernel(x, c, ada_w, ada_b, norm1_g, w_in, b_f, lam_q1, lam_k1, lam_q2, lam_k2, subln_g, w_o,
           norm2_g, w_rg, b_rg, w_re, b_re, w_gate, w_up, w_down, norm_f_g):
    bsz, seq, d = x.shape
    depth = ada_w.shape[0]
    assert depth == 1 and d == D_MODEL and seq % ROW_TILE == 0 and seq % ATT_WIDE == 0
    x2 = x.reshape(bsz * seq, d)
    dest, y_packed, x1, wcol, mod = _layer(
        x2, c, seq, 0, ada_w[0], ada_b[0], norm1_g[0], w_in[0], b_f[0], lam_q1[0], lam_k1[0],
        lam_q2[0], lam_k2[0], subln_g[0], w_o[0], norm2_g[0], w_rg[0], b_rg[0], w_re[0], b_re[0],
        w_gate[0], w_up[0], w_down[0])
    out = _combine(dest, y_packed, x1, wcol, mod, norm_f_g.reshape(1, d), seq)
    return out.reshape(bsz, seq, d)
```
